```python
import jax
import jax.numpy as jnp
from jax import lax
import numpy as np


D_MODEL = 2048
BATCH = 1
SEQ = 8192
DEPTH = 4

GRID_W = 64
CTX_LEN = 256

CONV_CH = 1024
CONV_WIDTH = 31
MLA_HEADS = 8
MLA_Q_RANK = 512
MLA_KV_RANK = 512
MLA_NOPE = 128
MLA_ROPE = 64
MLA_V = 128
ROPE_BASE = 10000.0
ATT_BLOCK = 128
NA_HEADS = 16
NA_HEAD_DIM = 128
NA_WIN_ROWS = 8
NA_WIN_COLS = 16
NA_QCOLS = 16
NA_BAND = NA_WIN_COLS + NA_QCOLS
N_EXPERTS = 16
EXPERT_FF = 1408
EC_CAPACITY_FACTOR = 2

N_EVEN = (DEPTH + 1) // 2
N_ODD = DEPTH // 2
IN_A = 2 * CONV_CH + MLA_Q_RANK + MLA_KV_RANK + MLA_ROPE
DN_ALPHA = (2 * DEPTH) ** 0.25
DN_BETA = (8 * DEPTH) ** -0.25
LN_EPS = 1e-5
RMS_EPS = 1e-6
NEG_INF = -1e30

kernel_name = 'hybrid_dit_conv_mla_na_ecmoe'


def layer_norm(x, g, b):
    xf = x.astype(jnp.float32)
    mu = jnp.mean(xf, axis=-1, keepdims=True)
    var = jnp.mean(jnp.square(xf - mu), axis=-1, keepdims=True)
    return ((xf - mu) * lax.rsqrt(var + LN_EPS) * g + b).astype(x.dtype)


def rms_norm(x, g):
    xf = x.astype(jnp.float32)
    return (xf * lax.rsqrt(jnp.mean(xf * xf, axis=-1, keepdims=True) + RMS_EPS) * g).astype(x.dtype)


def modulate(x, shift, scale):
    return x * (1.0 + scale[:, None, :]) + shift[:, None, :]


def post_norm(x, gate, y, g, b):
    return layer_norm(DN_ALPHA * x + gate[:, None, :] * y, g, b)


def axial_rope_tables(n_tok):
    t = jnp.arange(n_tok, dtype=jnp.int32)
    row = (t // GRID_W).astype(jnp.float32)
    col = (t % GRID_W).astype(jnp.float32)
    n_freq = MLA_ROPE // 4
    inv_freq = ROPE_BASE ** (-jnp.arange(n_freq, dtype=jnp.float32) / n_freq)
    ang = jnp.concatenate([row[:, None] * inv_freq, col[:, None] * inv_freq], axis=-1)
    return jnp.cos(ang), jnp.sin(ang)


def apply_rope(x, cos, sin):
    half = x.shape[-1] // 2
    x1, x2 = x[..., :half], x[..., half:]
    cs = cos[None, :, None, :].astype(x.dtype)
    sn = sin[None, :, None, :].astype(x.dtype)
    return jnp.concatenate([x1 * cs - x2 * sn, x1 * sn + x2 * cs], axis=-1)


def context_attention(q, k, v):
    scale = q.shape[-1] ** -0.5
    s = jnp.einsum('bqhd,bkhd->bhqk', q, k, preferred_element_type=jnp.float32) * scale
    p = jax.nn.softmax(s, axis=-1).astype(v.dtype)
    return jnp.einsum('bhqk,bkhd->bqhd', p, v)


def dense_latent_attention(q, k, v, kc, vc):
    B, L, H, dq = q.shape
    scale = dq ** -0.5
    k_all = jnp.concatenate([k, kc], axis=1)
    v_all = jnp.concatenate([v, vc], axis=1)
    n_blk = L // ATT_BLOCK
    q_blk = q.reshape(B, n_blk, ATT_BLOCK, H, dq).transpose(1, 0, 2, 3, 4)

    def one_block(qb):
        s = jnp.einsum('bqhd,bkhd->bhqk', qb, k_all, preferred_element_type=jnp.float32) * scale
        p = jax.nn.softmax(s, axis=-1).astype(v_all.dtype)
        return jnp.einsum('bhqk,bkhd->bqhd', p, v_all)

    o = lax.map(one_block, q_blk)
    return o.transpose(1, 0, 2, 3, 4).reshape(B, L, H, v.shape[-1])


def conformer_conv(u, w_dw, b_dw, cn_g, cn_b):
    a, gt = jnp.split(u, 2, axis=-1)
    h = a * jax.nn.sigmoid(gt)
    pad = CONV_WIDTH // 2
    h = lax.conv_general_dilated(h, w_dw[:, None, :].astype(h.dtype), window_strides=(1,),
                                 padding=[(pad, pad)], dimension_numbers=('NWC', 'WIO', 'NWC'),
                                 feature_group_count=CONV_CH) + b_dw
    return jax.nn.silu(layer_norm(h, cn_g, cn_b))


def even_mixer(h, hc, cos, sin, w_in, w_dw, b_dw, cn_g, cn_b, qa_g, kva_g, w_uq, w_ukv, w_o, ctx_out):
    B, L, _ = h.shape
    Lc = hc.shape[1]
    cuts = [2 * CONV_CH, 2 * CONV_CH + MLA_Q_RANK, 2 * CONV_CH + MLA_Q_RANK + MLA_KV_RANK]
    u, cq, ckv, kr = jnp.split(h @ w_in, cuts, axis=-1)
    uc, cqc, ckvc, krc = jnp.split(hc @ w_in, cuts, axis=-1)

    def queries(cq_, rope):
        b_, l_, _ = cq_.shape
        q = (rms_norm(cq_, qa_g) @ w_uq).reshape(b_, l_, MLA_HEADS, MLA_NOPE + MLA_ROPE)
        q_nope, q_rope = q[..., :MLA_NOPE], q[..., MLA_NOPE:]
        if rope:
            q_rope = apply_rope(q_rope, cos, sin)
        return jnp.concatenate([q_nope, q_rope], axis=-1)

    def keys_values(ckv_, kr_, rope):
        b_, l_, _ = ckv_.shape
        kv = (rms_norm(ckv_, kva_g) @ w_ukv).reshape(b_, l_, MLA_HEADS, MLA_NOPE + MLA_V)
        k_nope, v = kv[..., :MLA_NOPE], kv[..., MLA_NOPE:]
        k_rope = kr_[:, :, None, :]
        if rope:
            k_rope = apply_rope(k_rope, cos, sin)
        k = jnp.concatenate([k_nope, jnp.broadcast_to(k_rope, (b_, l_, MLA_HEADS, MLA_ROPE))], axis=-1)
        return k, v

    q = queries(cq, True)
    k, v = keys_values(ckv, kr, True)
    kc, vc = keys_values(ckvc, krc, False)
    att = dense_latent_attention(q, k, v, kc, vc).reshape(B, L, MLA_HEADS * MLA_V)
    conv = conformer_conv(u, w_dw, b_dw, cn_g, cn_b)
    out = jnp.concatenate([conv, att], axis=-1) @ w_o
    out_c = None
    if ctx_out:
        att_c = context_attention(queries(cqc, False), kc, vc).reshape(B, Lc, MLA_HEADS * MLA_V)
        conv_c = conformer_conv(uc, w_dw, b_dw, cn_g, cn_b)
        out_c = jnp.concatenate([conv_c, att_c], axis=-1) @ w_o
    return out, out_c


def neighbourhood_attention(q, k, v, kc, vc, rpb):
    B, L, H, dh = q.shape
    rows = L // GRID_W
    kr = min(NA_WIN_ROWS, rows)
    n_cb = GRID_W // NA_QCOLS
    scale = dh ** -0.5
    cb = np.arange(n_cb)
    band_start = np.clip(cb * NA_QCOLS - NA_WIN_COLS // 2, 0, GRID_W - NA_BAND)
    band_cols = band_start[:, None] + np.arange(NA_BAND)
    qcol = cb[:, None] * NA_QCOLS + np.arange(NA_QCOLS)
    win_start = np.clip(qcol - NA_WIN_COLS // 2, 0, GRID_W - NA_WIN_COLS)
    kcol = band_cols[:, None, :]
    col_valid = jnp.asarray((kcol >= win_start[..., None]) & (kcol < win_start[..., None] + NA_WIN_COLS))
    col_idx = jnp.asarray(np.clip(kcol - qcol[..., None] + NA_WIN_COLS - 1, 0, 2 * NA_WIN_COLS - 2))

    kg = k.reshape(B, rows, GRID_W, H, dh)
    vg = v.reshape(B, rows, GRID_W, H, dh)
    qg = q.reshape(B, rows, n_cb, NA_QCOLS, H, dh).transpose(1, 0, 2, 3, 4, 5)
    r_idx = jnp.arange(rows, dtype=jnp.int32)

    def one_row(args):
        qr, r = args
        start = jnp.clip(r - kr // 2, 0, rows - kr)
        krow = lax.dynamic_slice_in_dim(kg, start, kr, axis=1)
        vrow = lax.dynamic_slice_in_dim(vg, start, kr, axis=1)
        kb = krow[:, :, band_cols]
        vb = vrow[:, :, band_cols]
        s = jnp.einsum('bnqhd,binkhd->bhnqik', qr, kb, preferred_element_type=jnp.float32) * scale
        dr = start + jnp.arange(kr, dtype=jnp.int32) - r + (NA_WIN_ROWS - 1)
        bias = rpb[:, dr][:, :, col_idx].transpose(0, 2, 3, 1, 4)
        s = jnp.where(col_valid[None, None, :, :, None, :], s + bias[None].astype(jnp.float32), NEG_INF)
        s = s.reshape(B, H, n_cb, NA_QCOLS, kr * NA_BAND)
        sc = jnp.einsum('bnqhd,bchd->bhnqc', qr, kc, preferred_element_type=jnp.float32) * scale
        p = jax.nn.softmax(jnp.concatenate([s, sc], axis=-1), axis=-1).astype(v.dtype)
        p_lat = p[..., :kr * NA_BAND].reshape(B, H, n_cb, NA_QCOLS, kr, NA_BAND)
        p_ctx = p[..., kr * NA_BAND:]
        return (jnp.einsum('bhnqik,binkhd->bnqhd', p_lat, vb)
                + jnp.einsum('bhnqc,bchd->bnqhd', p_ctx, vc))

    o = lax.map(one_row, (qg, r_idx))
    return o.transpose(1, 0, 2, 3, 4, 5).reshape(B, L, H, dh)


def odd_mixer(h, hc, w_qkv, rpb, w_o, ctx_out):
    B, L, _ = h.shape
    Lc = hc.shape[1]
    qkv = (h @ w_qkv).reshape(B, L, 3, NA_HEADS, NA_HEAD_DIM)
    qkvc = (hc @ w_qkv).reshape(B, Lc, 3, NA_HEADS, NA_HEAD_DIM)
    kc, vc = qkvc[:, :, 1], qkvc[:, :, 2]
    o = neighbourhood_attention(qkv[:, :, 0], qkv[:, :, 1], qkv[:, :, 2], kc, vc, rpb)
    out = o.reshape(B, L, NA_HEADS * NA_HEAD_DIM) @ w_o
    out_c = None
    if ctx_out:
        oc = context_attention(qkvc[:, :, 0], kc, vc)
        out_c = oc.reshape(B, Lc, NA_HEADS * NA_HEAD_DIM) @ w_o
    return out, out_c


def expert_choice_moe(h, w_router, w_gate, w_up, w_down):
    B, n, D = h.shape
    cap = max(1, EC_CAPACITY_FACTOR * n // N_EXPERTS)
    logits = jnp.einsum('bnd,de->bne', h, w_router, preferred_element_type=jnp.float32)
    aff = jax.nn.softmax(logits, axis=-1)
    gate, idx = lax.top_k(aff.transpose(0, 2, 1), cap)
    bidx = jnp.arange(B, dtype=jnp.int32)[:, None, None]
    xin = h[bidx, idx]
    a = jnp.einsum('becd,edf->becf', xin, w_gate)
    u = jnp.einsum('becd,edf->becf', xin, w_up)
    y = jnp.einsum('becf,efd->becd', jax.nn.silu(a) * u, w_down) * gate[..., None].astype(h.dtype)
    return jnp.zeros_like(h).at[bidx, idx].add(y)


def setup_inputs(seed: int = 0) -> dict:
    key = jax.random.key(seed)
    ks = iter(jax.random.split(key, 40))
    D = D_MODEL

    def nrm(shape, scale):
        return jax.random.normal(next(ks), shape, jnp.float32) * scale

    return {
        'x': nrm((BATCH, SEQ, D), 1.0),
        'c': nrm((BATCH, D), 1.0),
        'ctx': nrm((BATCH, CTX_LEN, D), 1.0),
        'c_ctx': nrm((D,), 1.0),
        'w_ada': nrm((DEPTH, D, 6 * D), 0.5 * D ** -0.5),
        'b_ada': nrm((DEPTH, 6 * D), 0.02),
        'ln1_g': 1.0 + nrm((DEPTH, D), 0.05),
        'ln1_b': nrm((DEPTH, D), 0.02),
        'ln2_g': 1.0 + nrm((DEPTH, D), 0.05),
        'ln2_b': nrm((DEPTH, D), 0.02),
        'a_w_in': nrm((N_EVEN, D, IN_A), D ** -0.5),
        'a_w_dw': nrm((N_EVEN, CONV_WIDTH, CONV_CH), CONV_WIDTH ** -0.5),
        'a_b_dw': nrm((N_EVEN, CONV_CH), 0.02),
        'a_cn_g': 1.0 + nrm((N_EVEN, CONV_CH), 0.05),
        'a_cn_b': nrm((N_EVEN, CONV_CH), 0.02),
        'b_qa_g': 1.0 + nrm((N_EVEN, MLA_Q_RANK), 0.05),
        'b_kva_g': 1.0 + nrm((N_EVEN, MLA_KV_RANK), 0.05),
        'b_w_uq': nrm((N_EVEN, MLA_Q_RANK, MLA_HEADS * (MLA_NOPE + MLA_ROPE)), MLA_Q_RANK ** -0.5),
        'b_w_ukv': nrm((N_EVEN, MLA_KV_RANK, MLA_HEADS * (MLA_NOPE + MLA_V)), MLA_KV_RANK ** -0.5),
        'ab_w_o': nrm((N_EVEN, CONV_CH + MLA_HEADS * MLA_V, D), DN_BETA * (CONV_CH + MLA_HEADS * MLA_V) ** -0.5),
        'c_w_qkv': nrm((N_ODD, D, 3 * NA_HEADS * NA_HEAD_DIM), D ** -0.5),
        'c_rpb': nrm((N_ODD, NA_HEADS, 2 * NA_WIN_ROWS - 1, 2 * NA_WIN_COLS - 1), 0.1),
        'c_w_o': nrm((N_ODD, NA_HEADS * NA_HEAD_DIM, D), DN_BETA * (NA_HEADS * NA_HEAD_DIM) ** -0.5),
        'moe_w_router': nrm((DEPTH, D, N_EXPERTS), D ** -0.5),
        'moe_w_gate': nrm((DEPTH, N_EXPERTS, D, EXPERT_FF), D ** -0.5),
        'moe_w_up': nrm((DEPTH, N_EXPERTS, D, EXPERT_FF), D ** -0.5),
        'moe_w_down': nrm((DEPTH, N_EXPERTS, EXPERT_FF, D), DN_BETA * EXPERT_FF ** -0.5),
    }


def reference(x, c, ctx, c_ctx, w_ada, b_ada, ln1_g, ln1_b, ln2_g, ln2_b,
              a_w_in, a_w_dw, a_b_dw, a_cn_g, a_cn_b, b_qa_g, b_kva_g, b_w_uq, b_w_ukv, ab_w_o,
              c_w_qkv, c_rpb, c_w_o, moe_w_router, moe_w_gate, moe_w_up, moe_w_down):
    n_tok = x.shape[1]
    cos, sin = axial_rope_tables(n_tok)
    xc = ctx
    for layer in range(DEPTH):
        ctx_out = layer < DEPTH - 1
        mod = jax.nn.silu(c) @ w_ada[layer] + b_ada[layer]
        mod_c = jax.nn.silu(c_ctx)[None] @ w_ada[layer] + b_ada[layer]
        sh1, sc1, g1, sh2, sc2, g2 = jnp.split(mod, 6, axis=-1)
        sh1c, sc1c, g1c, sh2c, sc2c, g2c = jnp.split(mod_c, 6, axis=-1)
        h = modulate(x, sh1, sc1)
        hc = modulate(xc, sh1c, sc1c)
        j = layer // 2
        if layer % 2 == 0:
            o, oc = even_mixer(h, hc, cos, sin, a_w_in[j], a_w_dw[j], a_b_dw[j], a_cn_g[j], a_cn_b[j],
                               b_qa_g[j], b_kva_g[j], b_w_uq[j], b_w_ukv[j], ab_w_o[j], ctx_out)
        else:
            o, oc = odd_mixer(h, hc, c_w_qkv[j], c_rpb[j], c_w_o[j], ctx_out)
        x = post_norm(x, g1, o, ln1_g[layer], ln1_b[layer])
        h = modulate(x, sh2, sc2)
        y = expert_choice_moe(h, moe_w_router[layer], moe_w_gate[layer], moe_w_up[layer], moe_w_down[layer])
        x = post_norm(x, g2, y, ln2_g[layer], ln2_b[layer])
        if ctx_out:
            xc = post_norm(xc, g1c, oc, ln1_g[layer], ln1_b[layer])
            hc = modulate(xc, sh2c, sc2c)
            yc = expert_choice_moe(hc, moe_w_router[layer], moe_w_gate[layer], moe_w_up[layer], moe_w_down[layer])
            xc = post_norm(xc, g2c, yc, ln2_g[layer], ln2_b[layer])
    return x
```

```python
import functools

import numpy as np
import jax
import jax.numpy as jnp
from jax import lax
from jax.experimental import pallas as pl
from jax.experimental.pallas import tpu as pltpu

F32 = jnp.float32
BF16 = jnp.bfloat16
I32 = jnp.int32

D_MODEL = 2048
SEQ = 8192
DEPTH = 4
GRID_W = 64
CTX_LEN = 256
M_ALL = SEQ + CTX_LEN

CONV_CH = 1024
CONV_WIDTH = 31
CONV_PAD = CONV_WIDTH // 2
MLA_HEADS = 8
MLA_Q_RANK = 512
MLA_KV_RANK = 512
MLA_NOPE = 128
MLA_ROPE = 64
MLA_V = 128
MLA_QK_PAD = 256
ROPE_BASE = 10000.0
NA_HEADS = 16
NA_HEAD_DIM = 128
NA_WIN_ROWS = 8
NA_WIN_COLS = 16
N_EXPERTS = 16
EXPERT_FF = 1408
EC_CAPACITY_FACTOR = 2
CAP_LAT = EC_CAPACITY_FACTOR * SEQ // N_EXPERTS
CAP_CTX = EC_CAPACITY_FACTOR * CTX_LEN // N_EXPERTS
CAP_ALL = CAP_LAT + CAP_CTX
IN_A = 2 * CONV_CH + MLA_Q_RANK + MLA_KV_RANK + MLA_ROPE
IN_A_PAD = 3200
DN_ALPHA = (2 * DEPTH) ** 0.25
LN_EPS = 1e-5
RMS_EPS = 1e-6
NEG_INF = -1e30

LANES = 128
SUBLANES = 8
VMEM_LIMIT = 56 * 1024 * 1024

ROW_TILE = 256
N_LAT_TILES = SEQ // ROW_TILE
N_ROW_TILES = M_ALL // ROW_TILE
MM_ROW_TILE = 768
ROUTE_TILE = 128
N_ROUTE_TILES = M_ALL // ROUTE_TILE
N_ROUTE_LAT = SEQ // ROUTE_TILE
Y_ROWS = CAP_ALL + 32


def _cparams(sem, vmem=VMEM_LIMIT):
    return pltpu.CompilerParams(dimension_semantics=sem, vmem_limit_bytes=vmem)


def _nt_dot(a, b):
    return lax.dot_general(a, b, (((1,), (1,)), ((), ())), preferred_element_type=F32)


def _tn_dot(a, b):
    return lax.dot_general(a, b, (((0,), (0,)), ((), ())), preferred_element_type=F32)


def _ada_kernel(c_ref, w_ref, b_ref, o_ref):
    c = c_ref[...]
    a = (c * jax.nn.sigmoid(c)).astype(BF16)
    o_ref[0] = jnp.dot(a, w_ref[0].astype(BF16), preferred_element_type=F32) + b_ref[0]


def ada_modulation(cc, w_ada, b_ada):
    depth, d, n = w_ada.shape
    tn = 1024
    return pl.pallas_call(
        _ada_kernel,
        out_shape=jax.ShapeDtypeStruct((depth, SUBLANES, n), F32),
        grid=(depth, n // tn),
        in_specs=[
            pl.BlockSpec((SUBLANES, d), lambda l, j: (0, 0)),
            pl.BlockSpec((1, d, tn), lambda l, j: (l, 0, j)),
            pl.BlockSpec((1, 1, tn), lambda l, j: (l, 0, j)),
        ],
        out_specs=pl.BlockSpec((1, SUBLANES, tn), lambda l, j: (l, 0, j)),
        compiler_params=_cparams(("arbitrary", "arbitrary")),
        name="ada_modulation",
    )(cc, w_ada, b_ada.reshape(depth, 1, n))


def _stream_vec_spec():
    return pl.BlockSpec((1, 1, D_MODEL), lambda i: (jnp.minimum(i // N_LAT_TILES, 1), 0, 0))


def _row_spec(width=D_MODEL):
    return pl.BlockSpec((ROW_TILE, width), lambda i: (i, 0))


def _modulate_kernel(x_ref, sh_ref, sc_ref, h_ref):
    h_ref[...] = (x_ref[...] * (1.0 + sc_ref[0]) + sh_ref[0]).astype(h_ref.dtype)


def modulate_rows(x, sh, sc):
    return pl.pallas_call(
        _modulate_kernel,
        out_shape=jax.ShapeDtypeStruct(x.shape, BF16),
        grid=(N_ROW_TILES,),
        in_specs=[_row_spec(), _stream_vec_spec(), _stream_vec_spec()],
        out_specs=_row_spec(),
        compiler_params=_cparams(("arbitrary",)),
        name="modulate_rows",
    )(x, sh, sc)


def _post_norm_kernel(*refs, with_router):
    if with_router:
        x_ref, y_ref, gate_ref, lg_ref, lb_ref, sh_ref, sc_ref, wr_ref, xo_ref, h_ref, aff_ref = refs
    else:
        x_ref, y_ref, gate_ref, lg_ref, lb_ref, sh_ref, sc_ref, xo_ref, h_ref = refs
    z = DN_ALPHA * x_ref[...] + gate_ref[0] * y_ref[...]
    mu = jnp.mean(z, axis=-1, keepdims=True)
    zc = z - mu
    var = jnp.mean(zc * zc, axis=-1, keepdims=True)
    xn = zc * lax.rsqrt(var + LN_EPS) * lg_ref[...] + lb_ref[...]
    xo_ref[...] = xn
    h = (xn * (1.0 + sc_ref[0]) + sh_ref[0]).astype(BF16)
    h_ref[...] = h
    if with_router:
        logits = jnp.dot(h, wr_ref[...].astype(BF16), preferred_element_type=F32)
        lane = lax.broadcasted_iota(I32, logits.shape, 1)
        logits = jnp.where(lane < N_EXPERTS, logits, NEG_INF)
        e = jnp.exp(logits - jnp.max(logits, axis=-1, keepdims=True))
        aff_ref[...] = e / jnp.sum(e, axis=-1, keepdims=True)


def post_norm_rows(x, y, gate, ln_g, ln_b, sh, sc, w_router=None):
    with_router = w_router is not None
    vec = pl.BlockSpec((1, D_MODEL), lambda i: (0, 0))
    in_specs = [_row_spec(), _row_spec(), _stream_vec_spec(), vec, vec, _stream_vec_spec(), _stream_vec_spec()]
    out_shape = [jax.ShapeDtypeStruct(x.shape, F32), jax.ShapeDtypeStruct(x.shape, BF16)]
    out_specs = [_row_spec(), _row_spec()]
    args = [x, y, gate, ln_g.reshape(1, -1), ln_b.reshape(1, -1), sh, sc]
    if with_router:
        in_specs.append(pl.BlockSpec((D_MODEL, LANES), lambda i: (0, 0)))
        out_shape.append(jax.ShapeDtypeStruct((x.shape[0], LANES), F32))
        out_specs.append(_row_spec(LANES))
        args.append(jnp.pad(w_router, ((0, 0), (0, LANES - N_EXPERTS))))
    return pl.pallas_call(
        functools.partial(_post_norm_kernel, with_router=with_router),
        out_shape=out_shape,
        grid=(N_ROW_TILES,),
        in_specs=in_specs,
        out_specs=out_specs,
        compiler_params=_cparams(("arbitrary",)),
        name="post_norm_router" if with_router else "post_norm",
    )(*args)


def _matmul_kernel(*refs, splits):
    n_a = len(splits)
    a_refs, w_ref, o_ref, wb_ref = refs[:n_a], refs[n_a], refs[n_a + 1], refs[n_a + 2]

    @pl.when(pl.program_id(1) == 0)
    def _():
        wb_ref[...] = w_ref[...].astype(BF16)

    acc = None
    off = 0
    for a_ref, k in zip(a_refs, splits):
        part = jnp.dot(a_ref[...].astype(BF16), wb_ref[off:off + k, :], preferred_element_type=F32)
        acc = part if acc is None else acc + part
        off += k
    o_ref[...] = acc.astype(o_ref.dtype)


def matmul(a_list, w, out_dtype, tn, tm=MM_ROW_TILE):
    m = a_list[0].shape[0]
    splits = tuple(a.shape[1] for a in a_list)
    k, n = w.shape
    assert sum(splits) == k and n % tn == 0 and m % tm == 0
    in_specs = [pl.BlockSpec((tm, ki), lambda j, i: (i, 0)) for ki in splits]
    in_specs.append(pl.BlockSpec((k, tn), lambda j, i: (0, j)))
    return pl.pallas_call(
        functools.partial(_matmul_kernel, splits=splits),
        out_shape=jax.ShapeDtypeStruct((m, n), out_dtype),
        grid=(n // tn, m // tm),
        in_specs=in_specs,
        out_specs=pl.BlockSpec((tm, tn), lambda j, i: (i, j)),
        scratch_shapes=[pltpu.VMEM((k, tn), BF16)],
        compiler_params=_cparams(("arbitrary", "arbitrary")),
        name="matmul",
    )(*a_list, w)


def rope_tables():
    t = jnp.arange(SEQ, dtype=I32)
    row = (t // GRID_W).astype(F32)
    col = (t % GRID_W).astype(F32)
    n_freq = MLA_ROPE // 4
    inv_freq = ROPE_BASE ** (-jnp.arange(n_freq, dtype=F32) / n_freq)
    ang = jnp.concatenate([row[:, None] * inv_freq, col[:, None] * inv_freq], axis=-1)
    cos, sin = jnp.cos(ang), jnp.sin(ang)
    half = MLA_ROPE // 2
    cos = jnp.concatenate([cos, jnp.ones((CTX_LEN, half), F32)], axis=0)
    sin = jnp.concatenate([sin, jnp.zeros((CTX_LEN, half), F32)], axis=0)
    z = jnp.zeros_like(cos)
    c_tab = jnp.concatenate([cos, cos, z, z], axis=1)
    sa_tab = jnp.concatenate([z, sin, z, z], axis=1)
    sb_tab = jnp.concatenate([-sin, z, z, z], axis=1)
    return c_tab, sa_tab, sb_tab


def _mla_proj_kernel(cq_ref, ckv_ref, kr_ref, qg_ref, kvg_ref, wq_ref, wkv_ref, c_ref, sa_ref, sb_ref,
                     q_out, k_out, v_out, wqb, wkvb):
    @pl.when(pl.program_id(0) == 0)
    def _():
        wqb[...] = wq_ref[...].astype(BF16)
        wkvb[...] = wkv_ref[...].astype(BF16)

    def rms(x, g):
        return x * lax.rsqrt(jnp.mean(x * x, axis=-1, keepdims=True) + RMS_EPS) * g

    c_tab, sa_tab, sb_tab = c_ref[...], sa_ref[...], sb_ref[...]
    half = MLA_ROPE // 2

    def rope(g):
        return g * c_tab + pltpu.roll(g, half, 1) * sa_tab + pltpu.roll(g, LANES - half, 1) * sb_tab

    qk_scale = (MLA_NOPE + MLA_ROPE) ** -0.5
    q = jnp.dot(rms(cq_ref[...], qg_ref[...]).astype(BF16), wqb[...], preferred_element_type=F32) * qk_scale
    kv = jnp.dot(rms(ckv_ref[...], kvg_ref[...]).astype(BF16), wkvb[...], preferred_element_type=F32)
    k_rope = rope(kr_ref[...]).astype(BF16)
    for h in range(MLA_HEADS):
        lo = h * MLA_QK_PAD
        q_out[:, lo:lo + LANES] = q[:, lo:lo + LANES].astype(BF16)
        q_out[:, lo + LANES:lo + 2 * LANES] = rope(q[:, lo + LANES:lo + 2 * LANES]).astype(BF16)
        k_out[:, lo:lo + LANES] = kv[:, h * MLA_NOPE:(h + 1) * MLA_NOPE].astype(BF16)
        k_out[:, lo + LANES:lo + 2 * LANES] = k_rope
    v_out[...] = kv[:, MLA_HEADS * MLA_NOPE:].astype(BF16)


def mla_projections(u_all, qa_g, kva_g, w_uq, w_ukv, tabs):
    m = u_all.shape[0]
    tm = 384
    hq = MLA_HEADS * MLA_QK_PAD
    wq = w_uq.reshape(MLA_Q_RANK, MLA_HEADS, MLA_NOPE + MLA_ROPE)
    wq = jnp.pad(wq, ((0, 0), (0, 0), (0, MLA_QK_PAD - MLA_NOPE - MLA_ROPE))).reshape(MLA_Q_RANK, hq)
    wkv = w_ukv.reshape(MLA_KV_RANK, MLA_HEADS, MLA_NOPE + MLA_V)
    wkv = jnp.concatenate([wkv[..., :MLA_NOPE].reshape(MLA_KV_RANK, -1), wkv[..., MLA_NOPE:].reshape(MLA_KV_RANK, -1)], axis=1)
    nkv = wkv.shape[1]
    cq_blk = 2 * CONV_CH // MLA_Q_RANK
    kr_blk = (2 * CONV_CH + MLA_Q_RANK + MLA_KV_RANK) // LANES
    tab_spec = pl.BlockSpec((tm, LANES), lambda i: (i, 0))
    return pl.pallas_call(
        _mla_proj_kernel,
        out_shape=[jax.ShapeDtypeStruct((m, hq), BF16), jax.ShapeDtypeStruct((m, hq), BF16),
                   jax.ShapeDtypeStruct((m, MLA_HEADS * MLA_V), BF16)],
        grid=(m // tm,),
        in_specs=[
            pl.BlockSpec((tm, MLA_Q_RANK), lambda i: (i, cq_blk)),
            pl.BlockSpec((tm, MLA_KV_RANK), lambda i: (i, cq_blk + 1)),
            pl.BlockSpec((tm, LANES), lambda i: (i, kr_blk)),
            pl.BlockSpec((1, MLA_Q_RANK), lambda i: (0, 0)),
            pl.BlockSpec((1, MLA_KV_RANK), lambda i: (0, 0)),
            pl.BlockSpec((MLA_Q_RANK, hq), lambda i: (0, 0)),
            pl.BlockSpec((MLA_KV_RANK, nkv), lambda i: (0, 0)),
            tab_spec, tab_spec, tab_spec,
        ],
        out_specs=[pl.BlockSpec((tm, hq), lambda i: (i, 0)), pl.BlockSpec((tm, hq), lambda i: (i, 0)),
                   pl.BlockSpec((tm, MLA_HEADS * MLA_V), lambda i: (i, 0))],
        scratch_shapes=[pltpu.VMEM((MLA_Q_RANK, hq), BF16), pltpu.VMEM((MLA_KV_RANK, nkv), BF16)],
        compiler_params=_cparams(("arbitrary",)),
        name="mla_projections",
    )(u_all, u_all, u_all, qa_g.reshape(1, -1), kva_g.reshape(1, -1), wq, wkv, *tabs)


def _flash_kernel(q_ref, k_ref, v_ref, o_ref, *, tk, n_chunks):
    q = q_ref[...]
    tq = q.shape[0]

    def body(c, carry):
        m, l, acc = carry
        r0 = pl.multiple_of(c * tk, tk)
        s = _nt_dot(q, k_ref[pl.ds(r0, tk), :])
        m_new = jnp.maximum(m, jnp.max(s, axis=-1, keepdims=True))
        alpha = jnp.exp(m - m_new)
        p = jnp.exp(s - m_new)
        l = alpha * l + jnp.sum(p, axis=-1, keepdims=True)
        acc = alpha * acc + jnp.dot(p.astype(BF16), v_ref[pl.ds(r0, tk), :], preferred_element_type=F32)
        return m_new, l, acc

    init = (jnp.full((tq, 1), NEG_INF, F32), jnp.zeros((tq, 1), F32), jnp.zeros((tq, v_ref.shape[1]), F32))
    _, l, acc = lax.fori_loop(0, n_chunks, body, init)
    o_ref[...] = (acc / l).astype(o_ref.dtype)


def mla_attention(q_cat, k_cat, v):
    m = k_cat.shape[0]
    tq, tk = 512, 768
    return pl.pallas_call(
        functools.partial(_flash_kernel, tk=tk, n_chunks=m // tk),
        out_shape=jax.ShapeDtypeStruct((SEQ, MLA_HEADS * MLA_V), BF16),
        grid=(MLA_HEADS, SEQ // tq),
        in_specs=[
            pl.BlockSpec((tq, MLA_QK_PAD), lambda h, i: (i, h)),
            pl.BlockSpec((m, MLA_QK_PAD), lambda h, i: (0, h)),
            pl.BlockSpec((m, MLA_V), lambda h, i: (0, h)),
        ],
        out_specs=pl.BlockSpec((tq, MLA_V), lambda h, i: (i, h)),
        compiler_params=_cparams(("arbitrary", "arbitrary")),
        name="mla_attention",
    )(q_cat, k_cat, v)


def _ctx_attn_kernel(q_ref, k_ref, v_ref, o_ref, *, scale):
    s = _nt_dot(q_ref[...], k_ref[...]) * scale
    p = jnp.exp(s - jnp.max(s, axis=-1, keepdims=True))
    l = jnp.sum(p, axis=-1, keepdims=True)
    o_ref[...] = (jnp.dot(p.astype(BF16), v_ref[...], preferred_element_type=F32) / l).astype(o_ref.dtype)


def context_attention(q_arr, k_arr, v_arr, heads, dq, dv, q_col, k_col, v_col, scale):
    rb = SEQ // CTX_LEN
    return pl.pallas_call(
        functools.partial(_ctx_attn_kernel, scale=scale),
        out_shape=jax.ShapeDtypeStruct((CTX_LEN, heads * dv), BF16),
        grid=(heads,),
        in_specs=[
            pl.BlockSpec((CTX_LEN, dq), lambda h: (rb, q_col + h)),
            pl.BlockSpec((CTX_LEN, dq), lambda h: (rb, k_col + h)),
            pl.BlockSpec((CTX_LEN, dv), lambda h: (rb, v_col + h)),
        ],
        out_specs=pl.BlockSpec((CTX_LEN, dv), lambda h: (0, h)),
        compiler_params=_cparams(("arbitrary",)),
        name="context_attention",
    )(q_arr, k_arr, v_arr)


CONV_HALO = 16
CONV_ROW_BLOCK = 64


def _conv_kernel(a_ref, g_ref, ap_ref, gp_ref, an_ref, gn_ref, w_ref, b_ref, cg_ref, cb_ref, o_ref, hbuf, cbuf):
    i = pl.program_id(0)
    tl = a_ref.shape[0]

    def glu(a, g):
        return a[...] * jax.nn.sigmoid(g[...])

    has_prev = jnp.logical_and(i != 0, i != N_LAT_TILES)
    has_next = jnp.logical_and(i != N_LAT_TILES - 1, i != N_ROW_TILES - 1)
    hbuf[0:CONV_HALO, :] = jnp.where(has_prev, glu(ap_ref, gp_ref), 0.0)
    hbuf[CONV_HALO:CONV_HALO + tl, :] = glu(a_ref, g_ref)
    hbuf[CONV_HALO + tl:, :] = jnp.where(has_next, glu(an_ref, gn_ref), 0.0)

    base = CONV_HALO - CONV_PAD
    for rb in range(tl // CONV_ROW_BLOCK):
        r0 = rb * CONV_ROW_BLOCK
        for c in range(CONV_CH // LANES):
            cs = slice(c * LANES, (c + 1) * LANES)
            acc = jnp.broadcast_to(b_ref[:, cs], (CONV_ROW_BLOCK, LANES))
            for j in range(CONV_WIDTH):
                acc = acc + hbuf[r0 + base + j:r0 + base + j + CONV_ROW_BLOCK, cs] * w_ref[j:j + 1, cs]
            cbuf[r0:r0 + CONV_ROW_BLOCK, cs] = acc

    y = cbuf[...]
    mu = jnp.mean(y, axis=-1, keepdims=True)
    yc = y - mu
    var = jnp.mean(yc * yc, axis=-1, keepdims=True)
    yn = yc * lax.rsqrt(var + LN_EPS) * cg_ref[...] + cb_ref[...]
    o_ref[...] = (yn * jax.nn.sigmoid(yn)).astype(o_ref.dtype)


def conformer_conv(u_all, w_dw, b_dw, cn_g, cn_b):
    m = u_all.shape[0]
    tl = ROW_TILE
    hpt = tl // CONV_HALO
    n_halo = m // CONV_HALO
    main = lambda col: pl.BlockSpec((tl, CONV_CH), lambda i: (i, col))
    prev = lambda col: pl.BlockSpec((CONV_HALO, CONV_CH), lambda i: (jnp.maximum(i * hpt - 1, 0), col))
    nxt = lambda col: pl.BlockSpec((CONV_HALO, CONV_CH), lambda i: (jnp.minimum((i + 1) * hpt, n_halo - 1), col))
    vec = pl.BlockSpec((1, CONV_CH), lambda i: (0, 0))
    return pl.pallas_call(
        _conv_kernel,
        out_shape=jax.ShapeDtypeStruct((m, CONV_CH), BF16),
        grid=(m // tl,),
        in_specs=[main(0), main(1), prev(0), prev(1), nxt(0), nxt(1),
                  pl.BlockSpec((CONV_WIDTH, CONV_CH), lambda i: (0, 0)), vec, vec, vec],
        out_specs=pl.BlockSpec((tl, CONV_CH), lambda i: (i, 0)),
        scratch_shapes=[pltpu.VMEM((tl + 2 * CONV_HALO, CONV_CH), F32), pltpu.VMEM((tl, CONV_CH), F32)],
        compiler_params=_cparams(("arbitrary",)),
        name="conformer_conv",
    )(u_all, u_all, u_all, u_all, u_all, u_all, w_dw, b_dw.reshape(1, -1), cn_g.reshape(1, -1), cn_b.reshape(1, -1))


NA_Q_ROWS = 8
NA_K_ROWS = 16
NA_KBLK = 4 * GRID_W
NA_QTOK = NA_Q_ROWS * GRID_W
NA_KTOK = NA_K_ROWS * GRID_W
NA_GROUPS = SEQ // NA_QTOK


def na_bias_tables(rpb):
    rows = SEQ // GRID_W
    qc = np.arange(GRID_W)[:, None]
    kc = np.arange(GRID_W)[None, :]
    ws = np.clip(qc - NA_WIN_COLS // 2, 0, GRID_W - NA_WIN_COLS)
    col_valid = (kc >= ws) & (kc < ws + NA_WIN_COLS)
    col_idx = np.clip(kc - qc + NA_WIN_COLS - 1, 0, 2 * NA_WIN_COLS - 2)
    dr_all, rv_all = [], []
    for g in (0, 1, NA_GROUPS - 1):
        qr = (NA_Q_ROWS * g + np.arange(NA_Q_ROWS))[:, None]
        kr = (NA_Q_ROWS * g - (NA_K_ROWS - NA_Q_ROWS) // 2 + np.arange(NA_K_ROWS))[None, :]
        start = np.clip(qr - NA_WIN_ROWS // 2, 0, rows - NA_WIN_ROWS)
        rv_all.append((kr >= start) & (kr < start + NA_WIN_ROWS))
        dr_all.append(np.clip(kr - qr + NA_WIN_ROWS - 1, 0, 2 * NA_WIN_ROWS - 2))
    dr = np.stack(dr_all)
    rv = np.stack(rv_all)
    cmat = jnp.where(col_valid[None, None], rpb[:, :, col_idx], NEG_INF)
    b = cmat[:, dr]
    b = jnp.where(rv[None, :, :, :, None, None], b, NEG_INF)
    b = b.transpose(1, 0, 2, 4, 3, 5)
    return b.reshape(3, NA_HEADS, NA_QTOK, NA_KTOK)


def _na_kernel(q_ref, k0, k1, k2, k3, v0, v1, v2, v3, kc_ref, vc_ref, b_ref, o_ref):
    scale = NA_HEAD_DIM ** -0.5
    q = q_ref[...]
    s = jnp.concatenate([_nt_dot(q, k[...]) for k in (k0, k1, k2, k3)], axis=1) * scale + b_ref[...]
    sc = _nt_dot(q, kc_ref[...]) * scale
    m = jnp.maximum(jnp.max(s, axis=-1, keepdims=True), jnp.max(sc, axis=-1, keepdims=True))
    p = jnp.exp(s - m)
    pc = jnp.exp(sc - m)
    l = jnp.sum(p, axis=-1, keepdims=True) + jnp.sum(pc, axis=-1, keepdims=True)
    o = jnp.dot(pc.astype(BF16), vc_ref[...], preferred_element_type=F32)
    for j, v in enumerate((v0, v1, v2, v3)):
        o = o + jnp.dot(p[:, j * NA_KBLK:(j + 1) * NA_KBLK].astype(BF16), v[...], preferred_element_type=F32)
    o_ref[...] = (o / l).astype(o_ref.dtype)


def neighbourhood_attention(qkv, bias):
    n_kblk = SEQ // NA_KBLK
    ctx_blk = SEQ // CTX_LEN

    def kv_spec(j, col0):
        return pl.BlockSpec((NA_KBLK, NA_HEAD_DIM),
                            lambda h, g: (jnp.clip(2 * g - 1 + j, 0, n_kblk - 1), col0 + h))

    def cls(g):
        return jnp.where(g == 0, 0, jnp.where(g == NA_GROUPS - 1, 2, 1))

    return pl.pallas_call(
        _na_kernel,
        out_shape=jax.ShapeDtypeStruct((SEQ, NA_HEADS * NA_HEAD_DIM), BF16),
        grid=(NA_HEADS, NA_GROUPS),
        in_specs=[pl.BlockSpec((NA_QTOK, NA_HEAD_DIM), lambda h, g: (g, h))]
        + [kv_spec(j, NA_HEADS) for j in range(4)]
        + [kv_spec(j, 2 * NA_HEADS) for j in range(4)]
        + [pl.BlockSpec((CTX_LEN, NA_HEAD_DIM), lambda h, g: (ctx_blk, NA_HEADS + h)),
           pl.BlockSpec((CTX_LEN, NA_HEAD_DIM), lambda h, g: (ctx_blk, 2 * NA_HEADS + h)),
           pl.BlockSpec((None, None, NA_QTOK, NA_KTOK), lambda h, g: (cls(g), h, 0, 0))],
        out_specs=pl.BlockSpec((NA_QTOK, NA_HEAD_DIM), lambda h, g: (g, h)),
        compiler_params=_cparams(("arbitrary", "arbitrary")),
        name="neighbourhood_attention",
    )(*([qkv] * 11), bias)


def _select_kernel(aff_ref, posm_ref, *, cap):
    bits = pltpu.bitcast(aff_ref[...], I32)
    n = bits.shape[1]

    def search(i, thr):
        cand = thr | (jnp.int32(1) << (30 - i))
        cnt = jnp.sum((bits >= cand).astype(I32), axis=1, keepdims=True)
        return jnp.where(cnt >= cap, cand, thr)

    thr = lax.fori_loop(0, 31, search, jnp.zeros((N_EXPERTS, 1), I32))
    gt = bits > thr
    eq = bits == thr
    need = (cap - jnp.sum(gt.astype(I32), axis=1, keepdims=True)).astype(F32)
    tri = jnp.where(lax.broadcasted_iota(I32, (LANES, LANES), 0) <= lax.broadcasted_iota(I32, (LANES, LANES), 1),
                    1.0, 0.0).astype(BF16)
    off_eq = jnp.zeros((N_EXPERTS, 1), F32)
    off_sel = jnp.zeros((N_EXPERTS, 1), F32)
    for b in range(n // LANES):
        cs = slice(b * LANES, (b + 1) * LANES)
        eq_b = jnp.where(eq[:, cs], 1.0, 0.0)
        inc_eq = jnp.dot(eq_b.astype(BF16), tri, preferred_element_type=F32)
        rank = off_eq + inc_eq - eq_b
        sel_b = jnp.logical_or(gt[:, cs], jnp.logical_and(eq[:, cs], rank < need))
        sel_f = jnp.where(sel_b, 1.0, 0.0)
        inc_sel = jnp.dot(sel_f.astype(BF16), tri, preferred_element_type=F32)
        pos = off_sel + inc_sel - sel_f
        posm_ref[:, cs] = jnp.where(sel_b, pos.astype(I32), -1)
        off_eq = off_eq + inc_eq[:, LANES - 1:LANES]
        off_sel = off_sel + inc_sel[:, LANES - 1:LANES]


def expert_choice_select(aff_t, cap):
    return pl.pallas_call(
        functools.partial(_select_kernel, cap=cap),
        out_shape=jax.ShapeDtypeStruct(aff_t.shape, I32),
        compiler_params=_cparams(None),
        name="expert_choice_select",
    )(aff_t)


def route_tile_meta(posm_all):
    cnt = (posm_all >= 0).astype(I32).reshape(N_EXPERTS, N_ROUTE_TILES, ROUTE_TILE).sum(-1).T
    lat, ctx = cnt[:N_ROUTE_LAT], cnt[N_ROUTE_LAT:]
    s0 = jnp.concatenate([jnp.cumsum(lat, 0) - lat, CAP_LAT + jnp.cumsum(ctx, 0) - ctx], axis=0)
    return s0.reshape(-1), cnt.reshape(-1)


DISPATCH_PIECE = 16
DISPATCH_ROWS = N_EXPERTS * ROUTE_TILE
DISPATCH_BLOCK = 256


def _dispatch_kernel(s0_ref, cnt_ref, h_ref, posm_ref, xin_ref, pt_ref, x_ref, sem):
    t = pl.program_id(0)
    pt_ref[...] = jnp.zeros(pt_ref.shape, pt_ref.dtype)
    iota = lax.broadcasted_iota(I32, (DISPATCH_PIECE, ROUTE_TILE), 0)
    offs = []
    o = jnp.int32(0)
    for e in range(N_EXPERTS):
        n = cnt_ref[t * N_EXPERTS + e]
        s0 = s0_ref[t * N_EXPERTS + e]
        npc = (n + DISPATCH_PIECE - 1) // DISPATCH_PIECE
        row = posm_ref[e:e + 1, :]
        offs.append(o)

        def piece(p, carry, row=row, s0=s0, o=o):
            r0 = pl.multiple_of(o + p * DISPATCH_PIECE, DISPATCH_PIECE)
            hit = row == (s0 + p * DISPATCH_PIECE + iota)
            pt_ref[pl.ds(r0, DISPATCH_PIECE), :] = jnp.where(hit, 1.0, 0.0).astype(BF16)
            return carry

        lax.fori_loop(0, npc, piece, 0)
        o = o + npc * DISPATCH_PIECE

    def block(b, carry):
        r0 = pl.multiple_of(b * DISPATCH_BLOCK, DISPATCH_BLOCK)
        x_ref[pl.ds(r0, DISPATCH_BLOCK), :] = jnp.dot(pt_ref[pl.ds(r0, DISPATCH_BLOCK), :], h_ref[...],
                                                      preferred_element_type=F32)
        return carry

    lax.fori_loop(0, (o + DISPATCH_BLOCK - 1) // DISPATCH_BLOCK, block, 0)

    def row_copy(src_row, e, dst_row):
        return pltpu.make_async_copy(x_ref.at[pl.ds(src_row, 1)], xin_ref.at[e, pl.ds(dst_row, 1)], sem)

    total = jnp.int32(0)
    for e in range(N_EXPERTS):
        n = cnt_ref[t * N_EXPERTS + e]
        s0 = s0_ref[t * N_EXPERTS + e]

        def issue(j, carry, e=e, s0=s0, o=offs[e]):
            row_copy(o + j, e, s0 + j).start()
            return carry

        lax.fori_loop(0, n, issue, 0)
        total = total + n

    def drain(j, carry):
        row_copy(0, 0, 0).wait()
        return carry

    lax.fori_loop(0, total, drain, 0)


def moe_dispatch(h, posm_all, s0, cnt):
    return pl.pallas_call(
        _dispatch_kernel,
        out_shape=jax.ShapeDtypeStruct((N_EXPERTS, CAP_ALL, D_MODEL), F32),
        grid_spec=pltpu.PrefetchScalarGridSpec(
            num_scalar_prefetch=2,
            grid=(N_ROUTE_TILES,),
            in_specs=[pl.BlockSpec((ROUTE_TILE, D_MODEL), lambda t, s0, cnt: (t, 0)),
                      pl.BlockSpec((N_EXPERTS, ROUTE_TILE), lambda t, s0, cnt: (0, t))],
            out_specs=pl.BlockSpec(memory_space=pl.ANY),
            scratch_shapes=[pltpu.VMEM((DISPATCH_ROWS, ROUTE_TILE), BF16),
                            pltpu.VMEM((DISPATCH_ROWS, D_MODEL), F32),
                            pltpu.SemaphoreType.DMA(())]),
        compiler_params=_cparams(("arbitrary",)),
        name="moe_dispatch",
    )(s0, cnt, h, posm_all)


def _ffn_up_kernel(x_ref, wg_ref, wu_ref, o_ref, ag_ref, au_ref):
    k = pl.program_id(1)
    x = x_ref[0].astype(BF16)
    pg = jnp.dot(x, wg_ref[0].astype(BF16), preferred_element_type=F32)
    pu = jnp.dot(x, wu_ref[0].astype(BF16), preferred_element_type=F32)

    @pl.when(k == 0)
    def _():
        ag_ref[...] = pg
        au_ref[...] = pu

    @pl.when(k > 0)
    def _():
        ag_ref[...] += pg
        au_ref[...] += pu

    @pl.when(k == pl.num_programs(1) - 1)
    def _():
        a = ag_ref[...]
        o_ref[0] = (a * jax.nn.sigmoid(a) * au_ref[...]).astype(o_ref.dtype)


def expert_ffn_up(xin, w_gate, w_up):
    e, cap, d = xin.shape
    ff = w_gate.shape[2]
    tk = 512
    return pl.pallas_call(
        _ffn_up_kernel,
        out_shape=jax.ShapeDtypeStruct((e, cap, ff), BF16),
        grid=(e, d // tk),
        in_specs=[pl.BlockSpec((1, cap, tk), lambda i, k: (i, 0, k)),
                  pl.BlockSpec((1, tk, ff), lambda i, k: (i, k, 0)),
                  pl.BlockSpec((1, tk, ff), lambda i, k: (i, k, 0))],
        out_specs=pl.BlockSpec((1, cap, ff), lambda i, k: (i, 0, 0)),
        scratch_shapes=[pltpu.VMEM((cap, ff), F32), pltpu.VMEM((cap, ff), F32)],
        compiler_params=_cparams(("arbitrary", "arbitrary")),
        name="expert_ffn_up",
    )(xin, w_gate, w_up)


def _ffn_down_kernel(a_ref, w_ref, o_ref):
    cap = a_ref.shape[1]
    o_ref[0, 0:cap, :] = jnp.dot(a_ref[0], w_ref[0].astype(BF16), preferred_element_type=F32)
    o_ref[0, cap:, :] = jnp.zeros((o_ref.shape[1] - cap, o_ref.shape[2]), F32)


def expert_ffn_down(act, w_down):
    e, cap, ff = act.shape
    d = w_down.shape[2]
    tn = 512
    return pl.pallas_call(
        _ffn_down_kernel,
        out_shape=jax.ShapeDtypeStruct((e, Y_ROWS, d), F32),
        grid=(e, d // tn),
        in_specs=[pl.BlockSpec((1, cap, ff), lambda i, j: (i, 0, 0)),
                  pl.BlockSpec((1, ff, tn), lambda i, j: (i, 0, j))],
        out_specs=pl.BlockSpec((1, Y_ROWS, tn), lambda i, j: (i, 0, j)),
        compiler_params=_cparams(("arbitrary", "arbitrary")),
        name="expert_ffn_down",
    )(act, w_down)


COMBINE_PIECE = 32
COMBINE_BLOCK = 256
COMBINE_ROWS = N_EXPERTS * ((ROUTE_TILE + SUBLANES - 1 + COMBINE_PIECE - 1) // COMBINE_PIECE) * COMBINE_PIECE


def _combine_kernel(s0_ref, cnt_ref, y_ref, posm_ref, aff_ref, o_ref, st_ref, pt_ref, sem):
    t = pl.program_id(0)
    iota = lax.broadcasted_iota(I32, (COMBINE_PIECE, ROUTE_TILE), 0)

    def window_copy(e, w, r0):
        return pltpu.make_async_copy(y_ref.at[e, pl.ds(w, COMBINE_PIECE)], st_ref.at[pl.ds(r0, COMBINE_PIECE)], sem)

    kk = jnp.int32(0)
    for e in range(N_EXPERTS):
        n = cnt_ref[t * N_EXPERTS + e]
        s0 = s0_ref[t * N_EXPERTS + e]
        a0 = (s0 // SUBLANES) * SUBLANES
        npc = jnp.where(n > 0, (s0 + n - a0 + COMBINE_PIECE - 1) // COMBINE_PIECE, 0)
        prow = posm_ref[e:e + 1, :]
        arow = aff_ref[e:e + 1, :]

        def piece(p, carry, e=e, a0=a0, kk=kk, prow=prow, arow=arow):
            w = pl.multiple_of(a0 + p * COMBINE_PIECE, SUBLANES)
            r0 = pl.multiple_of((kk + p) * COMBINE_PIECE, COMBINE_PIECE)
            window_copy(e, w, r0).start()
            pt_ref[pl.ds(r0, COMBINE_PIECE), :] = jnp.where(prow == (w + iota), arow, 0.0)
            return carry

        lax.fori_loop(0, npc, piece, 0)
        kk = kk + npc

    pieces_per_block = COMBINE_BLOCK // COMBINE_PIECE
    nblk = (kk + pieces_per_block - 1) // pieces_per_block

    def zero_tail(p, carry):
        r0 = pl.multiple_of((kk + p) * COMBINE_PIECE, COMBINE_PIECE)
        st_ref[pl.ds(r0, COMBINE_PIECE), :] = jnp.zeros((COMBINE_PIECE, D_MODEL), F32)
        pt_ref[pl.ds(r0, COMBINE_PIECE), :] = jnp.zeros((COMBINE_PIECE, ROUTE_TILE), F32)
        return carry

    lax.fori_loop(0, nblk * pieces_per_block - kk, zero_tail, 0)

    def drain(p, carry):
        window_copy(0, 0, 0).wait()
        return carry

    lax.fori_loop(0, kk, drain, 0)

    o_ref[...] = jnp.zeros(o_ref.shape, F32)

    def block(b, carry):
        r0 = pl.multiple_of(b * COMBINE_BLOCK, COMBINE_BLOCK)
        pw = pt_ref[pl.ds(r0, COMBINE_BLOCK), :]
        gate = jnp.sum(pw, axis=1, keepdims=True)
        onehot = jnp.where(pw != 0.0, 1.0, 0.0).astype(BF16)
        ys = st_ref[pl.ds(r0, COMBINE_BLOCK), :] * gate
        hi = ys.astype(BF16)
        lo = (ys - hi.astype(F32)).astype(BF16)
        o_ref[...] += _tn_dot(onehot, hi) + _tn_dot(onehot, lo)
        return carry

    lax.fori_loop(0, nblk, block, 0)


def moe_combine(y, posm_all, aff_t, s0, cnt):
    return pl.pallas_call(
        _combine_kernel,
        out_shape=jax.ShapeDtypeStruct((M_ALL, D_MODEL), F32),
        grid_spec=pltpu.PrefetchScalarGridSpec(
            num_scalar_prefetch=2,
            grid=(N_ROUTE_TILES,),
            in_specs=[pl.BlockSpec(memory_space=pl.ANY),
                      pl.BlockSpec((N_EXPERTS, ROUTE_TILE), lambda t, s0, cnt: (0, t)),
                      pl.BlockSpec((N_EXPERTS, ROUTE_TILE), lambda t, s0, cnt: (0, t))],
            out_specs=pl.BlockSpec((ROUTE_TILE, D_MODEL), lambda t, s0, cnt: (t, 0)),
            scratch_shapes=[pltpu.VMEM((COMBINE_ROWS, D_MODEL), F32),
                            pltpu.VMEM((COMBINE_ROWS, ROUTE_TILE), F32),
                            pltpu.SemaphoreType.DMA(())]),
        compiler_params=_cparams(("arbitrary",)),
        name="moe_combine",
    )(s0, cnt, y, posm_all, aff_t)


def expert_choice_moe(h, aff, w_gate, w_up, w_down):
    aff_t = aff[:, :N_EXPERTS].T
    posm_lat = expert_choice_select(aff_t[:, :SEQ], CAP_LAT)
    posm_ctx = expert_choice_select(aff_t[:, SEQ:], CAP_CTX)
    posm_all = jnp.concatenate([posm_lat, jnp.where(posm_ctx >= 0, posm_ctx + CAP_LAT, -1)], axis=1)
    s0, cnt = route_tile_meta(posm_all)
    xin = moe_dispatch(h, posm_all, s0, cnt)
    act = expert_ffn_up(xin, w_gate, w_up)
    y = expert_ffn_down(act, w_down)
    return moe_combine(y, posm_all, aff_t, s0, cnt)


def even_mixer(h, tabs, w_in, w_dw, b_dw, cn_g, cn_b, qa_g, kva_g, w_uq, w_ukv, w_o):
    u_all = matmul([h], jnp.pad(w_in, ((0, 0), (0, IN_A_PAD - IN_A))), F32, tn=640)
    q_cat, k_cat, v = mla_projections(u_all, qa_g, kva_g, w_uq, w_ukv, tabs)
    att = mla_attention(q_cat, k_cat, v)
    att_c = context_attention(q_cat, k_cat, v, MLA_HEADS, MLA_QK_PAD, MLA_V, 0, 0, 0, 1.0)
    conv = conformer_conv(u_all, w_dw, b_dw, cn_g, cn_b)
    return matmul([conv, jnp.concatenate([att, att_c], axis=0)], w_o, F32, tn=1024)


def odd_mixer(h, w_qkv, rpb, w_o):
    qkv = matmul([h], w_qkv, BF16, tn=1024)
    o = neighbourhood_attention(qkv, na_bias_tables(rpb))
    o_c = context_attention(qkv, qkv, qkv, NA_HEADS, NA_HEAD_DIM, NA_HEAD_DIM, 0, NA_HEADS, 2 * NA_HEADS,
                            NA_HEAD_DIM ** -0.5)
    return matmul([jnp.concatenate([o, o_c], axis=0)], w_o, F32, tn=1024)


def kernel(x, c, ctx, c_ctx, w_ada, b_ada, ln1_g, ln1_b, ln2_g, ln2_b, a_w_in, a_w_dw, a_b_dw, a_cn_g, a_cn_b,
           b_qa_g, b_kva_g, b_w_uq, b_w_ukv, ab_w_o, c_w_qkv, c_rpb, c_w_o, moe_w_router, moe_w_gate, moe_w_up,
           moe_w_down):
    assert x.shape == (1, SEQ, D_MODEL) and ctx.shape == (1, CTX_LEN, D_MODEL)
    xs = jnp.concatenate([x[0], ctx[0]], axis=0)
    cc = jnp.concatenate([c, c_ctx[None], jnp.zeros((SUBLANES - 2, D_MODEL), F32)], axis=0)
    mod = ada_modulation(cc, w_ada, b_ada)

    def vec(layer, k):
        return mod[layer, :2, k * D_MODEL:(k + 1) * D_MODEL].reshape(2, 1, D_MODEL)

    tabs = rope_tables()
    h = modulate_rows(xs, vec(0, 0), vec(0, 1))
    for layer in range(DEPTH):
        j = layer // 2
        if layer % 2 == 0:
            o = even_mixer(h, tabs, a_w_in[j], a_w_dw[j], a_b_dw[j], a_cn_g[j], a_cn_b[j], b_qa_g[j], b_kva_g[j],
                           b_w_uq[j], b_w_ukv[j], ab_w_o[j])
        else:
            o = odd_mixer(h, c_w_qkv[j], c_rpb[j], c_w_o[j])
        xs, h2, aff = post_norm_rows(xs, o, vec(layer, 2), ln1_g[layer], ln1_b[layer], vec(layer, 3), vec(layer, 4),
                                     moe_w_router[layer])
        y = expert_choice_moe(h2, aff, moe_w_gate[layer], moe_w_up[layer], moe_w_down[layer])
        nxt = min(layer + 1, DEPTH - 1)
        xs, h = post_norm_rows(xs, y, vec(layer, 5), ln2_g[layer], ln2_b[layer], vec(nxt, 0), vec(nxt, 1))
    return xs[:SEQ][None]
```

```python
import functools

import numpy as np
import jax
import jax.numpy as jnp
from jax import lax
from jax.experimental import pallas as pl
from jax.experimental.pallas import tpu as pltpu

F32 = jnp.float32
BF16 = jnp.bfloat16
I32 = jnp.int32

D_MODEL = 2048
SEQ = 8192
DEPTH = 4
GRID_W = 64
CTX_LEN = 256
M_ALL = SEQ + CTX_LEN

CONV_CH = 1024
CONV_WIDTH = 31
CONV_PAD = CONV_WIDTH // 2
MLA_HEADS = 8
MLA_Q_RANK = 512
MLA_KV_RANK = 512
MLA_NOPE = 128
MLA_ROPE = 64
MLA_V = 128
MLA_QK_PAD = 256
ROPE_BASE = 10000.0
NA_HEADS = 16
NA_HEAD_DIM = 128
NA_WIN_ROWS = 8
NA_WIN_COLS = 16
N_EXPERTS = 16
EXPERT_FF = 1408
EC_CAPACITY_FACTOR = 2
CAP_LAT = EC_CAPACITY_FACTOR * SEQ // N_EXPERTS
CAP_CTX = EC_CAPACITY_FACTOR * CTX_LEN // N_EXPERTS
CAP_ALL = CAP_LAT + CAP_CTX
IN_A = 2 * CONV_CH + MLA_Q_RANK + MLA_KV_RANK + MLA_ROPE
IN_A_MAIN = IN_A - MLA_ROPE
LOG2_E = 1.4426950408889634
DN_ALPHA = (2 * DEPTH) ** 0.25
LN_EPS = 1e-5
RMS_EPS = 1e-6
NEG_INF = -1e30

LANES = 128
SUBLANES = 8
VMEM_LIMIT = 56 * 1024 * 1024

ROW_TILE = 256
N_LAT_TILES = SEQ // ROW_TILE
N_ROW_TILES = M_ALL // ROW_TILE
MM_ROW_TILE = 768
ROUTE_TILE = 128
N_ROUTE_TILES = M_ALL // ROUTE_TILE
N_ROUTE_LAT = SEQ // ROUTE_TILE
Y_ROWS = CAP_ALL + 32


def _cparams(sem, vmem=VMEM_LIMIT):
    return pltpu.CompilerParams(dimension_semantics=sem, vmem_limit_bytes=vmem)


def _nt_dot(a, b):
    return lax.dot_general(a, b, (((1,), (1,)), ((), ())), preferred_element_type=F32)


def _tn_dot(a, b):
    return lax.dot_general(a, b, (((0,), (0,)), ((), ())), preferred_element_type=F32)


def _ada_kernel(c_ref, w_ref, b_ref, o_ref):
    c = c_ref[...]
    a = (c * jax.nn.sigmoid(c)).astype(BF16)
    o_ref[0] = jnp.dot(a, w_ref[0].astype(BF16), preferred_element_type=F32) + b_ref[0]


def ada_modulation(cc, w_ada, b_ada):
    depth, d, n = w_ada.shape
    tn = 1024
    return pl.pallas_call(
        _ada_kernel,
        out_shape=jax.ShapeDtypeStruct((depth, SUBLANES, n), F32),
        grid=(depth, n // tn),
        in_specs=[
            pl.BlockSpec((SUBLANES, d), lambda l, j: (0, 0)),
            pl.BlockSpec((1, d, tn), lambda l, j: (l, 0, j)),
            pl.BlockSpec((1, 1, tn), lambda l, j: (l, 0, j)),
        ],
        out_specs=pl.BlockSpec((1, SUBLANES, tn), lambda l, j: (l, 0, j)),
        compiler_params=_cparams(("arbitrary", "arbitrary")),
        name="ada_modulation",
    )(cc, w_ada, b_ada.reshape(depth, 1, n))


def _stream_vec_spec():
    return pl.BlockSpec((1, 1, D_MODEL), lambda i: (jnp.minimum(i // N_LAT_TILES, 1), 0, 0))


def _row_spec(width=D_MODEL):
    return pl.BlockSpec((ROW_TILE, width), lambda i: (i, 0))


def _modulate_kernel(x_ref, sh_ref, sc_ref, h_ref):
    h_ref[...] = (x_ref[...] * (1.0 + sc_ref[0]) + sh_ref[0]).astype(h_ref.dtype)


def modulate_rows(x, sh, sc):
    return pl.pallas_call(
        _modulate_kernel,
        out_shape=jax.ShapeDtypeStruct(x.shape, BF16),
        grid=(N_ROW_TILES,),
        in_specs=[_row_spec(), _stream_vec_spec(), _stream_vec_spec()],
        out_specs=_row_spec(),
        compiler_params=_cparams(("arbitrary",)),
        name="modulate_rows",
    )(x, sh, sc)


def _post_norm_kernel(*refs, with_router):
    if with_router:
        x_ref, y_ref, gate_ref, lg_ref, lb_ref, sh_ref, sc_ref, wr_ref, xo_ref, h_ref, aff_ref = refs
    else:
        x_ref, y_ref, gate_ref, lg_ref, lb_ref, sh_ref, sc_ref, xo_ref, h_ref = refs
    z = DN_ALPHA * x_ref[...] + gate_ref[0] * y_ref[...]
    mu = jnp.mean(z, axis=-1, keepdims=True)
    zc = z - mu
    var = jnp.mean(zc * zc, axis=-1, keepdims=True)
    xn = zc * lax.rsqrt(var + LN_EPS) * lg_ref[...] + lb_ref[...]
    xo_ref[...] = xn
    h = (xn * (1.0 + sc_ref[0]) + sh_ref[0]).astype(BF16)
    h_ref[...] = h
    if with_router:
        logits = jnp.dot(h, wr_ref[...].astype(BF16), preferred_element_type=F32)
        lane = lax.broadcasted_iota(I32, logits.shape, 1)
        logits = jnp.where(lane < N_EXPERTS, logits, NEG_INF)
        e = jnp.exp(logits - jnp.max(logits, axis=-1, keepdims=True))
        aff_ref[...] = e / jnp.sum(e, axis=-1, keepdims=True)


def post_norm_rows(x, y, gate, ln_g, ln_b, sh, sc, w_router=None):
    with_router = w_router is not None
    vec = pl.BlockSpec((1, D_MODEL), lambda i: (0, 0))
    in_specs = [_row_spec(), _row_spec(), _stream_vec_spec(), vec, vec, _stream_vec_spec(), _stream_vec_spec()]
    out_shape = [jax.ShapeDtypeStruct(x.shape, F32), jax.ShapeDtypeStruct(x.shape, BF16)]
    out_specs = [_row_spec(), _row_spec()]
    args = [x, y, gate, ln_g.reshape(1, -1), ln_b.reshape(1, -1), sh, sc]
    if with_router:
        in_specs.append(pl.BlockSpec((D_MODEL, LANES), lambda i: (0, 0)))
        out_shape.append(jax.ShapeDtypeStruct((x.shape[0], LANES), F32))
        out_specs.append(_row_spec(LANES))
        args.append(jnp.pad(w_router, ((0, 0), (0, LANES - N_EXPERTS))))
    return pl.pallas_call(
        functools.partial(_post_norm_kernel, with_router=with_router),
        out_shape=out_shape,
        grid=(N_ROW_TILES,),
        in_specs=in_specs,
        out_specs=out_specs,
        compiler_params=_cparams(("arbitrary",)),
        name="post_norm_router" if with_router else "post_norm",
    )(*args)


def _matmul_kernel(*refs, splits):
    n_a = len(splits)
    a_refs, w_ref, o_ref, wb_ref = refs[:n_a], refs[n_a], refs[n_a + 1], refs[n_a + 2]

    @pl.when(pl.program_id(1) == 0)
    def _():
        wb_ref[...] = w_ref[...].astype(BF16)

    acc = None
    off = 0
    for a_ref, k in zip(a_refs, splits):
        part = jnp.dot(a_ref[...].astype(BF16), wb_ref[off:off + k, :], preferred_element_type=F32)
        acc = part if acc is None else acc + part
        off += k
    o_ref[...] = acc.astype(o_ref.dtype)


def matmul(a_list, w, out_dtype, tn, tm=MM_ROW_TILE, layer=None, n_cols=None):
    m = a_list[0].shape[0]
    splits = tuple(a.shape[1] for a in a_list)
    k, n = w.shape[-2:]
    n = n if n_cols is None else n_cols
    assert sum(splits) == k and n % tn == 0 and m % tm == 0
    in_specs = [pl.BlockSpec((tm, ki), lambda j, i: (i, 0)) for ki in splits]
    if layer is None:
        in_specs.append(pl.BlockSpec((k, tn), lambda j, i: (0, j)))
    else:
        in_specs.append(pl.BlockSpec((None, k, tn), lambda j, i: (layer, 0, j)))
    return pl.pallas_call(
        functools.partial(_matmul_kernel, splits=splits),
        out_shape=jax.ShapeDtypeStruct((m, n), out_dtype),
        grid=(n // tn, m // tm),
        in_specs=in_specs,
        out_specs=pl.BlockSpec((tm, tn), lambda j, i: (i, j)),
        scratch_shapes=[pltpu.VMEM((k, tn), BF16)],
        compiler_params=_cparams(("arbitrary", "arbitrary")),
        name="matmul",
    )(*a_list, w)


def rope_tables():
    t = jnp.arange(SEQ, dtype=I32)
    row = (t // GRID_W).astype(F32)
    col = (t % GRID_W).astype(F32)
    n_freq = MLA_ROPE // 4
    inv_freq = ROPE_BASE ** (-jnp.arange(n_freq, dtype=F32) / n_freq)
    ang = jnp.concatenate([row[:, None] * inv_freq, col[:, None] * inv_freq], axis=-1)
    cos, sin = jnp.cos(ang), jnp.sin(ang)
    half = MLA_ROPE // 2
    cos = jnp.concatenate([cos, jnp.ones((CTX_LEN, half), F32)], axis=0)
    sin = jnp.concatenate([sin, jnp.zeros((CTX_LEN, half), F32)], axis=0)
    z = jnp.zeros_like(cos)
    c_tab = jnp.concatenate([cos, cos, z, z], axis=1)
    sa_tab = jnp.concatenate([z, sin, z, z], axis=1)
    sb_tab = jnp.concatenate([-sin, z, z, z], axis=1)
    return c_tab, sa_tab, sb_tab


def _mla_proj_kernel(cq_ref, ckv_ref, kr_ref, qg_ref, kvg_ref, wq_ref, wkv_ref, c_ref, sa_ref, sb_ref,
                     q_out, k_out, vt_out, wqb, wkvb):
    @pl.when(pl.program_id(0) == 0)
    def _():
        wqb[...] = wq_ref[...].astype(BF16)
        wkvb[...] = wkv_ref[...].astype(BF16)

    def rms(x, g):
        return x * lax.rsqrt(jnp.mean(x * x, axis=-1, keepdims=True) + RMS_EPS) * g

    c_tab, sa_tab, sb_tab = c_ref[...], sa_ref[...], sb_ref[...]
    half = MLA_ROPE // 2

    def rope(g):
        return g * c_tab + pltpu.roll(g, half, 1) * sa_tab + pltpu.roll(g, LANES - half, 1) * sb_tab

    qk_scale = (MLA_NOPE + MLA_ROPE) ** -0.5 * LOG2_E
    q = jnp.dot(rms(cq_ref[...], qg_ref[...]).astype(BF16), wqb[...], preferred_element_type=F32) * qk_scale
    kv = jnp.dot(rms(ckv_ref[...], kvg_ref[...]).astype(BF16), wkvb[...], preferred_element_type=F32)
    k_rope = rope(kr_ref[...]).astype(BF16)
    for h in range(MLA_HEADS):
        lo = h * MLA_QK_PAD
        q_out[:, lo:lo + LANES] = q[:, lo:lo + LANES].astype(BF16)
        q_out[:, lo + LANES:lo + 2 * LANES] = rope(q[:, lo + LANES:lo + 2 * LANES]).astype(BF16)
        k_out[:, lo:lo + LANES] = kv[:, h * MLA_NOPE:(h + 1) * MLA_NOPE].astype(BF16)
        k_out[:, lo + LANES:lo + 2 * LANES] = k_rope
    vt_out[...] = kv[:, MLA_HEADS * MLA_NOPE:].T.astype(BF16)


def mla_projections(u_main, kr, qa_g, kva_g, w_uq, w_ukv, tabs):
    m = u_main.shape[0]
    tm = 384
    hq = MLA_HEADS * MLA_QK_PAD
    wq = w_uq.reshape(MLA_Q_RANK, MLA_HEADS, MLA_NOPE + MLA_ROPE)
    wq = jnp.pad(wq, ((0, 0), (0, 0), (0, MLA_QK_PAD - MLA_NOPE - MLA_ROPE))).reshape(MLA_Q_RANK, hq)
    wkv = w_ukv.reshape(MLA_KV_RANK, MLA_HEADS, MLA_NOPE + MLA_V)
    wkv = jnp.concatenate([wkv[..., :MLA_NOPE].reshape(MLA_KV_RANK, -1), wkv[..., MLA_NOPE:].reshape(MLA_KV_RANK, -1)], axis=1)
    nkv = wkv.shape[1]
    cq_blk = 2 * CONV_CH // MLA_Q_RANK
    tab_spec = pl.BlockSpec((tm, LANES), lambda i: (i, 0))
    return pl.pallas_call(
        _mla_proj_kernel,
        out_shape=[jax.ShapeDtypeStruct((m, hq), BF16), jax.ShapeDtypeStruct((m, hq), BF16),
                   jax.ShapeDtypeStruct((MLA_HEADS * MLA_V, m), BF16)],
        grid=(m // tm,),
        in_specs=[
            pl.BlockSpec((tm, MLA_Q_RANK), lambda i: (i, cq_blk)),
            pl.BlockSpec((tm, MLA_KV_RANK), lambda i: (i, cq_blk + 1)),
            pl.BlockSpec((tm, LANES), lambda i: (i, 0)),
            pl.BlockSpec((1, MLA_Q_RANK), lambda i: (0, 0)),
            pl.BlockSpec((1, MLA_KV_RANK), lambda i: (0, 0)),
            pl.BlockSpec((MLA_Q_RANK, hq), lambda i: (0, 0)),
            pl.BlockSpec((MLA_KV_RANK, nkv), lambda i: (0, 0)),
            tab_spec, tab_spec, tab_spec,
        ],
        out_specs=[pl.BlockSpec((tm, hq), lambda i: (i, 0)), pl.BlockSpec((tm, hq), lambda i: (i, 0)),
                   pl.BlockSpec((MLA_HEADS * MLA_V, tm), lambda i: (0, i))],
        scratch_shapes=[pltpu.VMEM((MLA_Q_RANK, hq), BF16), pltpu.VMEM((MLA_KV_RANK, nkv), BF16)],
        compiler_params=_cparams(("arbitrary",)),
        name="mla_projections",
    )(u_main, u_main, kr, qa_g.reshape(1, -1), kva_g.reshape(1, -1), wq, wkv, *tabs)


MLA_KEY_CHUNK = 1408
MLA_Q_TILE = 512


def _flash_kernel(q_ref, k_ref, vt_ref, o_ref, *, tk, n_chunks):
    q = q_ref[...]
    tq = q.shape[0]
    m = jnp.full((1, tq), NEG_INF, F32)
    l = jnp.zeros((1, tq), F32)
    acc = jnp.zeros((vt_ref.shape[0], tq), F32)
    s_next = _nt_dot(k_ref[0:tk, :], q)
    for c in range(n_chunks):
        s = s_next
        if c + 1 < n_chunks:
            s_next = _nt_dot(k_ref[(c + 1) * tk:(c + 2) * tk, :], q)
        m_new = jnp.maximum(m, jnp.max(s, axis=0, keepdims=True))
        alpha = jnp.exp2(m - m_new)
        p = jnp.exp2(s - m_new)
        l = alpha * l + jnp.sum(p, axis=0, keepdims=True)
        acc = alpha * acc + jnp.dot(vt_ref[:, c * tk:(c + 1) * tk], p.astype(BF16), preferred_element_type=F32)
        m = m_new
    o_ref[...] = (acc / l).T.astype(o_ref.dtype)


def mla_attention(q_cat, k_cat, vt):
    m = k_cat.shape[0]
    tq, tk = MLA_Q_TILE, MLA_KEY_CHUNK
    return pl.pallas_call(
        functools.partial(_flash_kernel, tk=tk, n_chunks=m // tk),
        out_shape=jax.ShapeDtypeStruct((SEQ, MLA_HEADS * MLA_V), BF16),
        grid=(MLA_HEADS, SEQ // tq),
        in_specs=[
            pl.BlockSpec((tq, MLA_QK_PAD), lambda h, i: (i, h)),
            pl.BlockSpec((m, MLA_QK_PAD), lambda h, i: (0, h)),
            pl.BlockSpec((MLA_V, m), lambda h, i: (h, 0)),
        ],
        out_specs=pl.BlockSpec((tq, MLA_V), lambda h, i: (i, h)),
        compiler_params=_cparams(("arbitrary", "arbitrary")),
        name="mla_attention",
    )(q_cat, k_cat, vt)


def _ctx_attn_kernel(q_ref, k_ref, v_ref, o_ref, *, scale, base2, v_transposed):
    s = _nt_dot(q_ref[...], k_ref[...]) * scale
    z = s - jnp.max(s, axis=-1, keepdims=True)
    p = jnp.exp2(z) if base2 else jnp.exp(z)
    l = jnp.sum(p, axis=-1, keepdims=True)
    pv = _nt_dot(p.astype(BF16), v_ref[...]) if v_transposed else jnp.dot(p.astype(BF16), v_ref[...],
                                                                         preferred_element_type=F32)
    o_ref[...] = (pv / l).astype(o_ref.dtype)


def context_attention(q_arr, k_arr, v_arr, heads, dq, dv, q_col, k_col, v_col, scale, base2=False,
                      v_transposed=False):
    rb = SEQ // CTX_LEN
    if v_transposed:
        v_spec = pl.BlockSpec((dv, CTX_LEN), lambda h: (v_col + h, rb))
    else:
        v_spec = pl.BlockSpec((CTX_LEN, dv), lambda h: (rb, v_col + h))
    return pl.pallas_call(
        functools.partial(_ctx_attn_kernel, scale=scale, base2=base2, v_transposed=v_transposed),
        out_shape=jax.ShapeDtypeStruct((CTX_LEN, heads * dv), BF16),
        grid=(heads,),
        in_specs=[
            pl.BlockSpec((CTX_LEN, dq), lambda h: (rb, q_col + h)),
            pl.BlockSpec((CTX_LEN, dq), lambda h: (rb, k_col + h)),
            v_spec,
        ],
        out_specs=pl.BlockSpec((CTX_LEN, dv), lambda h: (0, h)),
        compiler_params=_cparams(("arbitrary",)),
        name="context_attention",
    )(q_arr, k_arr, v_arr)


CONV_HALO = 16
CONV_ROW_BLOCK = 64


def _conv_kernel(a_ref, g_ref, ap_ref, gp_ref, an_ref, gn_ref, w_ref, b_ref, cg_ref, cb_ref, o_ref, hbuf, cbuf):
    i = pl.program_id(0)
    tl = a_ref.shape[0]

    def glu(a, g):
        return a[...] * jax.nn.sigmoid(g[...])

    has_prev = jnp.logical_and(i != 0, i != N_LAT_TILES)
    has_next = jnp.logical_and(i != N_LAT_TILES - 1, i != N_ROW_TILES - 1)
    hbuf[0:CONV_HALO, :] = jnp.where(has_prev, glu(ap_ref, gp_ref), 0.0)
    hbuf[CONV_HALO:CONV_HALO + tl, :] = glu(a_ref, g_ref)
    hbuf[CONV_HALO + tl:, :] = jnp.where(has_next, glu(an_ref, gn_ref), 0.0)

    base = CONV_HALO - CONV_PAD
    for rb in range(tl // CONV_ROW_BLOCK):
        r0 = rb * CONV_ROW_BLOCK
        for c in range(CONV_CH // LANES):
            cs = slice(c * LANES, (c + 1) * LANES)
            acc = jnp.broadcast_to(b_ref[:, cs], (CONV_ROW_BLOCK, LANES))
            for j in range(CONV_WIDTH):
                acc = acc + hbuf[r0 + base + j:r0 + base + j + CONV_ROW_BLOCK, cs] * w_ref[j:j + 1, cs]
            cbuf[r0:r0 + CONV_ROW_BLOCK, cs] = acc

    y = cbuf[...]
    mu = jnp.mean(y, axis=-1, keepdims=True)
    yc = y - mu
    var = jnp.mean(yc * yc, axis=-1, keepdims=True)
    yn = yc * lax.rsqrt(var + LN_EPS) * cg_ref[...] + cb_ref[...]
    o_ref[...] = (yn * jax.nn.sigmoid(yn)).astype(o_ref.dtype)


def conformer_conv(u_main, w_dw, b_dw, cn_g, cn_b):
    m = u_main.shape[0]
    tl = ROW_TILE
    hpt = tl // CONV_HALO
    n_halo = m // CONV_HALO
    main = lambda col: pl.BlockSpec((tl, CONV_CH), lambda i: (i, col))
    prev = lambda col: pl.BlockSpec((CONV_HALO, CONV_CH), lambda i: (jnp.maximum(i * hpt - 1, 0), col))
    nxt = lambda col: pl.BlockSpec((CONV_HALO, CONV_CH), lambda i: (jnp.minimum((i + 1) * hpt, n_halo - 1), col))
    vec = pl.BlockSpec((1, CONV_CH), lambda i: (0, 0))
    return pl.pallas_call(
        _conv_kernel,
        out_shape=jax.ShapeDtypeStruct((m, CONV_CH), BF16),
        grid=(m // tl,),
        in_specs=[main(0), main(1), prev(0), prev(1), nxt(0), nxt(1),
                  pl.BlockSpec((CONV_WIDTH, CONV_CH), lambda i: (0, 0)), vec, vec, vec],
        out_specs=pl.BlockSpec((tl, CONV_CH), lambda i: (i, 0)),
        scratch_shapes=[pltpu.VMEM((tl + 2 * CONV_HALO, CONV_CH), F32), pltpu.VMEM((tl, CONV_CH), F32)],
        compiler_params=_cparams(("arbitrary",)),
        name="conformer_conv",
    )(u_main, u_main, u_main, u_main, u_main, u_main, w_dw, b_dw.reshape(1, -1), cn_g.reshape(1, -1), cn_b.reshape(1, -1))


NA_Q_ROWS = 8
NA_K_ROWS = 16
NA_KBLK = 4 * GRID_W
NA_QTOK = NA_Q_ROWS * GRID_W
NA_KTOK = NA_K_ROWS * GRID_W
NA_GROUPS = SEQ // NA_QTOK


NA_HEADS_PER_STEP = 2
NA_CLASSES = (0, 1, NA_GROUPS - 1)


def _na_window_rows():
    rows = SEQ // GRID_W
    out = []
    for g in NA_CLASSES:
        qr = (NA_Q_ROWS * g + np.arange(NA_Q_ROWS))[:, None]
        kr = (NA_Q_ROWS * g - (NA_K_ROWS - NA_Q_ROWS) // 2 + np.arange(NA_K_ROWS))[None, :]
        start = np.clip(qr - NA_WIN_ROWS // 2, 0, rows - NA_WIN_ROWS)
        out.append((kr >= start) & (kr < start + NA_WIN_ROWS))
    return np.stack(out)


def na_bias_strips(rpb):
    qc = np.arange(GRID_W)[:, None]
    kc = np.arange(GRID_W)[None, :]
    ws = np.clip(qc - NA_WIN_COLS // 2, 0, GRID_W - NA_WIN_COLS)
    col_valid = (kc >= ws) & (kc < ws + NA_WIN_COLS)
    col_idx = np.clip(kc - qc + NA_WIN_COLS - 1, 0, 2 * NA_WIN_COLS - 2)
    onehot = jnp.asarray((col_idx[None] == np.arange(2 * NA_WIN_COLS - 1)[:, None, None]).astype(np.float32))
    cm = jnp.einsum('hrd,dqk->hrqk', rpb, onehot, precision=lax.Precision.HIGHEST)
    cm = jnp.where(col_valid[None, None], cm, NEG_INF)
    neg = jnp.full_like(cm, NEG_INF)
    return jnp.concatenate([cm, neg], axis=-1), jnp.concatenate([neg, cm], axis=-1)


def _na_kernel(q_ref, k0, k1, k2, k3, v0, v1, v2, v3, kc_ref, vc_ref, bl_ref, br_ref, o_ref, tbl_ref):
    g = pl.program_id(1)
    row_valid = _na_window_rows()

    @pl.when(g == 0)
    def _():
        neg_tile = jnp.full((GRID_W, LANES), NEG_INF, F32)
        for hh in range(NA_HEADS_PER_STEP):
            for c in range(len(NA_CLASSES)):
                for ql in range(NA_Q_ROWS):
                    for kp in range(NA_K_ROWS // 2):
                        ka, kb = 2 * kp, 2 * kp + 1
                        da = ka - ql + NA_WIN_ROWS - 1 - (NA_K_ROWS - NA_Q_ROWS) // 2
                        ta = bl_ref[hh, da] if row_valid[c, ql, ka] else neg_tile
                        tb = br_ref[hh, da + 1] if row_valid[c, ql, kb] else neg_tile
                        tbl_ref[hh, c, ql * GRID_W:(ql + 1) * GRID_W, kp * LANES:(kp + 1) * LANES] = jnp.maximum(ta, tb)

    cls = jnp.where(g == 0, 0, jnp.where(g == NA_GROUPS - 1, 2, 1))
    scale = NA_HEAD_DIM ** -0.5
    for hh in range(NA_HEADS_PER_STEP):
        hs = slice(hh * NA_HEAD_DIM, (hh + 1) * NA_HEAD_DIM)
        q = q_ref[:, hs]
        s = jnp.concatenate([_nt_dot(q, k[:, hs]) for k in (k0, k1, k2, k3)], axis=1) * scale + tbl_ref[hh, cls]
        sc = _nt_dot(q, kc_ref[:, hs]) * scale
        m = jnp.maximum(jnp.max(s, axis=-1, keepdims=True), jnp.max(sc, axis=-1, keepdims=True))
        p = jnp.exp(s - m)
        pc = jnp.exp(sc - m)
        l = jnp.sum(p, axis=-1, keepdims=True) + jnp.sum(pc, axis=-1, keepdims=True)
        o = jnp.dot(pc.astype(BF16), vc_ref[:, hs], preferred_element_type=F32)
        for j, v in enumerate((v0, v1, v2, v3)):
            o = o + jnp.dot(p[:, j * NA_KBLK:(j + 1) * NA_KBLK].astype(BF16), v[:, hs], preferred_element_type=F32)
        o_ref[:, hs] = (o / l).astype(o_ref.dtype)


def neighbourhood_attention(qkv, bias_left, bias_right):
    n_kblk = SEQ // NA_KBLK
    ctx_blk = SEQ // CTX_LEN
    hp = NA_HEADS_PER_STEP
    width = hp * NA_HEAD_DIM
    n_pairs = NA_HEADS // hp

    def kv_spec(j, col0):
        return pl.BlockSpec((NA_KBLK, width), lambda h, g: (jnp.clip(2 * g - 1 + j, 0, n_kblk - 1), col0 + h))

    strip_spec = pl.BlockSpec((hp, 2 * NA_WIN_ROWS - 1, GRID_W, LANES), lambda h, g: (h, 0, 0, 0))
    return pl.pallas_call(
        _na_kernel,
        out_shape=jax.ShapeDtypeStruct((SEQ, NA_HEADS * NA_HEAD_DIM), BF16),
        grid=(n_pairs, NA_GROUPS),
        in_specs=[pl.BlockSpec((NA_QTOK, width), lambda h, g: (g, h))]
        + [kv_spec(j, n_pairs) for j in range(4)]
        + [kv_spec(j, 2 * n_pairs) for j in range(4)]
        + [pl.BlockSpec((CTX_LEN, width), lambda h, g: (ctx_blk, n_pairs + h)),
           pl.BlockSpec((CTX_LEN, width), lambda h, g: (ctx_blk, 2 * n_pairs + h)),
           strip_spec, strip_spec],
        out_specs=pl.BlockSpec((NA_QTOK, width), lambda h, g: (g, h)),
        scratch_shapes=[pltpu.VMEM((hp, len(NA_CLASSES), NA_QTOK, NA_KTOK), F32)],
        compiler_params=_cparams(("arbitrary", "arbitrary")),
        name="neighbourhood_attention",
    )(*([qkv] * 11), bias_left, bias_right)


def _select_kernel(aff_ref, posm_ref, *, cap):
    bits = pltpu.bitcast(aff_ref[...], I32)
    n = bits.shape[1]

    def search(i, thr):
        cand = thr | (jnp.int32(1) << (30 - i))
        cnt = jnp.sum((bits >= cand).astype(I32), axis=1, keepdims=True)
        return jnp.where(cnt >= cap, cand, thr)

    thr = lax.fori_loop(0, 31, search, jnp.zeros((N_EXPERTS, 1), I32))
    gt = bits > thr
    eq = bits == thr
    need = (cap - jnp.sum(gt.astype(I32), axis=1, keepdims=True)).astype(F32)
    tri = jnp.where(lax.broadcasted_iota(I32, (LANES, LANES), 0) <= lax.broadcasted_iota(I32, (LANES, LANES), 1),
                    1.0, 0.0).astype(BF16)
    off_eq = jnp.zeros((N_EXPERTS, 1), F32)
    off_sel = jnp.zeros((N_EXPERTS, 1), F32)
    for b in range(n // LANES):
        cs = slice(b * LANES, (b + 1) * LANES)
        eq_b = jnp.where(eq[:, cs], 1.0, 0.0)
        inc_eq = jnp.dot(eq_b.astype(BF16), tri, preferred_element_type=F32)
        rank = off_eq + inc_eq - eq_b
        sel_b = jnp.logical_or(gt[:, cs], jnp.logical_and(eq[:, cs], rank < need))
        sel_f = jnp.where(sel_b, 1.0, 0.0)
        inc_sel = jnp.dot(sel_f.astype(BF16), tri, preferred_element_type=F32)
        pos = off_sel + inc_sel - sel_f
        posm_ref[:, cs] = jnp.where(sel_b, pos.astype(I32), -1)
        off_eq = off_eq + inc_eq[:, LANES - 1:LANES]
        off_sel = off_sel + inc_sel[:, LANES - 1:LANES]


def expert_choice_select(aff_t, cap):
    return pl.pallas_call(
        functools.partial(_select_kernel, cap=cap),
        out_shape=jax.ShapeDtypeStruct(aff_t.shape, I32),
        compiler_params=_cparams(None),
        name="expert_choice_select",
    )(aff_t)


def route_tile_meta(posm_all):
    cnt = (posm_all >= 0).astype(I32).reshape(N_EXPERTS, N_ROUTE_TILES, ROUTE_TILE).sum(-1).T
    lat, ctx = cnt[:N_ROUTE_LAT], cnt[N_ROUTE_LAT:]
    s0 = jnp.concatenate([jnp.cumsum(lat, 0) - lat, CAP_LAT + jnp.cumsum(ctx, 0) - ctx], axis=0)
    return s0.reshape(-1), cnt.reshape(-1)


DISPATCH_PIECE = 16
DISPATCH_BLOCK = 256
DISPATCH_ROWS = N_EXPERTS * ((ROUTE_TILE + SUBLANES - 1 + DISPATCH_PIECE - 1) // DISPATCH_PIECE) * DISPATCH_PIECE


def _dispatch_kernel(s0_ref, cnt_ref, h_ref, posm_ref, xin_ref, pt_ref, x_ref, carry_ref, sem):
    t = pl.program_id(0)

    @pl.when(t == 0)
    def _():
        carry_ref[...] = jnp.zeros(carry_ref.shape, F32)

    pt_ref[...] = jnp.zeros(pt_ref.shape, pt_ref.dtype)
    iota = lax.broadcasted_iota(I32, (DISPATCH_PIECE, ROUTE_TILE), 0)
    offs = []
    o = jnp.int32(0)
    for e in range(N_EXPERTS):
        n = cnt_ref[t * N_EXPERTS + e]
        s0 = s0_ref[t * N_EXPERTS + e]
        a0 = (s0 // SUBLANES) * SUBLANES
        npc = jnp.where(n > 0, (s0 + n - a0 + DISPATCH_PIECE - 1) // DISPATCH_PIECE, 0)
        row = posm_ref[e:e + 1, :]
        offs.append(o)

        def piece(p, carry, row=row, a0=a0, o=o):
            r0 = pl.multiple_of(o + p * DISPATCH_PIECE, DISPATCH_PIECE)
            hit = row == (a0 + p * DISPATCH_PIECE + iota)
            pt_ref[pl.ds(r0, DISPATCH_PIECE), :] = jnp.where(hit, 1.0, 0.0).astype(BF16)
            return carry

        lax.fori_loop(0, npc, piece, 0)
        o = o + npc * DISPATCH_PIECE

    def block(b, carry):
        r0 = pl.multiple_of(b * DISPATCH_BLOCK, DISPATCH_BLOCK)
        x_ref[pl.ds(r0, DISPATCH_BLOCK), :] = jnp.dot(pt_ref[pl.ds(r0, DISPATCH_BLOCK), :], h_ref[...],
                                                      preferred_element_type=F32)
        return carry

    lax.fori_loop(0, (o + DISPATCH_BLOCK - 1) // DISPATCH_BLOCK, block, 0)

    def tile_copy(src_row, e, dst_row):
        return pltpu.make_async_copy(x_ref.at[pl.ds(src_row, SUBLANES)], xin_ref.at[e, pl.ds(dst_row, SUBLANES)], sem)

    total = jnp.int32(0)
    for e in range(N_EXPERTS):
        n = cnt_ref[t * N_EXPERTS + e]
        s0 = s0_ref[t * N_EXPERTS + e]
        a0 = (s0 // SUBLANES) * SUBLANES
        span = s0 + n - a0
        full = jnp.where(n > 0, span // SUBLANES, 0)
        o_e = offs[e]

        @pl.when(n > 0)
        def _(e=e, a0=a0, span=span, full=full, o_e=o_e):
            head = pl.multiple_of(o_e, SUBLANES)
            x_ref[pl.ds(head, SUBLANES), :] += carry_ref[e]

            def issue(g, carry):
                tile_copy(pl.multiple_of(o_e + g * SUBLANES, SUBLANES), e,
                          pl.multiple_of(a0 + g * SUBLANES, SUBLANES)).start()
                return carry

            lax.fori_loop(0, full, issue, 0)
            tail = x_ref[pl.ds(pl.multiple_of(o_e + full * SUBLANES, SUBLANES), SUBLANES), :]
            carry_ref[e] = jnp.where(span - full * SUBLANES > 0, tail, 0.0)

        total = total + full

    def drain(g, carry):
        tile_copy(0, 0, 0).wait()
        return carry

    lax.fori_loop(0, total, drain, 0)


def moe_dispatch(h, posm_all, s0, cnt):
    return pl.pallas_call(
        _dispatch_kernel,
        out_shape=jax.ShapeDtypeStruct((N_EXPERTS, CAP_ALL, D_MODEL), F32),
        grid_spec=pltpu.PrefetchScalarGridSpec(
            num_scalar_prefetch=2,
            grid=(N_ROUTE_TILES,),
            in_specs=[pl.BlockSpec((ROUTE_TILE, D_MODEL), lambda t, s0, cnt: (t, 0)),
                      pl.BlockSpec((N_EXPERTS, ROUTE_TILE), lambda t, s0, cnt: (0, t))],
            out_specs=pl.BlockSpec(memory_space=pl.ANY),
            scratch_shapes=[pltpu.VMEM((DISPATCH_ROWS, ROUTE_TILE), BF16),
                            pltpu.VMEM((DISPATCH_ROWS + SUBLANES, D_MODEL), F32),
                            pltpu.VMEM((N_EXPERTS, SUBLANES, D_MODEL), F32),
                            pltpu.SemaphoreType.DMA(())]),
        compiler_params=_cparams(("arbitrary",)),
        name="moe_dispatch",
    )(s0, cnt, h, posm_all)


def _ffn_up_kernel(x_ref, wg_ref, wu_ref, o_ref, ag_ref, au_ref):
    k = pl.program_id(1)
    x = x_ref[0].astype(BF16)
    pg = jnp.dot(x, wg_ref[0].astype(BF16), preferred_element_type=F32)
    pu = jnp.dot(x, wu_ref[0].astype(BF16), preferred_element_type=F32)

    @pl.when(k == 0)
    def _():
        ag_ref[...] = pg
        au_ref[...] = pu

    @pl.when(k > 0)
    def _():
        ag_ref[...] += pg
        au_ref[...] += pu

    @pl.when(k == pl.num_programs(1) - 1)
    def _():
        a = ag_ref[...]
        o_ref[0] = (a * jax.nn.sigmoid(a) * au_ref[...]).astype(o_ref.dtype)


def expert_ffn_up(xin, w_gate, w_up, layer):
    e, cap, d = xin.shape
    ff = w_gate.shape[3]
    tk = 512
    return pl.pallas_call(
        _ffn_up_kernel,
        out_shape=jax.ShapeDtypeStruct((e, cap, ff), BF16),
        grid=(e, d // tk),
        in_specs=[pl.BlockSpec((1, cap, tk), lambda i, k: (i, 0, k)),
                  pl.BlockSpec((None, 1, tk, ff), lambda i, k: (layer, i, k, 0)),
                  pl.BlockSpec((None, 1, tk, ff), lambda i, k: (layer, i, k, 0))],
        out_specs=pl.BlockSpec((1, cap, ff), lambda i, k: (i, 0, 0)),
        scratch_shapes=[pltpu.VMEM((cap, ff), F32), pltpu.VMEM((cap, ff), F32)],
        compiler_params=_cparams(("arbitrary", "arbitrary")),
        name="expert_ffn_up",
    )(xin, w_gate, w_up)


def _ffn_down_kernel(a_ref, w_ref, o_ref):
    cap = a_ref.shape[1]
    o_ref[0, 0:cap, :] = jnp.dot(a_ref[0], w_ref[0].astype(BF16), preferred_element_type=F32)
    o_ref[0, cap:, :] = jnp.zeros((o_ref.shape[1] - cap, o_ref.shape[2]), F32)


def expert_ffn_down(act, w_down, layer):
    e, cap, ff = act.shape
    d = w_down.shape[3]
    tn = 512
    return pl.pallas_call(
        _ffn_down_kernel,
        out_shape=jax.ShapeDtypeStruct((e, Y_ROWS, d), F32),
        grid=(e, d // tn),
        in_specs=[pl.BlockSpec((1, cap, ff), lambda i, j: (i, 0, 0)),
                  pl.BlockSpec((None, 1, ff, tn), lambda i, j: (layer, i, 0, j))],
        out_specs=pl.BlockSpec((1, Y_ROWS, tn), lambda i, j: (i, 0, j)),
        compiler_params=_cparams(("arbitrary", "arbitrary")),
        name="expert_ffn_down",
    )(act, w_down)


COMBINE_PIECE = 32
COMBINE_BLOCK = 256
COMBINE_ROWS = N_EXPERTS * ((ROUTE_TILE + SUBLANES - 1 + COMBINE_PIECE - 1) // COMBINE_PIECE) * COMBINE_PIECE


def _combine_kernel(s0_ref, cnt_ref, y_ref, posm_ref, aff_ref, o_ref, st_ref, pt_ref, sem):
    t = pl.program_id(0)
    iota = lax.broadcasted_iota(I32, (COMBINE_PIECE, ROUTE_TILE), 0)

    def window_copy(e, w, r0):
        return pltpu.make_async_copy(y_ref.at[e, pl.ds(w, COMBINE_PIECE)], st_ref.at[pl.ds(r0, COMBINE_PIECE)], sem)

    kk = jnp.int32(0)
    for e in range(N_EXPERTS):
        n = cnt_ref[t * N_EXPERTS + e]
        s0 = s0_ref[t * N_EXPERTS + e]
        a0 = (s0 // SUBLANES) * SUBLANES
        npc = jnp.where(n > 0, (s0 + n - a0 + COMBINE_PIECE - 1) // COMBINE_PIECE, 0)
        prow = posm_ref[e:e + 1, :]
        arow = aff_ref[e:e + 1, :]

        def piece(p, carry, e=e, a0=a0, kk=kk, prow=prow, arow=arow):
            w = pl.multiple_of(a0 + p * COMBINE_PIECE, SUBLANES)
            r0 = pl.multiple_of((kk + p) * COMBINE_PIECE, COMBINE_PIECE)
            window_copy(e, w, r0).start()
            pt_ref[pl.ds(r0, COMBINE_PIECE), :] = jnp.where(prow == (w + iota), arow, 0.0)
            return carry

        lax.fori_loop(0, npc, piece, 0)
        kk = kk + npc

    pieces_per_block = COMBINE_BLOCK // COMBINE_PIECE
    nblk = (kk + pieces_per_block - 1) // pieces_per_block

    def zero_tail(p, carry):
        r0 = pl.multiple_of((kk + p) * COMBINE_PIECE, COMBINE_PIECE)
        st_ref[pl.ds(r0, COMBINE_PIECE), :] = jnp.zeros((COMBINE_PIECE, D_MODEL), F32)
        pt_ref[pl.ds(r0, COMBINE_PIECE), :] = jnp.zeros((COMBINE_PIECE, ROUTE_TILE), F32)
        return carry

    lax.fori_loop(0, nblk * pieces_per_block - kk, zero_tail, 0)

    def drain(p, carry):
        window_copy(0, 0, 0).wait()
        return carry

    lax.fori_loop(0, kk, drain, 0)

    o_ref[...] = jnp.zeros(o_ref.shape, F32)

    def block(b, carry):
        r0 = pl.multiple_of(b * COMBINE_BLOCK, COMBINE_BLOCK)
        pw = pt_ref[pl.ds(r0, COMBINE_BLOCK), :]
        gate = jnp.sum(pw, axis=1, keepdims=True)
        onehot = jnp.where(pw != 0.0, 1.0, 0.0).astype(BF16)
        ys = st_ref[pl.ds(r0, COMBINE_BLOCK), :] * gate
        hi = ys.astype(BF16)
        lo = (ys - hi.astype(F32)).astype(BF16)
        o_ref[...] += _tn_dot(onehot, hi) + _tn_dot(onehot, lo)
        return carry

    lax.fori_loop(0, nblk, block, 0)


def moe_combine(y, posm_all, aff_t, s0, cnt):
    return pl.pallas_call(
        _combine_kernel,
        out_shape=jax.ShapeDtypeStruct((M_ALL, D_MODEL), F32),
        grid_spec=pltpu.PrefetchScalarGridSpec(
            num_scalar_prefetch=2,
            grid=(N_ROUTE_TILES,),
            in_specs=[pl.BlockSpec(memory_space=pl.ANY),
                      pl.BlockSpec((N_EXPERTS, ROUTE_TILE), lambda t, s0, cnt: (0, t)),
                      pl.BlockSpec((N_EXPERTS, ROUTE_TILE), lambda t, s0, cnt: (0, t))],
            out_specs=pl.BlockSpec((ROUTE_TILE, D_MODEL), lambda t, s0, cnt: (t, 0)),
            scratch_shapes=[pltpu.VMEM((COMBINE_ROWS, D_MODEL), F32),
                            pltpu.VMEM((COMBINE_ROWS, ROUTE_TILE), F32),
                            pltpu.SemaphoreType.DMA(())]),
        compiler_params=_cparams(("arbitrary",)),
        name="moe_combine",
    )(s0, cnt, y, posm_all, aff_t)


def expert_choice_moe(h, aff, w_gate, w_up, w_down, layer):
    aff_t = aff[:, :N_EXPERTS].T
    posm_lat = expert_choice_select(aff_t[:, :SEQ], CAP_LAT)
    posm_ctx = expert_choice_select(aff_t[:, SEQ:], CAP_CTX)
    posm_all = jnp.concatenate([posm_lat, jnp.where(posm_ctx >= 0, posm_ctx + CAP_LAT, -1)], axis=1)
    s0, cnt = route_tile_meta(posm_all)
    xin = moe_dispatch(h, posm_all, s0, cnt)
    act = expert_ffn_up(xin, w_gate, w_up, layer)
    y = expert_ffn_down(act, w_down, layer)
    return moe_combine(y, posm_all, aff_t, s0, cnt)


def even_mixer(h, tabs, j, w_in, w_dw, b_dw, cn_g, cn_b, qa_g, kva_g, w_uq, w_ukv, w_o):
    u_main = matmul([h], w_in, F32, tn=768, layer=j, n_cols=IN_A_MAIN)
    w_kr = jnp.pad(w_in[j][:, IN_A_MAIN:], ((0, 0), (0, LANES - MLA_ROPE)))
    kr = matmul([h], w_kr, F32, tn=LANES)
    q_cat, k_cat, vt = mla_projections(u_main, kr, qa_g, kva_g, w_uq, w_ukv, tabs)
    att = mla_attention(q_cat, k_cat, vt)
    att_c = context_attention(q_cat, k_cat, vt, MLA_HEADS, MLA_QK_PAD, MLA_V, 0, 0, 0, 1.0, base2=True,
                              v_transposed=True)
    conv = conformer_conv(u_main, w_dw, b_dw, cn_g, cn_b)
    return matmul([conv, jnp.concatenate([att, att_c], axis=0)], w_o, F32, tn=1024, layer=j)


def odd_mixer(h, j, w_qkv, rpb, w_o):
    qkv = matmul([h], w_qkv, BF16, tn=1024, layer=j)
    o = neighbourhood_attention(qkv, *na_bias_strips(rpb))
    o_c = context_attention(qkv, qkv, qkv, NA_HEADS, NA_HEAD_DIM, NA_HEAD_DIM, 0, NA_HEADS, 2 * NA_HEADS,
                            NA_HEAD_DIM ** -0.5)
    return matmul([jnp.concatenate([o, o_c], axis=0)], w_o, F32, tn=1024, layer=j)


def kernel(x, c, ctx, c_ctx, w_ada, b_ada, ln1_g, ln1_b, ln2_g, ln2_b, a_w_in, a_w_dw, a_b_dw, a_cn_g, a_cn_b,
           b_qa_g, b_kva_g, b_w_uq, b_w_ukv, ab_w_o, c_w_qkv, c_rpb, c_w_o, moe_w_router, moe_w_gate, moe_w_up,
           moe_w_down):
    assert x.shape == (1, SEQ, D_MODEL) and ctx.shape == (1, CTX_LEN, D_MODEL)
    xs = jnp.concatenate([x[0], ctx[0]], axis=0)
    cc = jnp.concatenate([c, c_ctx[None], jnp.zeros((SUBLANES - 2, D_MODEL), F32)], axis=0)
    mod = ada_modulation(cc, w_ada, b_ada)

    def vec(layer, k):
        return mod[layer, :2, k * D_MODEL:(k + 1) * D_MODEL].reshape(2, 1, D_MODEL)

    tabs = rope_tables()
    h = modulate_rows(xs, vec(0, 0), vec(0, 1))
    for layer in range(DEPTH):
        j = layer // 2
        if layer % 2 == 0:
            o = even_mixer(h, tabs, j, a_w_in, a_w_dw[j], a_b_dw[j], a_cn_g[j], a_cn_b[j], b_qa_g[j], b_kva_g[j],
                           b_w_uq[j], b_w_ukv[j], ab_w_o)
        else:
            o = odd_mixer(h, j, c_w_qkv, c_rpb[j], c_w_o)
        xs, h2, aff = post_norm_rows(xs, o, vec(layer, 2), ln1_g[layer], ln1_b[layer], vec(layer, 3), vec(layer, 4),
                                     moe_w_router[layer])
        y = expert_choice_moe(h2, aff, moe_w_gate, moe_w_up, moe_w_down, layer)
        nxt = min(layer + 1, DEPTH - 1)
        xs, h = post_norm_rows(xs, y, vec(layer, 5), ln2_g[layer], ln2_b[layer], vec(nxt, 0), vec(nxt, 1))
    return xs[:SEQ][None]
```

```python
import functools

import numpy as np
import jax
import jax.numpy as jnp
from jax import lax
from jax.experimental import pallas as pl
from jax.experimental.pallas import tpu as pltpu

F32 = jnp.float32
BF16 = jnp.bfloat16
I32 = jnp.int32

D_MODEL = 2048
SEQ = 8192
DEPTH = 4
GRID_W = 64
CTX_LEN = 256
M_ALL = SEQ + CTX_LEN

CONV_CH = 1024
CONV_WIDTH = 31
CONV_PAD = CONV_WIDTH // 2
MLA_HEADS = 8
MLA_Q_RANK = 512
MLA_KV_RANK = 512
MLA_NOPE = 128
MLA_ROPE = 64
MLA_V = 128
MLA_QK_PAD = 256
ROPE_BASE = 10000.0
NA_HEADS = 16
NA_HEAD_DIM = 128
NA_WIN_ROWS = 8
NA_WIN_COLS = 16
N_EXPERTS = 16
EXPERT_FF = 1408
EC_CAPACITY_FACTOR = 2
CAP_LAT = EC_CAPACITY_FACTOR * SEQ // N_EXPERTS
CAP_CTX = EC_CAPACITY_FACTOR * CTX_LEN // N_EXPERTS
CAP_ALL = CAP_LAT + CAP_CTX
IN_A = 2 * CONV_CH + MLA_Q_RANK + MLA_KV_RANK + MLA_ROPE
IN_A_MAIN = IN_A - MLA_ROPE
LOG2_E = 1.4426950408889634
DN_ALPHA = (2 * DEPTH) ** 0.25
LN_EPS = 1e-5
RMS_EPS = 1e-6
NEG_INF = -1e30

LANES = 128
SUBLANES = 8
VMEM_LIMIT = 56 * 1024 * 1024

ROW_TILE = 256
N_LAT_TILES = SEQ // ROW_TILE
N_ROW_TILES = M_ALL // ROW_TILE
MM_ROW_TILE = 768
ROUTE_TILE = 128
N_ROUTE_TILES = M_ALL // ROUTE_TILE
N_ROUTE_LAT = SEQ // ROUTE_TILE
Y_ROWS = CAP_ALL + 32


def _cparams(sem, vmem=VMEM_LIMIT):
    return pltpu.CompilerParams(dimension_semantics=sem, vmem_limit_bytes=vmem)


def _nt_dot(a, b):
    return lax.dot_general(a, b, (((1,), (1,)), ((), ())), preferred_element_type=F32)


def _tn_dot(a, b):
    return lax.dot_general(a, b, (((0,), (0,)), ((), ())), preferred_element_type=F32)


def _ada_kernel(c_ref, w_ref, b_ref, o_ref):
    c = c_ref[...]
    a = (c * jax.nn.sigmoid(c)).astype(BF16)
    o_ref[0] = jnp.dot(a, w_ref[0].astype(BF16), preferred_element_type=F32) + b_ref[0]


def ada_modulation(cc, w_ada, b_ada):
    depth, d, n = w_ada.shape
    tn = 1024
    return pl.pallas_call(
        _ada_kernel,
        out_shape=jax.ShapeDtypeStruct((depth, SUBLANES, n), F32),
        grid=(depth, n // tn),
        in_specs=[
            pl.BlockSpec((SUBLANES, d), lambda l, j: (0, 0)),
            pl.BlockSpec((1, d, tn), lambda l, j: (l, 0, j)),
            pl.BlockSpec((1, 1, tn), lambda l, j: (l, 0, j)),
        ],
        out_specs=pl.BlockSpec((1, SUBLANES, tn), lambda l, j: (l, 0, j)),
        compiler_params=_cparams(("arbitrary", "arbitrary")),
        name="ada_modulation",
    )(cc, w_ada, b_ada.reshape(depth, 1, n))


def _stream_vec_spec():
    return pl.BlockSpec((1, 1, D_MODEL), lambda i: (jnp.minimum(i // N_LAT_TILES, 1), 0, 0))


def _row_spec(width=D_MODEL):
    return pl.BlockSpec((ROW_TILE, width), lambda i: (i, 0))


def _modulate_kernel(x_ref, sh_ref, sc_ref, h_ref):
    h_ref[...] = (x_ref[...] * (1.0 + sc_ref[0]) + sh_ref[0]).astype(h_ref.dtype)


def modulate_rows(x, sh, sc):
    return pl.pallas_call(
        _modulate_kernel,
        out_shape=jax.ShapeDtypeStruct(x.shape, BF16),
        grid=(N_ROW_TILES,),
        in_specs=[_row_spec(), _stream_vec_spec(), _stream_vec_spec()],
        out_specs=_row_spec(),
        compiler_params=_cparams(("arbitrary",)),
        name="modulate_rows",
    )(x, sh, sc)


def _post_norm_kernel(*refs, with_router):
    if with_router:
        x_ref, y_ref, gate_ref, lg_ref, lb_ref, sh_ref, sc_ref, wr_ref, xo_ref, h_ref, aff_ref = refs
    else:
        x_ref, y_ref, gate_ref, lg_ref, lb_ref, sh_ref, sc_ref, xo_ref, h_ref = refs
    z = DN_ALPHA * x_ref[...] + gate_ref[0] * y_ref[...]
    mu = jnp.mean(z, axis=-1, keepdims=True)
    zc = z - mu
    var = jnp.mean(zc * zc, axis=-1, keepdims=True)
    xn = zc * lax.rsqrt(var + LN_EPS) * lg_ref[...] + lb_ref[...]
    xo_ref[...] = xn
    h = (xn * (1.0 + sc_ref[0]) + sh_ref[0]).astype(BF16)
    h_ref[...] = h
    if with_router:
        logits = jnp.dot(h, wr_ref[...].astype(BF16), preferred_element_type=F32)
        lane = lax.broadcasted_iota(I32, logits.shape, 1)
        logits = jnp.where(lane < N_EXPERTS, logits, NEG_INF)
        e = jnp.exp(logits - jnp.max(logits, axis=-1, keepdims=True))
        aff_ref[...] = e / jnp.sum(e, axis=-1, keepdims=True)


def post_norm_rows(x, y, gate, ln_g, ln_b, sh, sc, w_router=None):
    with_router = w_router is not None
    vec = pl.BlockSpec((1, D_MODEL), lambda i: (0, 0))
    in_specs = [_row_spec(), _row_spec(), _stream_vec_spec(), vec, vec, _stream_vec_spec(), _stream_vec_spec()]
    out_shape = [jax.ShapeDtypeStruct(x.shape, F32), jax.ShapeDtypeStruct(x.shape, BF16)]
    out_specs = [_row_spec(), _row_spec()]
    args = [x, y, gate, ln_g.reshape(1, -1), ln_b.reshape(1, -1), sh, sc]
    if with_router:
        in_specs.append(pl.BlockSpec((D_MODEL, LANES), lambda i: (0, 0)))
        out_shape.append(jax.ShapeDtypeStruct((x.shape[0], LANES), F32))
        out_specs.append(_row_spec(LANES))
        args.append(jnp.pad(w_router, ((0, 0), (0, LANES - N_EXPERTS))))
    return pl.pallas_call(
        functools.partial(_post_norm_kernel, with_router=with_router),
        out_shape=out_shape,
        grid=(N_ROW_TILES,),
        in_specs=in_specs,
        out_specs=out_specs,
        compiler_params=_cparams(("arbitrary",)),
        name="post_norm_router" if with_router else "post_norm",
    )(*args)


def _matmul_kernel(*refs, splits):
    n_a = len(splits)
    a_refs, w_ref, o_ref, wb_ref = refs[:n_a], refs[n_a], refs[n_a + 1], refs[n_a + 2]

    @pl.when(pl.program_id(1) == 0)
    def _():
        wb_ref[...] = w_ref[...].astype(BF16)

    acc = None
    off = 0
    for a_ref, k in zip(a_refs, splits):
        part = jnp.dot(a_ref[...].astype(BF16), wb_ref[off:off + k, :], preferred_element_type=F32)
        acc = part if acc is None else acc + part
        off += k
    o_ref[...] = acc.astype(o_ref.dtype)


def matmul(a_list, w, out_dtype, tn, tm=MM_ROW_TILE, layer=None, n_cols=None):
    m = a_list[0].shape[0]
    splits = tuple(a.shape[1] for a in a_list)
    k, n = w.shape[-2:]
    n = n if n_cols is None else n_cols
    assert sum(splits) == k and n % tn == 0 and m % tm == 0
    in_specs = [pl.BlockSpec((tm, ki), lambda j, i: (i, 0)) for ki in splits]
    if layer is None:
        in_specs.append(pl.BlockSpec((k, tn), lambda j, i: (0, j)))
    else:
        in_specs.append(pl.BlockSpec((None, k, tn), lambda j, i: (layer, 0, j)))
    return pl.pallas_call(
        functools.partial(_matmul_kernel, splits=splits),
        out_shape=jax.ShapeDtypeStruct((m, n), out_dtype),
        grid=(n // tn, m // tm),
        in_specs=in_specs,
        out_specs=pl.BlockSpec((tm, tn), lambda j, i: (i, j)),
        scratch_shapes=[pltpu.VMEM((k, tn), BF16)],
        compiler_params=_cparams(("arbitrary", "arbitrary")),
        name="matmul",
    )(*a_list, w)


def rope_tables():
    t = jnp.arange(SEQ, dtype=I32)
    row = (t // GRID_W).astype(F32)
    col = (t % GRID_W).astype(F32)
    n_freq = MLA_ROPE // 4
    inv_freq = ROPE_BASE ** (-jnp.arange(n_freq, dtype=F32) / n_freq)
    ang = jnp.concatenate([row[:, None] * inv_freq, col[:, None] * inv_freq], axis=-1)
    cos, sin = jnp.cos(ang), jnp.sin(ang)
    half = MLA_ROPE // 2
    cos = jnp.concatenate([cos, jnp.ones((CTX_LEN, half), F32)], axis=0)
    sin = jnp.concatenate([sin, jnp.zeros((CTX_LEN, half), F32)], axis=0)
    z = jnp.zeros_like(cos)
    c_tab = jnp.concatenate([cos, cos, z, z], axis=1)
    sa_tab = jnp.concatenate([z, sin, z, z], axis=1)
    sb_tab = jnp.concatenate([-sin, z, z, z], axis=1)
    return c_tab, sa_tab, sb_tab


def _mla_proj_kernel(cq_ref, ckv_ref, kr_ref, qg_ref, kvg_ref, wq_ref, wkv_ref, c_ref, sa_ref, sb_ref,
                     q_out, k_out, vt_out, wqb, wkvb):
    @pl.when(pl.program_id(0) == 0)
    def _():
        wqb[...] = wq_ref[...].astype(BF16)
        wkvb[...] = wkv_ref[...].astype(BF16)

    def rms(x, g):
        return x * lax.rsqrt(jnp.mean(x * x, axis=-1, keepdims=True) + RMS_EPS) * g

    c_tab, sa_tab, sb_tab = c_ref[...], sa_ref[...], sb_ref[...]
    half = MLA_ROPE // 2

    def rope(g):
        return g * c_tab + pltpu.roll(g, half, 1) * sa_tab + pltpu.roll(g, LANES - half, 1) * sb_tab

    qk_scale = (MLA_NOPE + MLA_ROPE) ** -0.5 * LOG2_E
    q = jnp.dot(rms(cq_ref[...], qg_ref[...]).astype(BF16), wqb[...], preferred_element_type=F32) * qk_scale
    kv = jnp.dot(rms(ckv_ref[...], kvg_ref[...]).astype(BF16), wkvb[...], preferred_element_type=F32)
    k_rope = rope(kr_ref[...]).astype(BF16)
    for h in range(MLA_HEADS):
        lo = h * MLA_QK_PAD
        q_out[:, lo:lo + LANES] = q[:, lo:lo + LANES].astype(BF16)
        q_out[:, lo + LANES:lo + 2 * LANES] = rope(q[:, lo + LANES:lo + 2 * LANES]).astype(BF16)
        k_out[:, lo:lo + LANES] = kv[:, h * MLA_NOPE:(h + 1) * MLA_NOPE].astype(BF16)
        k_out[:, lo + LANES:lo + 2 * LANES] = k_rope
    vt_out[...] = kv[:, MLA_HEADS * MLA_NOPE:].T.astype(BF16)


def mla_projections(u_main, kr, qa_g, kva_g, w_uq, w_ukv, tabs):
    m = u_main.shape[0]
    tm = 384
    hq = MLA_HEADS * MLA_QK_PAD
    wq = w_uq.reshape(MLA_Q_RANK, MLA_HEADS, MLA_NOPE + MLA_ROPE)
    wq = jnp.pad(wq, ((0, 0), (0, 0), (0, MLA_QK_PAD - MLA_NOPE - MLA_ROPE))).reshape(MLA_Q_RANK, hq)
    wkv = w_ukv.reshape(MLA_KV_RANK, MLA_HEADS, MLA_NOPE + MLA_V)
    wkv = jnp.concatenate([wkv[..., :MLA_NOPE].reshape(MLA_KV_RANK, -1), wkv[..., MLA_NOPE:].reshape(MLA_KV_RANK, -1)], axis=1)
    nkv = wkv.shape[1]
    cq_blk = 2 * CONV_CH // MLA_Q_RANK
    tab_spec = pl.BlockSpec((tm, LANES), lambda i: (i, 0))
    return pl.pallas_call(
        _mla_proj_kernel,
        out_shape=[jax.ShapeDtypeStruct((m, hq), BF16), jax.ShapeDtypeStruct((m, hq), BF16),
                   jax.ShapeDtypeStruct((MLA_HEADS * MLA_V, m), BF16)],
        grid=(m // tm,),
        in_specs=[
            pl.BlockSpec((tm, MLA_Q_RANK), lambda i: (i, cq_blk)),
            pl.BlockSpec((tm, MLA_KV_RANK), lambda i: (i, cq_blk + 1)),
            pl.BlockSpec((tm, LANES), lambda i: (i, 0)),
            pl.BlockSpec((1, MLA_Q_RANK), lambda i: (0, 0)),
            pl.BlockSpec((1, MLA_KV_RANK), lambda i: (0, 0)),
            pl.BlockSpec((MLA_Q_RANK, hq), lambda i: (0, 0)),
            pl.BlockSpec((MLA_KV_RANK, nkv), lambda i: (0, 0)),
            tab_spec, tab_spec, tab_spec,
        ],
        out_specs=[pl.BlockSpec((tm, hq), lambda i: (i, 0)), pl.BlockSpec((tm, hq), lambda i: (i, 0)),
                   pl.BlockSpec((MLA_HEADS * MLA_V, tm), lambda i: (0, i))],
        scratch_shapes=[pltpu.VMEM((MLA_Q_RANK, hq), BF16), pltpu.VMEM((MLA_KV_RANK, nkv), BF16)],
        compiler_params=_cparams(("arbitrary",)),
        name="mla_projections",
    )(u_main, u_main, kr, qa_g.reshape(1, -1), kva_g.reshape(1, -1), wq, wkv, *tabs)


MLA_KEY_CHUNK = 1408
MLA_Q_TILE = 512


def _flash_kernel(q_ref, k_ref, vt_ref, o_ref, *, tk, n_chunks):
    q = q_ref[...]
    tq = q.shape[0]
    m = jnp.full((1, tq), NEG_INF, F32)
    l = jnp.zeros((1, tq), F32)
    acc = jnp.zeros((vt_ref.shape[0], tq), F32)
    s_next = _nt_dot(k_ref[0:tk, :], q)
    for c in range(n_chunks):
        s = s_next
        if c + 1 < n_chunks:
            s_next = _nt_dot(k_ref[(c + 1) * tk:(c + 2) * tk, :], q)
        m_new = jnp.maximum(m, jnp.max(s, axis=0, keepdims=True))
        alpha = jnp.exp2(m - m_new)
        p = jnp.exp2(s - m_new)
        l = alpha * l + jnp.sum(p, axis=0, keepdims=True)
        acc = alpha * acc + jnp.dot(vt_ref[:, c * tk:(c + 1) * tk], p.astype(BF16), preferred_element_type=F32)
        m = m_new
    o_ref[...] = (acc / l).T.astype(o_ref.dtype)


def mla_attention(q_cat, k_cat, vt):
    m = k_cat.shape[0]
    tq, tk = MLA_Q_TILE, MLA_KEY_CHUNK
    return pl.pallas_call(
        functools.partial(_flash_kernel, tk=tk, n_chunks=m // tk),
        out_shape=jax.ShapeDtypeStruct((SEQ, MLA_HEADS * MLA_V), BF16),
        grid=(MLA_HEADS, SEQ // tq),
        in_specs=[
            pl.BlockSpec((tq, MLA_QK_PAD), lambda h, i: (i, h)),
            pl.BlockSpec((m, MLA_QK_PAD), lambda h, i: (0, h)),
            pl.BlockSpec((MLA_V, m), lambda h, i: (h, 0)),
        ],
        out_specs=pl.BlockSpec((tq, MLA_V), lambda h, i: (i, h)),
        compiler_params=_cparams(("arbitrary", "arbitrary")),
        name="mla_attention",
    )(q_cat, k_cat, vt)


def _ctx_attn_kernel(q_ref, k_ref, v_ref, o_ref, *, scale, base2, v_transposed):
    s = _nt_dot(q_ref[...], k_ref[...]) * scale
    z = s - jnp.max(s, axis=-1, keepdims=True)
    p = jnp.exp2(z) if base2 else jnp.exp(z)
    l = jnp.sum(p, axis=-1, keepdims=True)
    pv = _nt_dot(p.astype(BF16), v_ref[...]) if v_transposed else jnp.dot(p.astype(BF16), v_ref[...],
                                                                         preferred_element_type=F32)
    o_ref[...] = (pv / l).astype(o_ref.dtype)


def context_attention(q_arr, k_arr, v_arr, heads, dq, dv, q_col, k_col, v_col, scale, base2=False,
                      v_transposed=False):
    rb = SEQ // CTX_LEN
    if v_transposed:
        v_spec = pl.BlockSpec((dv, CTX_LEN), lambda h: (v_col + h, rb))
    else:
        v_spec = pl.BlockSpec((CTX_LEN, dv), lambda h: (rb, v_col + h))
    return pl.pallas_call(
        functools.partial(_ctx_attn_kernel, scale=scale, base2=base2, v_transposed=v_transposed),
        out_shape=jax.ShapeDtypeStruct((CTX_LEN, heads * dv), BF16),
        grid=(heads,),
        in_specs=[
            pl.BlockSpec((CTX_LEN, dq), lambda h: (rb, q_col + h)),
            pl.BlockSpec((CTX_LEN, dq), lambda h: (rb, k_col + h)),
            v_spec,
        ],
        out_specs=pl.BlockSpec((CTX_LEN, dv), lambda h: (0, h)),
        compiler_params=_cparams(("arbitrary",)),
        name="context_attention",
    )(q_arr, k_arr, v_arr)


CONV_HALO = 16
CONV_ROW_BLOCK = 64


def _conv_kernel(a_ref, g_ref, ap_ref, gp_ref, an_ref, gn_ref, w_ref, b_ref, cg_ref, cb_ref, o_ref, hbuf, cbuf):
    i = pl.program_id(0)
    tl = a_ref.shape[0]

    def glu(a, g):
        return a[...] * jax.nn.sigmoid(g[...])

    has_prev = jnp.logical_and(i != 0, i != N_LAT_TILES)
    has_next = jnp.logical_and(i != N_LAT_TILES - 1, i != N_ROW_TILES - 1)
    hbuf[0:CONV_HALO, :] = jnp.where(has_prev, glu(ap_ref, gp_ref), 0.0)
    hbuf[CONV_HALO:CONV_HALO + tl, :] = glu(a_ref, g_ref)
    hbuf[CONV_HALO + tl:, :] = jnp.where(has_next, glu(an_ref, gn_ref), 0.0)

    base = CONV_HALO - CONV_PAD
    for rb in range(tl // CONV_ROW_BLOCK):
        r0 = rb * CONV_ROW_BLOCK
        for c in range(CONV_CH // LANES):
            cs = slice(c * LANES, (c + 1) * LANES)
            acc = jnp.broadcast_to(b_ref[:, cs], (CONV_ROW_BLOCK, LANES))
            for j in range(CONV_WIDTH):
                acc = acc + hbuf[r0 + base + j:r0 + base + j + CONV_ROW_BLOCK, cs] * w_ref[j:j + 1, cs]
            cbuf[r0:r0 + CONV_ROW_BLOCK, cs] = acc

    y = cbuf[...]
    mu = jnp.mean(y, axis=-1, keepdims=True)
    yc = y - mu
    var = jnp.mean(yc * yc, axis=-1, keepdims=True)
    yn = yc * lax.rsqrt(var + LN_EPS) * cg_ref[...] + cb_ref[...]
    o_ref[...] = (yn * jax.nn.sigmoid(yn)).astype(o_ref.dtype)


def conformer_conv(u_main, w_dw, b_dw, cn_g, cn_b):
    m = u_main.shape[0]
    tl = ROW_TILE
    hpt = tl // CONV_HALO
    n_halo = m // CONV_HALO
    main = lambda col: pl.BlockSpec((tl, CONV_CH), lambda i: (i, col))
    prev = lambda col: pl.BlockSpec((CONV_HALO, CONV_CH), lambda i: (jnp.maximum(i * hpt - 1, 0), col))
    nxt = lambda col: pl.BlockSpec((CONV_HALO, CONV_CH), lambda i: (jnp.minimum((i + 1) * hpt, n_halo - 1), col))
    vec = pl.BlockSpec((1, CONV_CH), lambda i: (0, 0))
    return pl.pallas_call(
        _conv_kernel,
        out_shape=jax.ShapeDtypeStruct((m, CONV_CH), BF16),
        grid=(m // tl,),
        in_specs=[main(0), main(1), prev(0), prev(1), nxt(0), nxt(1),
                  pl.BlockSpec((CONV_WIDTH, CONV_CH), lambda i: (0, 0)), vec, vec, vec],
        out_specs=pl.BlockSpec((tl, CONV_CH), lambda i: (i, 0)),
        scratch_shapes=[pltpu.VMEM((tl + 2 * CONV_HALO, CONV_CH), F32), pltpu.VMEM((tl, CONV_CH), F32)],
        compiler_params=_cparams(("arbitrary",)),
        name="conformer_conv",
    )(u_main, u_main, u_main, u_main, u_main, u_main, w_dw, b_dw.reshape(1, -1), cn_g.reshape(1, -1), cn_b.reshape(1, -1))


NA_Q_ROWS = 8
NA_K_ROWS = 16
NA_KBLK = 4 * GRID_W
NA_QTOK = NA_Q_ROWS * GRID_W
NA_KTOK = NA_K_ROWS * GRID_W
NA_GROUPS = SEQ // NA_QTOK


NA_HEADS_PER_STEP = 2
NA_CLASSES = (0, 1, NA_GROUPS - 1)


def _na_window_rows():
    rows = SEQ // GRID_W
    out = []
    for g in NA_CLASSES:
        qr = (NA_Q_ROWS * g + np.arange(NA_Q_ROWS))[:, None]
        kr = (NA_Q_ROWS * g - (NA_K_ROWS - NA_Q_ROWS) // 2 + np.arange(NA_K_ROWS))[None, :]
        start = np.clip(qr - NA_WIN_ROWS // 2, 0, rows - NA_WIN_ROWS)
        out.append((kr >= start) & (kr < start + NA_WIN_ROWS))
    return np.stack(out)


def na_bias_strips(rpb):
    qc = np.arange(GRID_W)[:, None]
    kc = np.arange(GRID_W)[None, :]
    ws = np.clip(qc - NA_WIN_COLS // 2, 0, GRID_W - NA_WIN_COLS)
    col_valid = (kc >= ws) & (kc < ws + NA_WIN_COLS)
    col_idx = np.clip(kc - qc + NA_WIN_COLS - 1, 0, 2 * NA_WIN_COLS - 2)
    onehot = jnp.asarray((col_idx[None] == np.arange(2 * NA_WIN_COLS - 1)[:, None, None]).astype(np.float32))
    cm = jnp.einsum('hrd,dqk->hrqk', rpb, onehot, precision=lax.Precision.HIGHEST)
    cm = jnp.where(col_valid[None, None], cm, NEG_INF)
    neg = jnp.full_like(cm, NEG_INF)
    return jnp.concatenate([cm, neg], axis=-1), jnp.concatenate([neg, cm], axis=-1)


def _na_kernel(q_ref, k0, k1, k2, k3, v0, v1, v2, v3, kc_ref, vc_ref, bl_ref, br_ref, o_ref, tbl_ref):
    g = pl.program_id(1)
    row_valid = _na_window_rows()

    @pl.when(g == 0)
    def _():
        neg_tile = jnp.full((GRID_W, LANES), NEG_INF, F32)
        for hh in range(NA_HEADS_PER_STEP):
            for c in range(len(NA_CLASSES)):
                for ql in range(NA_Q_ROWS):
                    for kp in range(NA_K_ROWS // 2):
                        ka, kb = 2 * kp, 2 * kp + 1
                        da = ka - ql + NA_WIN_ROWS - 1 - (NA_K_ROWS - NA_Q_ROWS) // 2
                        ta = bl_ref[hh, da] if row_valid[c, ql, ka] else neg_tile
                        tb = br_ref[hh, da + 1] if row_valid[c, ql, kb] else neg_tile
                        tbl_ref[hh, c, ql * GRID_W:(ql + 1) * GRID_W, kp * LANES:(kp + 1) * LANES] = jnp.maximum(ta, tb)

    cls = jnp.where(g == 0, 0, jnp.where(g == NA_GROUPS - 1, 2, 1))
    scale = NA_HEAD_DIM ** -0.5
    for hh in range(NA_HEADS_PER_STEP):
        hs = slice(hh * NA_HEAD_DIM, (hh + 1) * NA_HEAD_DIM)
        q = q_ref[:, hs]
        s = jnp.concatenate([_nt_dot(q, k[:, hs]) for k in (k0, k1, k2, k3)], axis=1) * scale + tbl_ref[hh, cls]
        sc = _nt_dot(q, kc_ref[:, hs]) * scale
        m = jnp.maximum(jnp.max(s, axis=-1, keepdims=True), jnp.max(sc, axis=-1, keepdims=True))
        p = jnp.exp(s - m)
        pc = jnp.exp(sc - m)
        l = jnp.sum(p, axis=-1, keepdims=True) + jnp.sum(pc, axis=-1, keepdims=True)
        o = jnp.dot(pc.astype(BF16), vc_ref[:, hs], preferred_element_type=F32)
        for j, v in enumerate((v0, v1, v2, v3)):
            o = o + jnp.dot(p[:, j * NA_KBLK:(j + 1) * NA_KBLK].astype(BF16), v[:, hs], preferred_element_type=F32)
        o_ref[:, hs] = (o / l).astype(o_ref.dtype)


def neighbourhood_attention(qkv, bias_left, bias_right):
    n_kblk = SEQ // NA_KBLK
    ctx_blk = SEQ // CTX_LEN
    hp = NA_HEADS_PER_STEP
    width = hp * NA_HEAD_DIM
    n_pairs = NA_HEADS // hp

    def kv_spec(j, col0):
        return pl.BlockSpec((NA_KBLK, width), lambda h, g: (jnp.clip(2 * g - 1 + j, 0, n_kblk - 1), col0 + h))

    strip_spec = pl.BlockSpec((hp, 2 * NA_WIN_ROWS - 1, GRID_W, LANES), lambda h, g: (h, 0, 0, 0))
    return pl.pallas_call(
        _na_kernel,
        out_shape=jax.ShapeDtypeStruct((SEQ, NA_HEADS * NA_HEAD_DIM), BF16),
        grid=(n_pairs, NA_GROUPS),
        in_specs=[pl.BlockSpec((NA_QTOK, width), lambda h, g: (g, h))]
        + [kv_spec(j, n_pairs) for j in range(4)]
        + [kv_spec(j, 2 * n_pairs) for j in range(4)]
        + [pl.BlockSpec((CTX_LEN, width), lambda h, g: (ctx_blk, n_pairs + h)),
           pl.BlockSpec((CTX_LEN, width), lambda h, g: (ctx_blk, 2 * n_pairs + h)),
           strip_spec, strip_spec],
        out_specs=pl.BlockSpec((NA_QTOK, width), lambda h, g: (g, h)),
        scratch_shapes=[pltpu.VMEM((hp, len(NA_CLASSES), NA_QTOK, NA_KTOK), F32)],
        compiler_params=_cparams(("arbitrary", "arbitrary")),
        name="neighbourhood_attention",
    )(*([qkv] * 11), bias_left, bias_right)


def _select_kernel(aff_ref, posm_ref, *, cap):
    bits = pltpu.bitcast(aff_ref[...], I32)
    n = bits.shape[1]

    def search(i, thr):
        cand = thr | (jnp.int32(1) << (30 - i))
        cnt = jnp.sum((bits >= cand).astype(I32), axis=1, keepdims=True)
        return jnp.where(cnt >= cap, cand, thr)

    thr = lax.fori_loop(0, 31, search, jnp.zeros((N_EXPERTS, 1), I32))
    gt = bits > thr
    eq = bits == thr
    need = (cap - jnp.sum(gt.astype(I32), axis=1, keepdims=True)).astype(F32)
    tri = jnp.where(lax.broadcasted_iota(I32, (LANES, LANES), 0) <= lax.broadcasted_iota(I32, (LANES, LANES), 1),
                    1.0, 0.0).astype(BF16)
    off_eq = jnp.zeros((N_EXPERTS, 1), F32)
    off_sel = jnp.zeros((N_EXPERTS, 1), F32)
    for b in range(n // LANES):
        cs = slice(b * LANES, (b + 1) * LANES)
        eq_b = jnp.where(eq[:, cs], 1.0, 0.0)
        inc_eq = jnp.dot(eq_b.astype(BF16), tri, preferred_element_type=F32)
        rank = off_eq + inc_eq - eq_b
        sel_b = jnp.logical_or(gt[:, cs], jnp.logical_and(eq[:, cs], rank < need))
        sel_f = jnp.where(sel_b, 1.0, 0.0)
        inc_sel = jnp.dot(sel_f.astype(BF16), tri, preferred_element_type=F32)
        pos = off_sel + inc_sel - sel_f
        posm_ref[:, cs] = jnp.where(sel_b, pos.astype(I32), -1)
        off_eq = off_eq + inc_eq[:, LANES - 1:LANES]
        off_sel = off_sel + inc_sel[:, LANES - 1:LANES]


def expert_choice_select(aff_t, cap):
    return pl.pallas_call(
        functools.partial(_select_kernel, cap=cap),
        out_shape=jax.ShapeDtypeStruct(aff_t.shape, I32),
        compiler_params=_cparams(None),
        name="expert_choice_select",
    )(aff_t)


def route_tile_meta(posm_all):
    cnt = (posm_all >= 0).astype(I32).reshape(N_EXPERTS, N_ROUTE_TILES, ROUTE_TILE).sum(-1).T
    lat, ctx = cnt[:N_ROUTE_LAT], cnt[N_ROUTE_LAT:]
    s0 = jnp.concatenate([jnp.cumsum(lat, 0) - lat, CAP_LAT + jnp.cumsum(ctx, 0) - ctx], axis=0)
    return s0.reshape(-1), cnt.reshape(-1)


DISPATCH_PIECE = 16
DISPATCH_BLOCK = 256
DISPATCH_ROWS = N_EXPERTS * ((ROUTE_TILE + SUBLANES - 1 + DISPATCH_PIECE - 1) // DISPATCH_PIECE) * DISPATCH_PIECE


XIN_WIDTH = D_MODEL + LANES


def _dispatch_kernel(s0_ref, cnt_ref, h_ref, posm_ref, aff_ref, xin_ref, pt_ref, x_ref, carry_ref, sem):
    t = pl.program_id(0)

    @pl.when(t == 0)
    def _():
        carry_ref[...] = jnp.zeros(carry_ref.shape, F32)

    pt_ref[...] = jnp.zeros(pt_ref.shape, pt_ref.dtype)
    iota = lax.broadcasted_iota(I32, (DISPATCH_PIECE, ROUTE_TILE), 0)
    offs = []
    o = jnp.int32(0)
    for e in range(N_EXPERTS):
        n = cnt_ref[t * N_EXPERTS + e]
        s0 = s0_ref[t * N_EXPERTS + e]
        a0 = (s0 // SUBLANES) * SUBLANES
        npc = jnp.where(n > 0, (s0 + n - a0 + DISPATCH_PIECE - 1) // DISPATCH_PIECE, 0)
        row = posm_ref[e:e + 1, :]
        arow = aff_ref[e:e + 1, :]
        offs.append(o)

        def piece(p, carry, row=row, arow=arow, a0=a0, o=o):
            r0 = pl.multiple_of(o + p * DISPATCH_PIECE, DISPATCH_PIECE)
            hit = row == (a0 + p * DISPATCH_PIECE + iota)
            pt_ref[pl.ds(r0, DISPATCH_PIECE), :] = jnp.where(hit, 1.0, 0.0).astype(BF16)
            gate = jnp.sum(jnp.where(hit, arow, 0.0), axis=1, keepdims=True)
            x_ref[pl.ds(r0, DISPATCH_PIECE), D_MODEL:] = jnp.broadcast_to(gate, (DISPATCH_PIECE, LANES))
            return carry

        lax.fori_loop(0, npc, piece, 0)
        o = o + npc * DISPATCH_PIECE

    def block(b, carry):
        r0 = pl.multiple_of(b * DISPATCH_BLOCK, DISPATCH_BLOCK)
        x_ref[pl.ds(r0, DISPATCH_BLOCK), 0:D_MODEL] = jnp.dot(pt_ref[pl.ds(r0, DISPATCH_BLOCK), :], h_ref[...],
                                                              preferred_element_type=F32)
        return carry

    lax.fori_loop(0, (o + DISPATCH_BLOCK - 1) // DISPATCH_BLOCK, block, 0)

    def tile_copy(src_row, e, dst_row):
        return pltpu.make_async_copy(x_ref.at[pl.ds(src_row, SUBLANES)], xin_ref.at[e, pl.ds(dst_row, SUBLANES)], sem)

    total = jnp.int32(0)
    for e in range(N_EXPERTS):
        n = cnt_ref[t * N_EXPERTS + e]
        s0 = s0_ref[t * N_EXPERTS + e]
        a0 = (s0 // SUBLANES) * SUBLANES
        span = s0 + n - a0
        full = jnp.where(n > 0, span // SUBLANES, 0)
        o_e = offs[e]

        @pl.when(n > 0)
        def _(e=e, a0=a0, span=span, full=full, o_e=o_e):
            head = pl.multiple_of(o_e, SUBLANES)
            x_ref[pl.ds(head, SUBLANES), :] += carry_ref[e]

            def issue(g, carry):
                tile_copy(pl.multiple_of(o_e + g * SUBLANES, SUBLANES), e,
                          pl.multiple_of(a0 + g * SUBLANES, SUBLANES)).start()
                return carry

            lax.fori_loop(0, full, issue, 0)
            tail = x_ref[pl.ds(pl.multiple_of(o_e + full * SUBLANES, SUBLANES), SUBLANES), :]
            carry_ref[e] = jnp.where(span - full * SUBLANES > 0, tail, 0.0)

        total = total + full

    def drain(g, carry):
        tile_copy(0, 0, 0).wait()
        return carry

    lax.fori_loop(0, total, drain, 0)


def moe_dispatch(h, posm_all, aff_t, s0, cnt):
    return pl.pallas_call(
        _dispatch_kernel,
        out_shape=jax.ShapeDtypeStruct((N_EXPERTS, CAP_ALL, XIN_WIDTH), F32),
        grid_spec=pltpu.PrefetchScalarGridSpec(
            num_scalar_prefetch=2,
            grid=(N_ROUTE_TILES,),
            in_specs=[pl.BlockSpec((ROUTE_TILE, D_MODEL), lambda t, s0, cnt: (t, 0)),
                      pl.BlockSpec((N_EXPERTS, ROUTE_TILE), lambda t, s0, cnt: (0, t)),
                      pl.BlockSpec((N_EXPERTS, ROUTE_TILE), lambda t, s0, cnt: (0, t))],
            out_specs=pl.BlockSpec(memory_space=pl.ANY),
            scratch_shapes=[pltpu.VMEM((DISPATCH_ROWS, ROUTE_TILE), BF16),
                            pltpu.VMEM((DISPATCH_ROWS + SUBLANES, XIN_WIDTH), F32),
                            pltpu.VMEM((N_EXPERTS, SUBLANES, XIN_WIDTH), F32),
                            pltpu.SemaphoreType.DMA(())]),
        compiler_params=_cparams(("arbitrary",)),
        name="moe_dispatch",
    )(s0, cnt, h, posm_all, aff_t)


def _ffn_up_kernel(x_ref, wg_ref, wu_ref, o_ref, ag_ref, au_ref):
    k = pl.program_id(1)
    x = x_ref[0].astype(BF16)
    pg = jnp.dot(x, wg_ref[0].astype(BF16), preferred_element_type=F32)
    pu = jnp.dot(x, wu_ref[0].astype(BF16), preferred_element_type=F32)

    @pl.when(k == 0)
    def _():
        ag_ref[...] = pg
        au_ref[...] = pu

    @pl.when(k > 0)
    def _():
        ag_ref[...] += pg
        au_ref[...] += pu

    @pl.when(k == pl.num_programs(1) - 1)
    def _():
        a = ag_ref[...]
        o_ref[0] = (a * jax.nn.sigmoid(a) * au_ref[...]).astype(o_ref.dtype)


def expert_ffn_up(xin, w_gate, w_up, layer):
    e, cap, _ = xin.shape
    d, ff = w_gate.shape[2:]
    tk = 1024
    return pl.pallas_call(
        _ffn_up_kernel,
        out_shape=jax.ShapeDtypeStruct((e, cap, ff), BF16),
        grid=(e, d // tk),
        in_specs=[pl.BlockSpec((1, cap, tk), lambda i, k: (i, 0, k)),
                  pl.BlockSpec((None, 1, tk, ff), lambda i, k: (layer, i, k, 0)),
                  pl.BlockSpec((None, 1, tk, ff), lambda i, k: (layer, i, k, 0))],
        out_specs=pl.BlockSpec((1, cap, ff), lambda i, k: (i, 0, 0)),
        scratch_shapes=[pltpu.VMEM((cap, ff), F32), pltpu.VMEM((cap, ff), F32)],
        compiler_params=_cparams(("arbitrary", "arbitrary"), vmem=62 * 1024 * 1024),
        name="expert_ffn_up",
    )(xin, w_gate, w_up)


def _ffn_down_kernel(a_ref, w_ref, g_ref, hi_ref, lo_ref):
    cap = a_ref.shape[1]
    y = jnp.dot(a_ref[0], w_ref[0].astype(BF16), preferred_element_type=F32) * g_ref[0, :, 0:1]
    hi = y.astype(BF16)
    hi_ref[0, 0:cap, :] = hi
    lo_ref[0, 0:cap, :] = (y - hi.astype(F32)).astype(BF16)
    pad = jnp.zeros((hi_ref.shape[1] - cap, hi_ref.shape[2]), BF16)
    hi_ref[0, cap:, :] = pad
    lo_ref[0, cap:, :] = pad


def expert_ffn_down(act, xin, w_down, layer):
    e, cap, ff = act.shape
    d = w_down.shape[3]
    tn = 512
    out = jax.ShapeDtypeStruct((e, Y_ROWS, d), BF16)
    out_spec = pl.BlockSpec((1, Y_ROWS, tn), lambda i, j: (i, 0, j))
    return pl.pallas_call(
        _ffn_down_kernel,
        out_shape=[out, out],
        grid=(e, d // tn),
        in_specs=[pl.BlockSpec((1, cap, ff), lambda i, j: (i, 0, 0)),
                  pl.BlockSpec((None, 1, ff, tn), lambda i, j: (layer, i, 0, j)),
                  pl.BlockSpec((1, cap, LANES), lambda i, j: (i, 0, D_MODEL // LANES))],
        out_specs=[out_spec, out_spec],
        compiler_params=_cparams(("arbitrary", "arbitrary")),
        name="expert_ffn_down",
    )(act, w_down, xin)


COMBINE_PIECE = 32
COMBINE_ALIGN = 16
COMBINE_BLOCK = 256
COMBINE_PIECES_PER_BLOCK = COMBINE_BLOCK // COMBINE_PIECE
COMBINE_ROWS = N_EXPERTS * ((ROUTE_TILE + COMBINE_ALIGN - 1 + COMBINE_PIECE - 1) // COMBINE_PIECE) * COMBINE_PIECE
COMBINE_BLOCKS = COMBINE_ROWS // COMBINE_BLOCK


def _combine_kernel(s0_ref, cnt_ref, yh_ref, yl_ref, posm_ref, x_ref, gate_ref, lg_ref, lb_ref, sh_ref, sc_ref,
                    xo_ref, h_ref, sth_ref, stl_ref, pt_ref, acc_ref, sems):
    t = pl.program_id(0)
    iota = lax.broadcasted_iota(I32, (COMBINE_PIECE, ROUTE_TILE), 0)

    def window_copies(e, w, k):
        r0 = pl.multiple_of(k * COMBINE_PIECE, COMBINE_PIECE)
        sem = sems.at[k // COMBINE_PIECES_PER_BLOCK]
        return (pltpu.make_async_copy(yh_ref.at[e, pl.ds(w, COMBINE_PIECE)], sth_ref.at[pl.ds(r0, COMBINE_PIECE)], sem),
                pltpu.make_async_copy(yl_ref.at[e, pl.ds(w, COMBINE_PIECE)], stl_ref.at[pl.ds(r0, COMBINE_PIECE)], sem))

    kk = jnp.int32(0)
    for e in range(N_EXPERTS):
        n = cnt_ref[t * N_EXPERTS + e]
        s0 = s0_ref[t * N_EXPERTS + e]
        a0 = (s0 // COMBINE_ALIGN) * COMBINE_ALIGN
        npc = jnp.where(n > 0, (s0 + n - a0 + COMBINE_PIECE - 1) // COMBINE_PIECE, 0)
        prow = posm_ref[e:e + 1, :]

        def piece(p, carry, e=e, a0=a0, kk=kk, prow=prow):
            w = pl.multiple_of(a0 + p * COMBINE_PIECE, COMBINE_ALIGN)
            k = kk + p
            for cp in window_copies(e, w, k):
                cp.start()
            r0 = pl.multiple_of(k * COMBINE_PIECE, COMBINE_PIECE)
            pt_ref[pl.ds(r0, COMBINE_PIECE), :] = jnp.where(prow == (w + iota), 1.0, 0.0).astype(BF16)
            return carry

        lax.fori_loop(0, npc, piece, 0)
        kk = kk + npc

    nblk = (kk + COMBINE_PIECES_PER_BLOCK - 1) // COMBINE_PIECES_PER_BLOCK

    def zero_tail(p, carry):
        r0 = pl.multiple_of((kk + p) * COMBINE_PIECE, COMBINE_PIECE)
        sth_ref[pl.ds(r0, COMBINE_PIECE), :] = jnp.zeros((COMBINE_PIECE, D_MODEL), BF16)
        stl_ref[pl.ds(r0, COMBINE_PIECE), :] = jnp.zeros((COMBINE_PIECE, D_MODEL), BF16)
        pt_ref[pl.ds(r0, COMBINE_PIECE), :] = jnp.zeros((COMBINE_PIECE, ROUTE_TILE), BF16)
        return carry

    lax.fori_loop(0, nblk * COMBINE_PIECES_PER_BLOCK - kk, zero_tail, 0)
    acc_ref[...] = jnp.zeros(acc_ref.shape, F32)

    def block(b, carry):
        def drain(p, c):
            for cp in window_copies(0, 0, b * COMBINE_PIECES_PER_BLOCK):
                cp.wait()
            return c

        lax.fori_loop(0, jnp.minimum(kk - b * COMBINE_PIECES_PER_BLOCK, COMBINE_PIECES_PER_BLOCK), drain, 0)
        r0 = pl.multiple_of(b * COMBINE_BLOCK, COMBINE_BLOCK)
        onehot = pt_ref[pl.ds(r0, COMBINE_BLOCK), :]
        acc_ref[...] += (_tn_dot(onehot, sth_ref[pl.ds(r0, COMBINE_BLOCK), :])
                         + _tn_dot(onehot, stl_ref[pl.ds(r0, COMBINE_BLOCK), :]))
        return carry

    lax.fori_loop(0, nblk, block, 0)

    z = DN_ALPHA * x_ref[...] + gate_ref[0] * acc_ref[...]
    mu = jnp.mean(z, axis=-1, keepdims=True)
    zc = z - mu
    var = jnp.mean(zc * zc, axis=-1, keepdims=True)
    xn = zc * lax.rsqrt(var + LN_EPS) * lg_ref[...] + lb_ref[...]
    xo_ref[...] = xn
    h_ref[...] = (xn * (1.0 + sc_ref[0]) + sh_ref[0]).astype(h_ref.dtype)


def moe_combine_post_norm(yh, yl, posm_all, s0, cnt, x, gate, ln_g, ln_b, sh, sc):
    tile = lambda width: pl.BlockSpec((ROUTE_TILE, width), lambda t, s0, cnt: (t, 0))
    route = pl.BlockSpec((N_EXPERTS, ROUTE_TILE), lambda t, s0, cnt: (0, t))
    stream_vec = pl.BlockSpec((1, 1, D_MODEL), lambda t, s0, cnt: (jnp.minimum(t // N_ROUTE_LAT, 1), 0, 0))
    vec = pl.BlockSpec((1, D_MODEL), lambda t, s0, cnt: (0, 0))
    return pl.pallas_call(
        _combine_kernel,
        out_shape=[jax.ShapeDtypeStruct((M_ALL, D_MODEL), F32), jax.ShapeDtypeStruct((M_ALL, D_MODEL), BF16)],
        grid_spec=pltpu.PrefetchScalarGridSpec(
            num_scalar_prefetch=2,
            grid=(N_ROUTE_TILES,),
            in_specs=[pl.BlockSpec(memory_space=pl.ANY), pl.BlockSpec(memory_space=pl.ANY), route,
                      tile(D_MODEL), stream_vec, vec, vec, stream_vec, stream_vec],
            out_specs=[tile(D_MODEL), tile(D_MODEL)],
            scratch_shapes=[pltpu.VMEM((COMBINE_ROWS, D_MODEL), BF16),
                            pltpu.VMEM((COMBINE_ROWS, D_MODEL), BF16),
                            pltpu.VMEM((COMBINE_ROWS, ROUTE_TILE), BF16),
                            pltpu.VMEM((ROUTE_TILE, D_MODEL), F32),
                            pltpu.SemaphoreType.DMA((COMBINE_BLOCKS,))]),
        compiler_params=_cparams(("arbitrary",)),
        name="moe_combine_post_norm",
    )(s0, cnt, yh, yl, posm_all, x, gate, ln_g.reshape(1, -1), ln_b.reshape(1, -1), sh, sc)


def expert_choice_moe_post_norm(h, aff, w_gate, w_up, w_down, layer, x, gate, ln_g, ln_b, sh, sc):
    aff_t = aff[:, :N_EXPERTS].T
    posm_lat = expert_choice_select(aff_t[:, :SEQ], CAP_LAT)
    posm_ctx = expert_choice_select(aff_t[:, SEQ:], CAP_CTX)
    posm_all = jnp.concatenate([posm_lat, jnp.where(posm_ctx >= 0, posm_ctx + CAP_LAT, -1)], axis=1)
    s0, cnt = route_tile_meta(posm_all)
    xin = moe_dispatch(h, posm_all, aff_t, s0, cnt)
    act = expert_ffn_up(xin, w_gate, w_up, layer)
    yh, yl = expert_ffn_down(act, xin, w_down, layer)
    return moe_combine_post_norm(yh, yl, posm_all, s0, cnt, x, gate, ln_g, ln_b, sh, sc)


def even_mixer(h, tabs, j, w_in, w_dw, b_dw, cn_g, cn_b, qa_g, kva_g, w_uq, w_ukv, w_o):
    u_main = matmul([h], w_in, F32, tn=768, layer=j, n_cols=IN_A_MAIN)
    w_kr = jnp.pad(w_in[j][:, IN_A_MAIN:], ((0, 0), (0, LANES - MLA_ROPE)))
    kr = matmul([h], w_kr, F32, tn=LANES)
    q_cat, k_cat, vt = mla_projections(u_main, kr, qa_g, kva_g, w_uq, w_ukv, tabs)
    att = mla_attention(q_cat, k_cat, vt)
    att_c = context_attention(q_cat, k_cat, vt, MLA_HEADS, MLA_QK_PAD, MLA_V, 0, 0, 0, 1.0, base2=True,
                              v_transposed=True)
    conv = conformer_conv(u_main, w_dw, b_dw, cn_g, cn_b)
    return matmul([conv, jnp.concatenate([att, att_c], axis=0)], w_o, F32, tn=1024, layer=j)


def odd_mixer(h, j, w_qkv, rpb, w_o):
    qkv = matmul([h], w_qkv, BF16, tn=1024, layer=j)
    o = neighbourhood_attention(qkv, *na_bias_strips(rpb))
    o_c = context_attention(qkv, qkv, qkv, NA_HEADS, NA_HEAD_DIM, NA_HEAD_DIM, 0, NA_HEADS, 2 * NA_HEADS,
                            NA_HEAD_DIM ** -0.5)
    return matmul([jnp.concatenate([o, o_c], axis=0)], w_o, F32, tn=1024, layer=j)


def kernel(x, c, ctx, c_ctx, w_ada, b_ada, ln1_g, ln1_b, ln2_g, ln2_b, a_w_in, a_w_dw, a_b_dw, a_cn_g, a_cn_b,
           b_qa_g, b_kva_g, b_w_uq, b_w_ukv, ab_w_o, c_w_qkv, c_rpb, c_w_o, moe_w_router, moe_w_gate, moe_w_up,
           moe_w_down):
    assert x.shape == (1, SEQ, D_MODEL) and ctx.shape == (1, CTX_LEN, D_MODEL)
    xs = jnp.concatenate([x[0], ctx[0]], axis=0)
    cc = jnp.concatenate([c, c_ctx[None], jnp.zeros((SUBLANES - 2, D_MODEL), F32)], axis=0)
    mod = ada_modulation(cc, w_ada, b_ada)

    def vec(layer, k):
        return mod[layer, :2, k * D_MODEL:(k + 1) * D_MODEL].reshape(2, 1, D_MODEL)

    tabs = rope_tables()
    h = modulate_rows(xs, vec(0, 0), vec(0, 1))
    for layer in range(DEPTH):
        j = layer // 2
        if layer % 2 == 0:
            o = even_mixer(h, tabs, j, a_w_in, a_w_dw[j], a_b_dw[j], a_cn_g[j], a_cn_b[j], b_qa_g[j], b_kva_g[j],
                           b_w_uq[j], b_w_ukv[j], ab_w_o)
        else:
            o = odd_mixer(h, j, c_w_qkv, c_rpb[j], c_w_o)
        xs, h2, aff = post_norm_rows(xs, o, vec(layer, 2), ln1_g[layer], ln1_b[layer], vec(layer, 3), vec(layer, 4),
                                     moe_w_router[layer])
        nxt = min(layer + 1, DEPTH - 1)
        xs, h = expert_choice_moe_post_norm(h2, aff, moe_w_gate, moe_w_up, moe_w_down, layer, xs, vec(layer, 5),
                                            ln2_g[layer], ln2_b[layer], vec(nxt, 0), vec(nxt, 1))
    return xs[:SEQ][None]
```

```python
import functools

import numpy as np
import jax
import jax.numpy as jnp
from jax import lax
from jax.experimental import pallas as pl
from jax.experimental.pallas import tpu as pltpu

F32 = jnp.float32
BF16 = jnp.bfloat16
I32 = jnp.int32

D_MODEL = 2048
SEQ = 8192
DEPTH = 4
GRID_W = 64
CTX_LEN = 256
M_ALL = SEQ + CTX_LEN

CONV_CH = 1024
CONV_WIDTH = 31
CONV_PAD = CONV_WIDTH // 2
MLA_HEADS = 8
MLA_Q_RANK = 512
MLA_KV_RANK = 512
MLA_NOPE = 128
MLA_ROPE = 64
MLA_V = 128
MLA_QK_PAD = 256
ROPE_BASE = 10000.0
NA_HEADS = 16
NA_HEAD_DIM = 128
NA_WIN_ROWS = 8
NA_WIN_COLS = 16
N_EXPERTS = 16
EXPERT_FF = 1408
EC_CAPACITY_FACTOR = 2
CAP_LAT = EC_CAPACITY_FACTOR * SEQ // N_EXPERTS
CAP_CTX = EC_CAPACITY_FACTOR * CTX_LEN // N_EXPERTS
CAP_ALL = CAP_LAT + CAP_CTX
IN_A = 2 * CONV_CH + MLA_Q_RANK + MLA_KV_RANK + MLA_ROPE
IN_A_MAIN = IN_A - MLA_ROPE
LOG2_E = 1.4426950408889634
DN_ALPHA = (2 * DEPTH) ** 0.25
LN_EPS = 1e-5
RMS_EPS = 1e-6
NEG_INF = -1e30

LANES = 128
SUBLANES = 8
VMEM_LIMIT = 56 * 1024 * 1024

ROW_TILE = 256
N_LAT_TILES = SEQ // ROW_TILE
N_ROW_TILES = M_ALL // ROW_TILE
MM_ROW_TILE = 768
ROUTE_TILE = 128
N_ROUTE_TILES = M_ALL // ROUTE_TILE
N_ROUTE_LAT = SEQ // ROUTE_TILE
Y_ROWS = CAP_ALL + 32


def _cparams(sem, vmem=VMEM_LIMIT):
    return pltpu.CompilerParams(dimension_semantics=sem, vmem_limit_bytes=vmem)


def _nt_dot(a, b):
    return lax.dot_general(a, b, (((1,), (1,)), ((), ())), preferred_element_type=F32)


def _tn_dot(a, b):
    return lax.dot_general(a, b, (((0,), (0,)), ((), ())), preferred_element_type=F32)


def _ada_kernel(c_ref, w_ref, b_ref, o_ref):
    c = c_ref[...]
    a = (c * jax.nn.sigmoid(c)).astype(BF16)
    o_ref[0] = jnp.dot(a, w_ref[0].astype(BF16), preferred_element_type=F32) + b_ref[0]


def ada_modulation(cc, w_ada, b_ada):
    depth, d, n = w_ada.shape
    tn = 1024
    return pl.pallas_call(
        _ada_kernel,
        out_shape=jax.ShapeDtypeStruct((depth, SUBLANES, n), F32),
        grid=(depth, n // tn),
        in_specs=[
            pl.BlockSpec((SUBLANES, d), lambda l, j: (0, 0)),
            pl.BlockSpec((1, d, tn), lambda l, j: (l, 0, j)),
            pl.BlockSpec((1, 1, tn), lambda l, j: (l, 0, j)),
        ],
        out_specs=pl.BlockSpec((1, SUBLANES, tn), lambda l, j: (l, 0, j)),
        compiler_params=_cparams(("arbitrary", "arbitrary")),
        name="ada_modulation",
    )(cc, w_ada, b_ada.reshape(depth, 1, n))


def _stream_vec_spec():
    return pl.BlockSpec((1, 1, D_MODEL), lambda i: (jnp.minimum(i // N_LAT_TILES, 1), 0, 0))


def _row_spec(width=D_MODEL):
    return pl.BlockSpec((ROW_TILE, width), lambda i: (i, 0))


def _modulate_kernel(x_ref, sh_ref, sc_ref, h_ref):
    h_ref[...] = (x_ref[...] * (1.0 + sc_ref[0]) + sh_ref[0]).astype(h_ref.dtype)


def modulate_rows(x, sh, sc):
    return pl.pallas_call(
        _modulate_kernel,
        out_shape=jax.ShapeDtypeStruct(x.shape, BF16),
        grid=(N_ROW_TILES,),
        in_specs=[_row_spec(), _stream_vec_spec(), _stream_vec_spec()],
        out_specs=_row_spec(),
        compiler_params=_cparams(("arbitrary",)),
        name="modulate_rows",
    )(x, sh, sc)


def _post_norm_kernel(*refs, with_router):
    if with_router:
        x_ref, y_ref, gate_ref, lg_ref, lb_ref, sh_ref, sc_ref, wr_ref, xo_ref, h_ref, aff_ref, aff3_ref = refs
    else:
        x_ref, y_ref, gate_ref, lg_ref, lb_ref, sh_ref, sc_ref, xo_ref, h_ref = refs
    z = DN_ALPHA * x_ref[...] + gate_ref[0] * y_ref[...]
    mu = jnp.mean(z, axis=-1, keepdims=True)
    zc = z - mu
    var = jnp.mean(zc * zc, axis=-1, keepdims=True)
    xn = zc * lax.rsqrt(var + LN_EPS) * lg_ref[...] + lb_ref[...]
    xo_ref[...] = xn
    h = (xn * (1.0 + sc_ref[0]) + sh_ref[0]).astype(BF16)
    h_ref[...] = h
    if with_router:
        logits = jnp.dot(h, wr_ref[...].astype(BF16), preferred_element_type=F32)
        lane = lax.broadcasted_iota(I32, logits.shape, 1)
        logits = jnp.where(lane < N_EXPERTS, logits, NEG_INF)
        e = jnp.exp(logits - jnp.max(logits, axis=-1, keepdims=True))
        aff = e / jnp.sum(e, axis=-1, keepdims=True)
        aff_ref[...] = aff
        hi = aff.astype(BF16).astype(F32)
        mid = (aff - hi).astype(BF16).astype(F32)
        lo = (aff - hi - mid).astype(BF16).astype(F32)
        aff3_ref[...] = (hi + pltpu.roll(mid, N_EXPERTS, 1) + pltpu.roll(lo, 2 * N_EXPERTS, 1)).astype(BF16)


def post_norm_rows(x, y, gate, ln_g, ln_b, sh, sc, w_router=None):
    with_router = w_router is not None
    vec = pl.BlockSpec((1, D_MODEL), lambda i: (0, 0))
    in_specs = [_row_spec(), _row_spec(), _stream_vec_spec(), vec, vec, _stream_vec_spec(), _stream_vec_spec()]
    out_shape = [jax.ShapeDtypeStruct(x.shape, F32), jax.ShapeDtypeStruct(x.shape, BF16)]
    out_specs = [_row_spec(), _row_spec()]
    args = [x, y, gate, ln_g.reshape(1, -1), ln_b.reshape(1, -1), sh, sc]
    if with_router:
        in_specs.append(pl.BlockSpec((D_MODEL, LANES), lambda i: (0, 0)))
        out_shape += [jax.ShapeDtypeStruct((x.shape[0], LANES), F32), jax.ShapeDtypeStruct((x.shape[0], LANES), BF16)]
        out_specs += [_row_spec(LANES), _row_spec(LANES)]
        args.append(jnp.pad(w_router, ((0, 0), (0, LANES - N_EXPERTS))))
    return pl.pallas_call(
        functools.partial(_post_norm_kernel, with_router=with_router),
        out_shape=out_shape,
        grid=(N_ROW_TILES,),
        in_specs=in_specs,
        out_specs=out_specs,
        compiler_params=_cparams(("arbitrary",)),
        name="post_norm_router" if with_router else "post_norm",
    )(*args)


def _matmul_kernel(*refs, splits, lead_scale):
    n_a = len(splits)
    a_refs, w_ref, o_ref, wb_ref = refs[:n_a], refs[n_a], refs[n_a + 1], refs[n_a + 2]

    @pl.when(pl.program_id(1) == 0)
    def _():
        wb_ref[...] = w_ref[...].astype(BF16)

    acc = None
    off = 0
    for a_ref, k in zip(a_refs, splits):
        part = jnp.dot(a_ref[...].astype(BF16), wb_ref[off:off + k, :], preferred_element_type=F32)
        acc = part if acc is None else acc + part
        off += k
    if lead_scale is not None:
        n_tiles, value = lead_scale
        acc = acc * jnp.where(pl.program_id(0) < n_tiles, value, 1.0)
    o_ref[...] = acc.astype(o_ref.dtype)


def matmul(a_list, w, out_dtype, tn, tm=MM_ROW_TILE, layer=None, n_cols=None, lead_scale=None):
    m = a_list[0].shape[0]
    splits = tuple(a.shape[1] for a in a_list)
    k, n = w.shape[-2:]
    n = n if n_cols is None else n_cols
    assert sum(splits) == k and n % tn == 0 and m % tm == 0
    in_specs = [pl.BlockSpec((tm, ki), lambda j, i: (i, 0)) for ki in splits]
    if layer is None:
        in_specs.append(pl.BlockSpec((k, tn), lambda j, i: (0, j)))
    else:
        in_specs.append(pl.BlockSpec((None, k, tn), lambda j, i: (layer, 0, j)))
    return pl.pallas_call(
        functools.partial(_matmul_kernel, splits=splits, lead_scale=lead_scale),
        out_shape=jax.ShapeDtypeStruct((m, n), out_dtype),
        grid=(n // tn, m // tm),
        in_specs=in_specs,
        out_specs=pl.BlockSpec((tm, tn), lambda j, i: (i, j)),
        scratch_shapes=[pltpu.VMEM((k, tn), BF16)],
        compiler_params=_cparams(("arbitrary", "arbitrary")),
        name="matmul",
    )(*a_list, w)


def rope_tables():
    t = jnp.arange(SEQ, dtype=I32)
    row = (t // GRID_W).astype(F32)
    col = (t % GRID_W).astype(F32)
    n_freq = MLA_ROPE // 4
    inv_freq = ROPE_BASE ** (-jnp.arange(n_freq, dtype=F32) / n_freq)
    ang = jnp.concatenate([row[:, None] * inv_freq, col[:, None] * inv_freq], axis=-1)
    cos, sin = jnp.cos(ang), jnp.sin(ang)
    half = MLA_ROPE // 2
    cos = jnp.concatenate([cos, jnp.ones((CTX_LEN, half), F32)], axis=0)
    sin = jnp.concatenate([sin, jnp.zeros((CTX_LEN, half), F32)], axis=0)
    z = jnp.zeros_like(cos)
    c_tab = jnp.concatenate([cos, cos, z, z], axis=1)
    sa_tab = jnp.concatenate([z, sin, z, z], axis=1)
    sb_tab = jnp.concatenate([-sin, z, z, z], axis=1)
    return c_tab, sa_tab, sb_tab


def _mla_proj_kernel(cq_ref, ckv_ref, kr_ref, qg_ref, kvg_ref, wq_ref, wkv_ref, c_ref, sa_ref, sb_ref,
                     q_out, k_out, vt_out, wqb, wkvb):
    @pl.when(pl.program_id(0) == 0)
    def _():
        wqb[...] = wq_ref[...].astype(BF16)
        wkvb[...] = wkv_ref[...].astype(BF16)

    def rms(x, g):
        return x * lax.rsqrt(jnp.mean(x * x, axis=-1, keepdims=True) + RMS_EPS) * g

    c_tab, sa_tab, sb_tab = c_ref[...], sa_ref[...], sb_ref[...]
    half = MLA_ROPE // 2

    def rope(g):
        return g * c_tab + pltpu.roll(g, half, 1) * sa_tab + pltpu.roll(g, LANES - half, 1) * sb_tab

    qk_scale = (MLA_NOPE + MLA_ROPE) ** -0.5 * LOG2_E
    q = jnp.dot(rms(cq_ref[...], qg_ref[...]).astype(BF16), wqb[...], preferred_element_type=F32) * qk_scale
    kv = jnp.dot(rms(ckv_ref[...], kvg_ref[...]).astype(BF16), wkvb[...], preferred_element_type=F32)
    k_rope = rope(kr_ref[...]).astype(BF16)
    for h in range(MLA_HEADS):
        lo = h * MLA_QK_PAD
        q_out[:, lo:lo + LANES] = q[:, lo:lo + LANES].astype(BF16)
        q_out[:, lo + LANES:lo + 2 * LANES] = rope(q[:, lo + LANES:lo + 2 * LANES]).astype(BF16)
        k_out[:, lo:lo + LANES] = kv[:, h * MLA_NOPE:(h + 1) * MLA_NOPE].astype(BF16)
        k_out[:, lo + LANES:lo + 2 * LANES] = k_rope
    vt_out[...] = kv[:, MLA_HEADS * MLA_NOPE:].T.astype(BF16)


def mla_projections(u_main, kr, qa_g, kva_g, w_uq, w_ukv, tabs):
    m = u_main.shape[0]
    tm = 384
    hq = MLA_HEADS * MLA_QK_PAD
    wq = w_uq.reshape(MLA_Q_RANK, MLA_HEADS, MLA_NOPE + MLA_ROPE)
    wq = jnp.pad(wq, ((0, 0), (0, 0), (0, MLA_QK_PAD - MLA_NOPE - MLA_ROPE))).reshape(MLA_Q_RANK, hq)
    wkv = w_ukv.reshape(MLA_KV_RANK, MLA_HEADS, MLA_NOPE + MLA_V)
    wkv = jnp.concatenate([wkv[..., :MLA_NOPE].reshape(MLA_KV_RANK, -1), wkv[..., MLA_NOPE:].reshape(MLA_KV_RANK, -1)], axis=1)
    nkv = wkv.shape[1]
    cq_blk = 2 * CONV_CH // MLA_Q_RANK
    tab_spec = pl.BlockSpec((tm, LANES), lambda i: (i, 0))
    return pl.pallas_call(
        _mla_proj_kernel,
        out_shape=[jax.ShapeDtypeStruct((m, hq), BF16), jax.ShapeDtypeStruct((m, hq), BF16),
                   jax.ShapeDtypeStruct((MLA_HEADS * MLA_V, m), BF16)],
        grid=(m // tm,),
        in_specs=[
            pl.BlockSpec((tm, MLA_Q_RANK), lambda i: (i, cq_blk)),
            pl.BlockSpec((tm, MLA_KV_RANK), lambda i: (i, cq_blk + 1)),
            pl.BlockSpec((tm, LANES), lambda i: (i, 0)),
            pl.BlockSpec((1, MLA_Q_RANK), lambda i: (0, 0)),
            pl.BlockSpec((1, MLA_KV_RANK), lambda i: (0, 0)),
            pl.BlockSpec((MLA_Q_RANK, hq), lambda i: (0, 0)),
            pl.BlockSpec((MLA_KV_RANK, nkv), lambda i: (0, 0)),
            tab_spec, tab_spec, tab_spec,
        ],
        out_specs=[pl.BlockSpec((tm, hq), lambda i: (i, 0)), pl.BlockSpec((tm, hq), lambda i: (i, 0)),
                   pl.BlockSpec((MLA_HEADS * MLA_V, tm), lambda i: (0, i))],
        scratch_shapes=[pltpu.VMEM((MLA_Q_RANK, hq), BF16), pltpu.VMEM((MLA_KV_RANK, nkv), BF16)],
        compiler_params=_cparams(("arbitrary",)),
        name="mla_projections",
    )(u_main, u_main, kr, qa_g.reshape(1, -1), kva_g.reshape(1, -1), wq, wkv, *tabs)


MLA_KEY_CHUNK = 1408
MLA_Q_TILE = 512


def _flash_kernel(q_ref, k_ref, vt_ref, o_ref, *, tk, n_chunks):
    q = q_ref[...]
    tq = q.shape[0]
    m = jnp.full((1, tq), NEG_INF, F32)
    l = jnp.zeros((1, tq), F32)
    acc = jnp.zeros((vt_ref.shape[0], tq), F32)
    s_next = _nt_dot(k_ref[0:tk, :], q)
    for c in range(n_chunks):
        s = s_next
        if c + 1 < n_chunks:
            s_next = _nt_dot(k_ref[(c + 1) * tk:(c + 2) * tk, :], q)
        m_new = jnp.maximum(m, jnp.max(s, axis=0, keepdims=True))
        alpha = jnp.exp2(m - m_new)
        p = jnp.exp2(s - m_new)
        l = alpha * l + jnp.sum(p, axis=0, keepdims=True)
        acc = alpha * acc + jnp.dot(vt_ref[:, c * tk:(c + 1) * tk], p.astype(BF16), preferred_element_type=F32)
        m = m_new
    o_ref[...] = (acc / l).T.astype(o_ref.dtype)


def mla_attention(q_cat, k_cat, vt):
    m = k_cat.shape[0]
    tq, tk = MLA_Q_TILE, MLA_KEY_CHUNK
    return pl.pallas_call(
        functools.partial(_flash_kernel, tk=tk, n_chunks=m // tk),
        out_shape=jax.ShapeDtypeStruct((SEQ, MLA_HEADS * MLA_V), BF16),
        grid=(MLA_HEADS, SEQ // tq),
        in_specs=[
            pl.BlockSpec((tq, MLA_QK_PAD), lambda h, i: (i, h)),
            pl.BlockSpec((m, MLA_QK_PAD), lambda h, i: (0, h)),
            pl.BlockSpec((MLA_V, m), lambda h, i: (h, 0)),
        ],
        out_specs=pl.BlockSpec((tq, MLA_V), lambda h, i: (i, h)),
        compiler_params=_cparams(("arbitrary", "arbitrary")),
        name="mla_attention",
    )(q_cat, k_cat, vt)


def _ctx_attn_kernel(q_ref, k_ref, v_ref, o_ref, *, scale, base2, v_transposed):
    s = _nt_dot(q_ref[...], k_ref[...]) * scale
    z = s - jnp.max(s, axis=-1, keepdims=True)
    p = jnp.exp2(z) if base2 else jnp.exp(z)
    l = jnp.sum(p, axis=-1, keepdims=True)
    pv = _nt_dot(p.astype(BF16), v_ref[...]) if v_transposed else jnp.dot(p.astype(BF16), v_ref[...],
                                                                         preferred_element_type=F32)
    o_ref[...] = (pv / l).astype(o_ref.dtype)


def context_attention(q_arr, k_arr, v_arr, heads, dq, dv, q_col, k_col, v_col, scale, base2=False,
                      v_transposed=False):
    rb = SEQ // CTX_LEN
    if v_transposed:
        v_spec = pl.BlockSpec((dv, CTX_LEN), lambda h: (v_col + h, rb))
    else:
        v_spec = pl.BlockSpec((CTX_LEN, dv), lambda h: (rb, v_col + h))
    return pl.pallas_call(
        functools.partial(_ctx_attn_kernel, scale=scale, base2=base2, v_transposed=v_transposed),
        out_shape=jax.ShapeDtypeStruct((CTX_LEN, heads * dv), BF16),
        grid=(heads,),
        in_specs=[
            pl.BlockSpec((CTX_LEN, dq), lambda h: (rb, q_col + h)),
            pl.BlockSpec((CTX_LEN, dq), lambda h: (rb, k_col + h)),
            v_spec,
        ],
        out_specs=pl.BlockSpec((CTX_LEN, dv), lambda h: (0, h)),
        compiler_params=_cparams(("arbitrary",)),
        name="context_attention",
    )(q_arr, k_arr, v_arr)


CONV_HALO = 16
CONV_ROW_BLOCK = 64


def _conv_kernel(a_ref, g_ref, ap_ref, gp_ref, an_ref, gn_ref, w_ref, b_ref, cg_ref, cb_ref, o_ref, hbuf, cbuf):
    i = pl.program_id(0)
    tl = a_ref.shape[0]

    def glu(a, g):
        return a[...] * jax.nn.sigmoid(g[...])

    has_prev = jnp.logical_and(i != 0, i != N_LAT_TILES)
    has_next = jnp.logical_and(i != N_LAT_TILES - 1, i != N_ROW_TILES - 1)
    hbuf[0:CONV_HALO, :] = jnp.where(has_prev, glu(ap_ref, gp_ref), 0.0)
    hbuf[CONV_HALO:CONV_HALO + tl, :] = glu(a_ref, g_ref)
    hbuf[CONV_HALO + tl:, :] = jnp.where(has_next, glu(an_ref, gn_ref), 0.0)

    base = CONV_HALO - CONV_PAD
    for rb in range(tl // CONV_ROW_BLOCK):
        r0 = rb * CONV_ROW_BLOCK
        for c in range(CONV_CH // LANES):
            cs = slice(c * LANES, (c + 1) * LANES)
            acc = jnp.broadcast_to(b_ref[:, cs], (CONV_ROW_BLOCK, LANES))
            for j in range(CONV_WIDTH):
                acc = acc + hbuf[r0 + base + j:r0 + base + j + CONV_ROW_BLOCK, cs] * w_ref[j:j + 1, cs]
            cbuf[r0:r0 + CONV_ROW_BLOCK, cs] = acc

    y = cbuf[...]
    mu = jnp.mean(y, axis=-1, keepdims=True)
    yc = y - mu
    var = jnp.mean(yc * yc, axis=-1, keepdims=True)
    yn = yc * lax.rsqrt(var + LN_EPS) * cg_ref[...] + cb_ref[...]
    o_ref[...] = (yn * jax.nn.sigmoid(yn)).astype(o_ref.dtype)


def conformer_conv(u_main, w_dw, b_dw, cn_g, cn_b):
    m = u_main.shape[0]
    tl = ROW_TILE
    hpt = tl // CONV_HALO
    n_halo = m // CONV_HALO
    main = lambda col: pl.BlockSpec((tl, CONV_CH), lambda i: (i, col))
    prev = lambda col: pl.BlockSpec((CONV_HALO, CONV_CH), lambda i: (jnp.maximum(i * hpt - 1, 0), col))
    nxt = lambda col: pl.BlockSpec((CONV_HALO, CONV_CH), lambda i: (jnp.minimum((i + 1) * hpt, n_halo - 1), col))
    vec = pl.BlockSpec((1, CONV_CH), lambda i: (0, 0))
    return pl.pallas_call(
        _conv_kernel,
        out_shape=jax.ShapeDtypeStruct((m, CONV_CH), BF16),
        grid=(m // tl,),
        in_specs=[main(0), main(1), prev(0), prev(1), nxt(0), nxt(1),
                  pl.BlockSpec((CONV_WIDTH, CONV_CH), lambda i: (0, 0)), vec, vec, vec],
        out_specs=pl.BlockSpec((tl, CONV_CH), lambda i: (i, 0)),
        scratch_shapes=[pltpu.VMEM((tl + 2 * CONV_HALO, CONV_CH), F32), pltpu.VMEM((tl, CONV_CH), F32)],
        compiler_params=_cparams(("arbitrary",)),
        name="conformer_conv",
    )(u_main, u_main, u_main, u_main, u_main, u_main, w_dw, b_dw.reshape(1, -1), cn_g.reshape(1, -1), cn_b.reshape(1, -1))


NA_Q_ROWS = 8
NA_K_ROWS = 16
NA_KBLK = 4 * GRID_W
NA_QTOK = NA_Q_ROWS * GRID_W
NA_KTOK = NA_K_ROWS * GRID_W
NA_GROUPS = SEQ // NA_QTOK


NA_HEADS_PER_STEP = 2
NA_CLASSES = (0, 1, NA_GROUPS - 1)


def _na_window_rows():
    rows = SEQ // GRID_W
    out = []
    for g in NA_CLASSES:
        qr = (NA_Q_ROWS * g + np.arange(NA_Q_ROWS))[:, None]
        kr = (NA_Q_ROWS * g - (NA_K_ROWS - NA_Q_ROWS) // 2 + np.arange(NA_K_ROWS))[None, :]
        start = np.clip(qr - NA_WIN_ROWS // 2, 0, rows - NA_WIN_ROWS)
        out.append((kr >= start) & (kr < start + NA_WIN_ROWS))
    return np.stack(out)


def na_bias_strips(rpb):
    qc = np.arange(GRID_W)[:, None]
    kc = np.arange(GRID_W)[None, :]
    ws = np.clip(qc - NA_WIN_COLS // 2, 0, GRID_W - NA_WIN_COLS)
    col_valid = (kc >= ws) & (kc < ws + NA_WIN_COLS)
    col_idx = np.clip(kc - qc + NA_WIN_COLS - 1, 0, 2 * NA_WIN_COLS - 2)
    onehot = jnp.asarray((col_idx[None] == np.arange(2 * NA_WIN_COLS - 1)[:, None, None]).astype(np.float32))
    cm = jnp.einsum('hrd,dqk->hrqk', rpb, onehot, precision=lax.Precision.HIGHEST) * LOG2_E
    cm = jnp.where(col_valid[None, None], cm, NEG_INF)
    neg = jnp.full_like(cm, NEG_INF)
    return jnp.concatenate([cm, neg], axis=-1), jnp.concatenate([neg, cm], axis=-1)


def _na_kernel(q_ref, k0, k1, k2, k3, v0, v1, v2, v3, kc_ref, vc_ref, bl_ref, br_ref, o_ref, tbl_ref):
    g = pl.program_id(1)
    row_valid = _na_window_rows()

    @pl.when(g == 0)
    def _():
        neg_tile = jnp.full((GRID_W, LANES), NEG_INF, F32)
        for hh in range(NA_HEADS_PER_STEP):
            for c in range(len(NA_CLASSES)):
                for ql in range(NA_Q_ROWS):
                    for kp in range(NA_K_ROWS // 2):
                        ka, kb = 2 * kp, 2 * kp + 1
                        da = ka - ql + NA_WIN_ROWS - 1 - (NA_K_ROWS - NA_Q_ROWS) // 2
                        ta = bl_ref[hh, da] if row_valid[c, ql, ka] else neg_tile
                        tb = br_ref[hh, da + 1] if row_valid[c, ql, kb] else neg_tile
                        tbl_ref[hh, c, ql * GRID_W:(ql + 1) * GRID_W, kp * LANES:(kp + 1) * LANES] = jnp.maximum(ta, tb)

    cls = jnp.where(g == 0, 0, jnp.where(g == NA_GROUPS - 1, 2, 1))
    for hh in range(NA_HEADS_PER_STEP):
        hs = slice(hh * NA_HEAD_DIM, (hh + 1) * NA_HEAD_DIM)
        q = q_ref[:, hs]
        s = jnp.concatenate([_nt_dot(q, k[:, hs]) for k in (k0, k1, k2, k3)], axis=1) + tbl_ref[hh, cls]
        sc = _nt_dot(q, kc_ref[:, hs])
        m = jnp.maximum(jnp.max(s, axis=-1, keepdims=True), jnp.max(sc, axis=-1, keepdims=True))
        p = jnp.exp2(s - m)
        pc = jnp.exp2(sc - m)
        l = jnp.sum(p, axis=-1, keepdims=True) + jnp.sum(pc, axis=-1, keepdims=True)
        o = jnp.dot(pc.astype(BF16), vc_ref[:, hs], preferred_element_type=F32)
        for j, v in enumerate((v0, v1, v2, v3)):
            o = o + jnp.dot(p[:, j * NA_KBLK:(j + 1) * NA_KBLK].astype(BF16), v[:, hs], preferred_element_type=F32)
        o_ref[:, hs] = (o / l).astype(o_ref.dtype)


def neighbourhood_attention(qkv, bias_left, bias_right):
    n_kblk = SEQ // NA_KBLK
    ctx_blk = SEQ // CTX_LEN
    hp = NA_HEADS_PER_STEP
    width = hp * NA_HEAD_DIM
    n_pairs = NA_HEADS // hp

    def kv_spec(j, col0):
        return pl.BlockSpec((NA_KBLK, width), lambda h, g: (jnp.clip(2 * g - 1 + j, 0, n_kblk - 1), col0 + h))

    strip_spec = pl.BlockSpec((hp, 2 * NA_WIN_ROWS - 1, GRID_W, LANES), lambda h, g: (h, 0, 0, 0))
    return pl.pallas_call(
        _na_kernel,
        out_shape=jax.ShapeDtypeStruct((SEQ, NA_HEADS * NA_HEAD_DIM), BF16),
        grid=(n_pairs, NA_GROUPS),
        in_specs=[pl.BlockSpec((NA_QTOK, width), lambda h, g: (g, h))]
        + [kv_spec(j, n_pairs) for j in range(4)]
        + [kv_spec(j, 2 * n_pairs) for j in range(4)]
        + [pl.BlockSpec((CTX_LEN, width), lambda h, g: (ctx_blk, n_pairs + h)),
           pl.BlockSpec((CTX_LEN, width), lambda h, g: (ctx_blk, 2 * n_pairs + h)),
           strip_spec, strip_spec],
        out_specs=pl.BlockSpec((NA_QTOK, width), lambda h, g: (g, h)),
        scratch_shapes=[pltpu.VMEM((hp, len(NA_CLASSES), NA_QTOK, NA_KTOK), F32)],
        compiler_params=_cparams(("arbitrary", "arbitrary")),
        name="neighbourhood_attention",
    )(*([qkv] * 11), bias_left, bias_right)


def _select_kernel(aff_ref, posm_ref, *, cap):
    bits = pltpu.bitcast(aff_ref[...], I32)
    n = bits.shape[1]

    def search(i, thr):
        cand = thr | (jnp.int32(1) << (30 - i))
        cnt = jnp.sum((bits >= cand).astype(I32), axis=1, keepdims=True)
        return jnp.where(cnt >= cap, cand, thr)

    thr = lax.fori_loop(0, 31, search, jnp.zeros((N_EXPERTS, 1), I32))
    gt = bits > thr
    eq = bits == thr
    need = (cap - jnp.sum(gt.astype(I32), axis=1, keepdims=True)).astype(F32)
    tri = jnp.where(lax.broadcasted_iota(I32, (LANES, LANES), 0) <= lax.broadcasted_iota(I32, (LANES, LANES), 1),
                    1.0, 0.0).astype(BF16)
    off_eq = jnp.zeros((N_EXPERTS, 1), F32)
    off_sel = jnp.zeros((N_EXPERTS, 1), F32)
    for b in range(n // LANES):
        cs = slice(b * LANES, (b + 1) * LANES)
        eq_b = jnp.where(eq[:, cs], 1.0, 0.0)
        inc_eq = jnp.dot(eq_b.astype(BF16), tri, preferred_element_type=F32)
        rank = off_eq + inc_eq - eq_b
        sel_b = jnp.logical_or(gt[:, cs], jnp.logical_and(eq[:, cs], rank < need))
        sel_f = jnp.where(sel_b, 1.0, 0.0)
        inc_sel = jnp.dot(sel_f.astype(BF16), tri, preferred_element_type=F32)
        pos = off_sel + inc_sel - sel_f
        posm_ref[:, cs] = jnp.where(sel_b, pos.astype(I32), -1)
        off_eq = off_eq + inc_eq[:, LANES - 1:LANES]
        off_sel = off_sel + inc_sel[:, LANES - 1:LANES]


def expert_choice_select(aff_t, cap):
    return pl.pallas_call(
        functools.partial(_select_kernel, cap=cap),
        out_shape=jax.ShapeDtypeStruct(aff_t.shape, I32),
        compiler_params=_cparams(None),
        name="expert_choice_select",
    )(aff_t)


def route_tile_meta(posm_all):
    cnt = (posm_all >= 0).astype(I32).reshape(N_EXPERTS, N_ROUTE_TILES, ROUTE_TILE).sum(-1).T
    lat, ctx = cnt[:N_ROUTE_LAT], cnt[N_ROUTE_LAT:]
    s0 = jnp.concatenate([jnp.cumsum(lat, 0) - lat, CAP_LAT + jnp.cumsum(ctx, 0) - ctx], axis=0)
    return s0.reshape(-1), cnt.reshape(-1)


DISPATCH_PIECE = 16
DISPATCH_BLOCK = 256
DISPATCH_ROWS = N_EXPERTS * ((ROUTE_TILE + SUBLANES - 1 + DISPATCH_PIECE - 1) // DISPATCH_PIECE) * DISPATCH_PIECE


XIN_WIDTH = D_MODEL + LANES


def _dispatch_kernel(s0_ref, cnt_ref, h_ref, posm_ref, aff3_ref, xin_ref, pt_ref, x_ref, carry_ref, sem):
    t = pl.program_id(0)

    @pl.when(t == 0)
    def _():
        carry_ref[...] = jnp.zeros(carry_ref.shape, F32)

    pt_ref[...] = jnp.zeros(pt_ref.shape, pt_ref.dtype)
    iota = lax.broadcasted_iota(I32, (DISPATCH_PIECE, ROUTE_TILE), 0)
    offs = []
    o = jnp.int32(0)
    for e in range(N_EXPERTS):
        n = cnt_ref[t * N_EXPERTS + e]
        s0 = s0_ref[t * N_EXPERTS + e]
        a0 = (s0 // SUBLANES) * SUBLANES
        npc = jnp.where(n > 0, (s0 + n - a0 + DISPATCH_PIECE - 1) // DISPATCH_PIECE, 0)
        row = posm_ref[e:e + 1, :]
        offs.append(o)

        def piece(p, carry, row=row, a0=a0, o=o):
            r0 = pl.multiple_of(o + p * DISPATCH_PIECE, DISPATCH_PIECE)
            hit = row == (a0 + p * DISPATCH_PIECE + iota)
            pt_ref[pl.ds(r0, DISPATCH_PIECE), :] = jnp.where(hit, 1.0, 0.0).astype(BF16)
            return carry

        lax.fori_loop(0, npc, piece, 0)
        o = o + npc * DISPATCH_PIECE

    def block(b, carry):
        r0 = pl.multiple_of(b * DISPATCH_BLOCK, DISPATCH_BLOCK)
        onehot = pt_ref[pl.ds(r0, DISPATCH_BLOCK), :]
        x_ref[pl.ds(r0, DISPATCH_BLOCK), 0:D_MODEL] = jnp.dot(onehot, h_ref[...], preferred_element_type=F32)
        x_ref[pl.ds(r0, DISPATCH_BLOCK), D_MODEL:] = jnp.dot(onehot, aff3_ref[...], preferred_element_type=F32)
        return carry

    lax.fori_loop(0, (o + DISPATCH_BLOCK - 1) // DISPATCH_BLOCK, block, 0)

    def tile_copy(src_row, e, dst_row):
        return pltpu.make_async_copy(x_ref.at[pl.ds(src_row, SUBLANES)], xin_ref.at[e, pl.ds(dst_row, SUBLANES)], sem)

    total = jnp.int32(0)
    for e in range(N_EXPERTS):
        n = cnt_ref[t * N_EXPERTS + e]
        s0 = s0_ref[t * N_EXPERTS + e]
        a0 = (s0 // SUBLANES) * SUBLANES
        span = s0 + n - a0
        full = jnp.where(n > 0, span // SUBLANES, 0)
        o_e = offs[e]

        @pl.when(n > 0)
        def _(e=e, a0=a0, span=span, full=full, o_e=o_e):
            head = pl.multiple_of(o_e, SUBLANES)
            x_ref[pl.ds(head, SUBLANES), :] += carry_ref[e]

            def issue(g, carry):
                tile_copy(pl.multiple_of(o_e + g * SUBLANES, SUBLANES), e,
                          pl.multiple_of(a0 + g * SUBLANES, SUBLANES)).start()
                return carry

            lax.fori_loop(0, full, issue, 0)
            tail = x_ref[pl.ds(pl.multiple_of(o_e + full * SUBLANES, SUBLANES), SUBLANES), :]
            carry_ref[e] = jnp.where(span - full * SUBLANES > 0, tail, 0.0)

        total = total + full

    def drain(g, carry):
        tile_copy(0, 0, 0).wait()
        return carry

    lax.fori_loop(0, total, drain, 0)


def moe_dispatch(h, posm_all, aff3, s0, cnt):
    return pl.pallas_call(
        _dispatch_kernel,
        out_shape=jax.ShapeDtypeStruct((N_EXPERTS, CAP_ALL, XIN_WIDTH), F32),
        grid_spec=pltpu.PrefetchScalarGridSpec(
            num_scalar_prefetch=2,
            grid=(N_ROUTE_TILES,),
            in_specs=[pl.BlockSpec((ROUTE_TILE, D_MODEL), lambda t, s0, cnt: (t, 0)),
                      pl.BlockSpec((N_EXPERTS, ROUTE_TILE), lambda t, s0, cnt: (0, t)),
                      pl.BlockSpec((ROUTE_TILE, LANES), lambda t, s0, cnt: (t, 0))],
            out_specs=pl.BlockSpec(memory_space=pl.ANY),
            scratch_shapes=[pltpu.VMEM((DISPATCH_ROWS, ROUTE_TILE), BF16),
                            pltpu.VMEM((DISPATCH_ROWS + SUBLANES, XIN_WIDTH), F32),
                            pltpu.VMEM((N_EXPERTS, SUBLANES, XIN_WIDTH), F32),
                            pltpu.SemaphoreType.DMA(())]),
        compiler_params=_cparams(("arbitrary",)),
        name="moe_dispatch",
    )(s0, cnt, h, posm_all, aff3)


def _ffn_up_kernel(x_ref, wg_ref, wu_ref, o_ref, ag_ref, au_ref):
    k = pl.program_id(1)
    x = x_ref[0].astype(BF16)
    pg = jnp.dot(x, wg_ref[0].astype(BF16), preferred_element_type=F32)
    pu = jnp.dot(x, wu_ref[0].astype(BF16), preferred_element_type=F32)

    @pl.when(k == 0)
    def _():
        ag_ref[...] = pg
        au_ref[...] = pu

    @pl.when(k > 0)
    def _():
        ag_ref[...] += pg
        au_ref[...] += pu

    @pl.when(k == pl.num_programs(1) - 1)
    def _():
        a = ag_ref[...]
        o_ref[0] = (a * jax.nn.sigmoid(a) * au_ref[...]).astype(o_ref.dtype)


def expert_ffn_up(xin, w_gate, w_up, layer):
    e, cap, _ = xin.shape
    d, ff = w_gate.shape[2:]
    tk = 1024
    return pl.pallas_call(
        _ffn_up_kernel,
        out_shape=jax.ShapeDtypeStruct((e, cap, ff), BF16),
        grid=(e, d // tk),
        in_specs=[pl.BlockSpec((1, cap, tk), lambda i, k: (i, 0, k)),
                  pl.BlockSpec((None, 1, tk, ff), lambda i, k: (layer, i, k, 0)),
                  pl.BlockSpec((None, 1, tk, ff), lambda i, k: (layer, i, k, 0))],
        out_specs=pl.BlockSpec((1, cap, ff), lambda i, k: (i, 0, 0)),
        scratch_shapes=[pltpu.VMEM((cap, ff), F32), pltpu.VMEM((cap, ff), F32)],
        compiler_params=_cparams(("arbitrary", "arbitrary"), vmem=62 * 1024 * 1024),
        name="expert_ffn_up",
    )(xin, w_gate, w_up)


def _ffn_down_kernel(a_ref, w_ref, g_ref, hi_ref, lo_ref):
    cap = a_ref.shape[1]
    g = g_ref[0]
    lane = lax.broadcasted_iota(I32, g.shape, 1)
    mine = jnp.logical_and(lane % N_EXPERTS == pl.program_id(0), lane < 3 * N_EXPERTS)
    gate = jnp.sum(jnp.where(mine, g, 0.0), axis=1, keepdims=True)
    y = jnp.dot(a_ref[0], w_ref[0].astype(BF16), preferred_element_type=F32) * gate
    hi = y.astype(BF16)
    hi_ref[0, 0:cap, :] = hi
    lo_ref[0, 0:cap, :] = (y - hi.astype(F32)).astype(BF16)
    pad = jnp.zeros((hi_ref.shape[1] - cap, hi_ref.shape[2]), BF16)
    hi_ref[0, cap:, :] = pad
    lo_ref[0, cap:, :] = pad


def expert_ffn_down(act, xin, w_down, layer):
    e, cap, ff = act.shape
    d = w_down.shape[3]
    tn = 512
    out = jax.ShapeDtypeStruct((e, Y_ROWS, d), BF16)
    out_spec = pl.BlockSpec((1, Y_ROWS, tn), lambda i, j: (i, 0, j))
    return pl.pallas_call(
        _ffn_down_kernel,
        out_shape=[out, out],
        grid=(e, d // tn),
        in_specs=[pl.BlockSpec((1, cap, ff), lambda i, j: (i, 0, 0)),
                  pl.BlockSpec((None, 1, ff, tn), lambda i, j: (layer, i, 0, j)),
                  pl.BlockSpec((1, cap, LANES), lambda i, j: (i, 0, D_MODEL // LANES))],
        out_specs=[out_spec, out_spec],
        compiler_params=_cparams(("arbitrary", "arbitrary")),
        name="expert_ffn_down",
    )(act, w_down, xin)


COMBINE_PIECE = 32
COMBINE_ALIGN = 16
COMBINE_BLOCK = 256
COMBINE_PIECES_PER_BLOCK = COMBINE_BLOCK // COMBINE_PIECE
COMBINE_ROWS = N_EXPERTS * ((ROUTE_TILE + COMBINE_ALIGN - 1 + COMBINE_PIECE - 1) // COMBINE_PIECE) * COMBINE_PIECE
COMBINE_BLOCKS = COMBINE_ROWS // COMBINE_BLOCK


def _combine_kernel(s0_ref, cnt_ref, yh_ref, yl_ref, posm_ref, x_ref, gate_ref, lg_ref, lb_ref, sh_ref, sc_ref,
                    xo_ref, h_ref, sth_ref, stl_ref, pt_ref, acc_ref, sems):
    t = pl.program_id(0)
    iota = lax.broadcasted_iota(I32, (COMBINE_PIECE, ROUTE_TILE), 0)

    def window_copies(e, w, k):
        r0 = pl.multiple_of(k * COMBINE_PIECE, COMBINE_PIECE)
        sem = sems.at[k // COMBINE_PIECES_PER_BLOCK]
        return (pltpu.make_async_copy(yh_ref.at[e, pl.ds(w, COMBINE_PIECE)], sth_ref.at[pl.ds(r0, COMBINE_PIECE)], sem),
                pltpu.make_async_copy(yl_ref.at[e, pl.ds(w, COMBINE_PIECE)], stl_ref.at[pl.ds(r0, COMBINE_PIECE)], sem))

    kk = jnp.int32(0)
    for e in range(N_EXPERTS):
        n = cnt_ref[t * N_EXPERTS + e]
        s0 = s0_ref[t * N_EXPERTS + e]
        a0 = (s0 // COMBINE_ALIGN) * COMBINE_ALIGN
        npc = jnp.where(n > 0, (s0 + n - a0 + COMBINE_PIECE - 1) // COMBINE_PIECE, 0)
        prow = posm_ref[e:e + 1, :]

        def piece(p, carry, e=e, a0=a0, kk=kk, prow=prow):
            w = pl.multiple_of(a0 + p * COMBINE_PIECE, COMBINE_ALIGN)
            k = kk + p
            for cp in window_copies(e, w, k):
                cp.start()
            r0 = pl.multiple_of(k * COMBINE_PIECE, COMBINE_PIECE)
            pt_ref[pl.ds(r0, COMBINE_PIECE), :] = jnp.where(prow == (w + iota), 1.0, 0.0).astype(BF16)
            return carry

        lax.fori_loop(0, npc, piece, 0)
        kk = kk + npc

    nblk = (kk + COMBINE_PIECES_PER_BLOCK - 1) // COMBINE_PIECES_PER_BLOCK

    def zero_tail(p, carry):
        r0 = pl.multiple_of((kk + p) * COMBINE_PIECE, COMBINE_PIECE)
        sth_ref[pl.ds(r0, COMBINE_PIECE), :] = jnp.zeros((COMBINE_PIECE, D_MODEL), BF16)
        stl_ref[pl.ds(r0, COMBINE_PIECE), :] = jnp.zeros((COMBINE_PIECE, D_MODEL), BF16)
        pt_ref[pl.ds(r0, COMBINE_PIECE), :] = jnp.zeros((COMBINE_PIECE, ROUTE_TILE), BF16)
        return carry

    lax.fori_loop(0, nblk * COMBINE_PIECES_PER_BLOCK - kk, zero_tail, 0)
    acc_ref[...] = jnp.zeros(acc_ref.shape, F32)

    def block(b, carry):
        def drain(p, c):
            for cp in window_copies(0, 0, b * COMBINE_PIECES_PER_BLOCK):
                cp.wait()
            return c

        lax.fori_loop(0, jnp.minimum(kk - b * COMBINE_PIECES_PER_BLOCK, COMBINE_PIECES_PER_BLOCK), drain, 0)
        r0 = pl.multiple_of(b * COMBINE_BLOCK, COMBINE_BLOCK)
        onehot = pt_ref[pl.ds(r0, COMBINE_BLOCK), :]
        acc_ref[...] += (_tn_dot(onehot, sth_ref[pl.ds(r0, COMBINE_BLOCK), :])
                         + _tn_dot(onehot, stl_ref[pl.ds(r0, COMBINE_BLOCK), :]))
        return carry

    lax.fori_loop(0, nblk, block, 0)

    z = DN_ALPHA * x_ref[...] + gate_ref[0] * acc_ref[...]
    mu = jnp.mean(z, axis=-1, keepdims=True)
    zc = z - mu
    var = jnp.mean(zc * zc, axis=-1, keepdims=True)
    xn = zc * lax.rsqrt(var + LN_EPS) * lg_ref[...] + lb_ref[...]
    xo_ref[...] = xn
    h_ref[...] = (xn * (1.0 + sc_ref[0]) + sh_ref[0]).astype(h_ref.dtype)


def moe_combine_post_norm(yh, yl, posm_all, s0, cnt, x, gate, ln_g, ln_b, sh, sc):
    tile = lambda width: pl.BlockSpec((ROUTE_TILE, width), lambda t, s0, cnt: (t, 0))
    route = pl.BlockSpec((N_EXPERTS, ROUTE_TILE), lambda t, s0, cnt: (0, t))
    stream_vec = pl.BlockSpec((1, 1, D_MODEL), lambda t, s0, cnt: (jnp.minimum(t // N_ROUTE_LAT, 1), 0, 0))
    vec = pl.BlockSpec((1, D_MODEL), lambda t, s0, cnt: (0, 0))
    return pl.pallas_call(
        _combine_kernel,
        out_shape=[jax.ShapeDtypeStruct((M_ALL, D_MODEL), F32), jax.ShapeDtypeStruct((M_ALL, D_MODEL), BF16)],
        grid_spec=pltpu.PrefetchScalarGridSpec(
            num_scalar_prefetch=2,
            grid=(N_ROUTE_TILES,),
            in_specs=[pl.BlockSpec(memory_space=pl.ANY), pl.BlockSpec(memory_space=pl.ANY), route,
                      tile(D_MODEL), stream_vec, vec, vec, stream_vec, stream_vec],
            out_specs=[tile(D_MODEL), tile(D_MODEL)],
            scratch_shapes=[pltpu.VMEM((COMBINE_ROWS, D_MODEL), BF16),
                            pltpu.VMEM((COMBINE_ROWS, D_MODEL), BF16),
                            pltpu.VMEM((COMBINE_ROWS, ROUTE_TILE), BF16),
                            pltpu.VMEM((ROUTE_TILE, D_MODEL), F32),
                            pltpu.SemaphoreType.DMA((COMBINE_BLOCKS,))]),
        compiler_params=_cparams(("arbitrary",)),
        name="moe_combine_post_norm",
    )(s0, cnt, yh, yl, posm_all, x, gate, ln_g.reshape(1, -1), ln_b.reshape(1, -1), sh, sc)


def expert_choice_moe_post_norm(h, aff, aff3, w_gate, w_up, w_down, layer, x, gate, ln_g, ln_b, sh, sc):
    aff_t = aff[:, :N_EXPERTS].T
    posm_lat = expert_choice_select(aff_t[:, :SEQ], CAP_LAT)
    posm_ctx = expert_choice_select(aff_t[:, SEQ:], CAP_CTX)
    posm_all = jnp.concatenate([posm_lat, jnp.where(posm_ctx >= 0, posm_ctx + CAP_LAT, -1)], axis=1)
    s0, cnt = route_tile_meta(posm_all)
    xin = moe_dispatch(h, posm_all, aff3, s0, cnt)
    act = expert_ffn_up(xin, w_gate, w_up, layer)
    yh, yl = expert_ffn_down(act, xin, w_down, layer)
    return moe_combine_post_norm(yh, yl, posm_all, s0, cnt, x, gate, ln_g, ln_b, sh, sc)


def even_mixer(h, tabs, j, w_in, w_dw, b_dw, cn_g, cn_b, qa_g, kva_g, w_uq, w_ukv, w_o):
    u_main = matmul([h], w_in, F32, tn=768, layer=j, n_cols=IN_A_MAIN)
    w_kr = jnp.pad(w_in[j][:, IN_A_MAIN:], ((0, 0), (0, LANES - MLA_ROPE)))
    kr = matmul([h], w_kr, F32, tn=LANES)
    q_cat, k_cat, vt = mla_projections(u_main, kr, qa_g, kva_g, w_uq, w_ukv, tabs)
    att = mla_attention(q_cat, k_cat, vt)
    att_c = context_attention(q_cat, k_cat, vt, MLA_HEADS, MLA_QK_PAD, MLA_V, 0, 0, 0, 1.0, base2=True,
                              v_transposed=True)
    conv = conformer_conv(u_main, w_dw, b_dw, cn_g, cn_b)
    return matmul([conv, jnp.concatenate([att, att_c], axis=0)], w_o, F32, tn=1024, layer=j)


def odd_mixer(h, j, w_qkv, rpb, w_o):
    tn = 1024
    q_tiles = NA_HEADS * NA_HEAD_DIM // tn
    qkv = matmul([h], w_qkv, BF16, tn=tn, layer=j, lead_scale=(q_tiles, NA_HEAD_DIM ** -0.5 * LOG2_E))
    o = neighbourhood_attention(qkv, *na_bias_strips(rpb))
    o_c = context_attention(qkv, qkv, qkv, NA_HEADS, NA_HEAD_DIM, NA_HEAD_DIM, 0, NA_HEADS, 2 * NA_HEADS, 1.0,
                            base2=True)
    return matmul([jnp.concatenate([o, o_c], axis=0)], w_o, F32, tn=1024, layer=j)


def kernel(x, c, ctx, c_ctx, w_ada, b_ada, ln1_g, ln1_b, ln2_g, ln2_b, a_w_in, a_w_dw, a_b_dw, a_cn_g, a_cn_b,
           b_qa_g, b_kva_g, b_w_uq, b_w_ukv, ab_w_o, c_w_qkv, c_rpb, c_w_o, moe_w_router, moe_w_gate, moe_w_up,
           moe_w_down):
    assert x.shape == (1, SEQ, D_MODEL) and ctx.shape == (1, CTX_LEN, D_MODEL)
    xs = jnp.concatenate([x[0], ctx[0]], axis=0)
    cc = jnp.concatenate([c, c_ctx[None], jnp.zeros((SUBLANES - 2, D_MODEL), F32)], axis=0)
    mod = ada_modulation(cc, w_ada, b_ada)

    def vec(layer, k):
        return mod[layer, :2, k * D_MODEL:(k + 1) * D_MODEL].reshape(2, 1, D_MODEL)

    tabs = rope_tables()
    h = modulate_rows(xs, vec(0, 0), vec(0, 1))
    for layer in range(DEPTH):
        j = layer // 2
        if layer % 2 == 0:
            o = even_mixer(h, tabs, j, a_w_in, a_w_dw[j], a_b_dw[j], a_cn_g[j], a_cn_b[j], b_qa_g[j], b_kva_g[j],
                           b_w_uq[j], b_w_ukv[j], ab_w_o)
        else:
            o = odd_mixer(h, j, c_w_qkv, c_rpb[j], c_w_o)
        xs, h2, aff, aff3 = post_norm_rows(xs, o, vec(layer, 2), ln1_g[layer], ln1_b[layer], vec(layer, 3), vec(layer, 4),
                                     moe_w_router[layer])
        nxt = min(layer + 1, DEPTH - 1)
        xs, h = expert_choice_moe_post_norm(h2, aff, aff3, moe_w_gate, moe_w_up, moe_w_down, layer, xs, vec(layer, 5),
                                            ln2_g[layer], ln2_b[layer], vec(nxt, 0), vec(nxt, 1))
    return xs[:SEQ][None]
```

```python
import functools

import numpy as np
import jax
import jax.numpy as jnp
from jax import lax
from jax.experimental import pallas as pl
from jax.experimental.pallas import tpu as pltpu

F32 = jnp.float32
BF16 = jnp.bfloat16
I32 = jnp.int32

D_MODEL = 2048
SEQ = 8192
DEPTH = 4
GRID_W = 64
CTX_LEN = 256
M_ALL = SEQ + CTX_LEN

CONV_CH = 1024
CONV_WIDTH = 31
CONV_PAD = CONV_WIDTH // 2
MLA_HEADS = 8
MLA_Q_RANK = 512
MLA_KV_RANK = 512
MLA_NOPE = 128
MLA_ROPE = 64
MLA_V = 128
MLA_QK_PAD = 256
ROPE_BASE = 10000.0
NA_HEADS = 16
NA_HEAD_DIM = 128
NA_WIN_ROWS = 8
NA_WIN_COLS = 16
N_EXPERTS = 16
EXPERT_FF = 1408
EC_CAPACITY_FACTOR = 2
CAP_LAT = EC_CAPACITY_FACTOR * SEQ // N_EXPERTS
CAP_CTX = EC_CAPACITY_FACTOR * CTX_LEN // N_EXPERTS
CAP_ALL = CAP_LAT + CAP_CTX
IN_A = 2 * CONV_CH + MLA_Q_RANK + MLA_KV_RANK + MLA_ROPE
IN_A_MAIN = IN_A - MLA_ROPE
LOG2_E = 1.4426950408889634
DN_ALPHA = (2 * DEPTH) ** 0.25
LN_EPS = 1e-5
RMS_EPS = 1e-6
NEG_INF = -1e30

LANES = 128
SUBLANES = 8
VMEM_LIMIT = 56 * 1024 * 1024

ROW_TILE = 256
N_LAT_TILES = SEQ // ROW_TILE
N_ROW_TILES = M_ALL // ROW_TILE
MM_ROW_TILE = 768
ROUTE_TILE = 128
N_ROUTE_TILES = M_ALL // ROUTE_TILE
N_ROUTE_LAT = SEQ // ROUTE_TILE
Y_ROWS = CAP_ALL + 32


def _cparams(sem, vmem=VMEM_LIMIT):
    return pltpu.CompilerParams(dimension_semantics=sem, vmem_limit_bytes=vmem)


def _nt_dot(a, b):
    return lax.dot_general(a, b, (((1,), (1,)), ((), ())), preferred_element_type=F32)


def _tn_dot(a, b):
    return lax.dot_general(a, b, (((0,), (0,)), ((), ())), preferred_element_type=F32)


def _ada_kernel(c_ref, w_ref, b_ref, o_ref):
    c = c_ref[...]
    a = (c * jax.nn.sigmoid(c)).astype(BF16)
    o_ref[0] = jnp.dot(a, w_ref[0].astype(BF16), preferred_element_type=F32) + b_ref[0]


def ada_modulation(cc, w_ada, b_ada):
    depth, d, n = w_ada.shape
    tn = 1024
    return pl.pallas_call(
        _ada_kernel,
        out_shape=jax.ShapeDtypeStruct((depth, SUBLANES, n), F32),
        grid=(depth, n // tn),
        in_specs=[
            pl.BlockSpec((SUBLANES, d), lambda l, j: (0, 0)),
            pl.BlockSpec((1, d, tn), lambda l, j: (l, 0, j)),
            pl.BlockSpec((1, 1, tn), lambda l, j: (l, 0, j)),
        ],
        out_specs=pl.BlockSpec((1, SUBLANES, tn), lambda l, j: (l, 0, j)),
        compiler_params=_cparams(("arbitrary", "arbitrary")),
        name="ada_modulation",
    )(cc, w_ada, b_ada.reshape(depth, 1, n))


def _stream_vec_spec():
    return pl.BlockSpec((1, 1, D_MODEL), lambda i: (jnp.minimum(i // N_LAT_TILES, 1), 0, 0))


def _row_spec(width=D_MODEL):
    return pl.BlockSpec((ROW_TILE, width), lambda i: (i, 0))


def _modulate_kernel(x_ref, sh_ref, sc_ref, h_ref):
    h_ref[...] = (x_ref[...] * (1.0 + sc_ref[0]) + sh_ref[0]).astype(h_ref.dtype)


def modulate_rows(x, sh, sc):
    return pl.pallas_call(
        _modulate_kernel,
        out_shape=jax.ShapeDtypeStruct(x.shape, BF16),
        grid=(N_ROW_TILES,),
        in_specs=[_row_spec(), _stream_vec_spec(), _stream_vec_spec()],
        out_specs=_row_spec(),
        compiler_params=_cparams(("arbitrary",)),
        name="modulate_rows",
    )(x, sh, sc)


def _post_norm_kernel(*refs, with_router):
    if with_router:
        x_ref, y_ref, gate_ref, lg_ref, lb_ref, sh_ref, sc_ref, wr_ref, xo_ref, h_ref, aff_ref, aff3_ref = refs
    else:
        x_ref, y_ref, gate_ref, lg_ref, lb_ref, sh_ref, sc_ref, xo_ref, h_ref = refs
    z = DN_ALPHA * x_ref[...] + gate_ref[0] * y_ref[...]
    mu = jnp.mean(z, axis=-1, keepdims=True)
    zc = z - mu
    var = jnp.mean(zc * zc, axis=-1, keepdims=True)
    xn = zc * lax.rsqrt(var + LN_EPS) * lg_ref[...] + lb_ref[...]
    xo_ref[...] = xn
    h = (xn * (1.0 + sc_ref[0]) + sh_ref[0]).astype(BF16)
    h_ref[...] = h
    if with_router:
        logits = jnp.dot(h, wr_ref[...].astype(BF16), preferred_element_type=F32)
        lane = lax.broadcasted_iota(I32, logits.shape, 1)
        logits = jnp.where(lane < N_EXPERTS, logits, NEG_INF)
        e = jnp.exp(logits - jnp.max(logits, axis=-1, keepdims=True))
        aff = e / jnp.sum(e, axis=-1, keepdims=True)
        aff_ref[...] = aff
        hi = aff.astype(BF16).astype(F32)
        mid = (aff - hi).astype(BF16).astype(F32)
        lo = (aff - hi - mid).astype(BF16).astype(F32)
        aff3_ref[...] = (hi + pltpu.roll(mid, N_EXPERTS, 1) + pltpu.roll(lo, 2 * N_EXPERTS, 1)).astype(BF16)


def post_norm_rows(x, y, gate, ln_g, ln_b, sh, sc, w_router=None):
    with_router = w_router is not None
    vec = pl.BlockSpec((1, D_MODEL), lambda i: (0, 0))
    in_specs = [_row_spec(), _row_spec(), _stream_vec_spec(), vec, vec, _stream_vec_spec(), _stream_vec_spec()]
    out_shape = [jax.ShapeDtypeStruct(x.shape, F32), jax.ShapeDtypeStruct(x.shape, BF16)]
    out_specs = [_row_spec(), _row_spec()]
    args = [x, y, gate, ln_g.reshape(1, -1), ln_b.reshape(1, -1), sh, sc]
    if with_router:
        in_specs.append(pl.BlockSpec((D_MODEL, LANES), lambda i: (0, 0)))
        out_shape += [jax.ShapeDtypeStruct((x.shape[0], LANES), F32), jax.ShapeDtypeStruct((x.shape[0], LANES), BF16)]
        out_specs += [_row_spec(LANES), _row_spec(LANES)]
        args.append(jnp.pad(w_router, ((0, 0), (0, LANES - N_EXPERTS))))
    return pl.pallas_call(
        functools.partial(_post_norm_kernel, with_router=with_router),
        out_shape=out_shape,
        grid=(N_ROW_TILES,),
        in_specs=in_specs,
        out_specs=out_specs,
        compiler_params=_cparams(("arbitrary",)),
        name="post_norm_router" if with_router else "post_norm",
    )(*args)


def _matmul_kernel(*refs, splits, paired, lead_scale):
    n_refs = len(splits) + sum(paired)
    a_refs, w_ref, o_ref, wb_ref = list(refs[:n_refs]), refs[n_refs], refs[n_refs + 1], refs[n_refs + 2]

    @pl.when(pl.program_id(1) == 0)
    def _():
        wb_ref[...] = w_ref[...].astype(BF16)

    acc = None
    off = 0
    for k, pair in zip(splits, paired):
        a = a_refs.pop(0)[...]
        if pair:
            a = jnp.where(pl.program_id(1) < N_LAT_TILES, a, a_refs.pop(0)[...])
        part = jnp.dot(a.astype(BF16), wb_ref[off:off + k, :], preferred_element_type=F32)
        acc = part if acc is None else acc + part
        off += k
    if lead_scale is not None:
        n_tiles, value = lead_scale
        acc = acc * jnp.where(pl.program_id(0) < n_tiles, value, 1.0)
    o_ref[...] = acc.astype(o_ref.dtype)


def matmul(a_list, w, out_dtype, tn, tm=MM_ROW_TILE, layer=None, n_cols=None, lead_scale=None):
    paired = tuple(isinstance(a, tuple) for a in a_list)
    splits = tuple(a[0].shape[1] if p else a.shape[1] for a, p in zip(a_list, paired))
    m = M_ALL if any(paired) else a_list[0].shape[0]
    k, n = w.shape[-2:]
    n = n if n_cols is None else n_cols
    assert sum(splits) == k and n % tn == 0 and m % tm == 0 and (tm == ROW_TILE or not any(paired))
    in_specs, args = [], []
    for a, ki, p in zip(a_list, splits, paired):
        if p:
            in_specs += [pl.BlockSpec((tm, ki), lambda j, i: (jnp.minimum(i, N_LAT_TILES - 1), 0)),
                         pl.BlockSpec((tm, ki), lambda j, i: (0, 0))]
            args += list(a)
        else:
            in_specs.append(pl.BlockSpec((tm, ki), lambda j, i: (i, 0)))
            args.append(a)
    if layer is None:
        in_specs.append(pl.BlockSpec((k, tn), lambda j, i: (0, j)))
    else:
        in_specs.append(pl.BlockSpec((None, k, tn), lambda j, i: (layer, 0, j)))
    return pl.pallas_call(
        functools.partial(_matmul_kernel, splits=splits, paired=paired, lead_scale=lead_scale),
        out_shape=jax.ShapeDtypeStruct((m, n), out_dtype),
        grid=(n // tn, m // tm),
        in_specs=in_specs,
        out_specs=pl.BlockSpec((tm, tn), lambda j, i: (i, j)),
        scratch_shapes=[pltpu.VMEM((k, tn), BF16)],
        compiler_params=_cparams(("arbitrary", "arbitrary")),
        name="matmul",
    )(*args, w)


def rope_tables():
    t = jnp.arange(SEQ, dtype=I32)
    row = (t // GRID_W).astype(F32)
    col = (t % GRID_W).astype(F32)
    n_freq = MLA_ROPE // 4
    inv_freq = ROPE_BASE ** (-jnp.arange(n_freq, dtype=F32) / n_freq)
    ang = jnp.concatenate([row[:, None] * inv_freq, col[:, None] * inv_freq], axis=-1)
    cos, sin = jnp.cos(ang), jnp.sin(ang)
    half = MLA_ROPE // 2
    cos = jnp.concatenate([cos, jnp.ones((CTX_LEN, half), F32)], axis=0)
    sin = jnp.concatenate([sin, jnp.zeros((CTX_LEN, half), F32)], axis=0)
    z = jnp.zeros_like(cos)
    c_tab = jnp.concatenate([cos, cos, z, z], axis=1)
    sa_tab = jnp.concatenate([z, sin, z, z], axis=1)
    sb_tab = jnp.concatenate([-sin, z, z, z], axis=1)
    return c_tab, sa_tab, sb_tab


def _mla_proj_kernel(cq_ref, ckv_ref, kr_ref, qg_ref, kvg_ref, wq_ref, wkv_ref, c_ref, sa_ref, sb_ref,
                     q_out, k_out, vt_out, wqb, wkvb):
    @pl.when(pl.program_id(0) == 0)
    def _():
        wqb[...] = wq_ref[...].astype(BF16)
        wkvb[...] = wkv_ref[...].astype(BF16)

    def rms(x, g):
        return x * lax.rsqrt(jnp.mean(x * x, axis=-1, keepdims=True) + RMS_EPS) * g

    c_tab, sa_tab, sb_tab = c_ref[...], sa_ref[...], sb_ref[...]
    half = MLA_ROPE // 2

    def rope(g):
        return g * c_tab + pltpu.roll(g, half, 1) * sa_tab + pltpu.roll(g, LANES - half, 1) * sb_tab

    qk_scale = (MLA_NOPE + MLA_ROPE) ** -0.5 * LOG2_E
    q = jnp.dot(rms(cq_ref[...], qg_ref[...]).astype(BF16), wqb[...], preferred_element_type=F32) * qk_scale
    kv = jnp.dot(rms(ckv_ref[...], kvg_ref[...]).astype(BF16), wkvb[...], preferred_element_type=F32)
    k_rope = rope(kr_ref[...]).astype(BF16)
    for h in range(MLA_HEADS):
        lo = h * MLA_QK_PAD
        q_out[:, lo:lo + LANES] = q[:, lo:lo + LANES].astype(BF16)
        q_out[:, lo + LANES:lo + 2 * LANES] = rope(q[:, lo + LANES:lo + 2 * LANES]).astype(BF16)
        k_out[:, lo:lo + LANES] = kv[:, h * MLA_NOPE:(h + 1) * MLA_NOPE].astype(BF16)
        k_out[:, lo + LANES:lo + 2 * LANES] = k_rope
    vt_out[...] = kv[:, MLA_HEADS * MLA_NOPE:].T.astype(BF16)


def mla_projections(u_main, kr, qa_g, kva_g, w_uq, w_ukv, tabs):
    m = u_main.shape[0]
    tm = 384
    hq = MLA_HEADS * MLA_QK_PAD
    wq = w_uq.reshape(MLA_Q_RANK, MLA_HEADS, MLA_NOPE + MLA_ROPE)
    wq = jnp.pad(wq, ((0, 0), (0, 0), (0, MLA_QK_PAD - MLA_NOPE - MLA_ROPE))).reshape(MLA_Q_RANK, hq)
    wkv = w_ukv.reshape(MLA_KV_RANK, MLA_HEADS, MLA_NOPE + MLA_V)
    wkv = jnp.concatenate([wkv[..., :MLA_NOPE].reshape(MLA_KV_RANK, -1), wkv[..., MLA_NOPE:].reshape(MLA_KV_RANK, -1)], axis=1)
    nkv = wkv.shape[1]
    cq_blk = 2 * CONV_CH // MLA_Q_RANK
    tab_spec = pl.BlockSpec((tm, LANES), lambda i: (i, 0))
    return pl.pallas_call(
        _mla_proj_kernel,
        out_shape=[jax.ShapeDtypeStruct((m, hq), BF16), jax.ShapeDtypeStruct((m, hq), BF16),
                   jax.ShapeDtypeStruct((MLA_HEADS * MLA_V, m), BF16)],
        grid=(m // tm,),
        in_specs=[
            pl.BlockSpec((tm, MLA_Q_RANK), lambda i: (i, cq_blk)),
            pl.BlockSpec((tm, MLA_KV_RANK), lambda i: (i, cq_blk + 1)),
            pl.BlockSpec((tm, LANES), lambda i: (i, 0)),
            pl.BlockSpec((1, MLA_Q_RANK), lambda i: (0, 0)),
            pl.BlockSpec((1, MLA_KV_RANK), lambda i: (0, 0)),
            pl.BlockSpec((MLA_Q_RANK, hq), lambda i: (0, 0)),
            pl.BlockSpec((MLA_KV_RANK, nkv), lambda i: (0, 0)),
            tab_spec, tab_spec, tab_spec,
        ],
        out_specs=[pl.BlockSpec((tm, hq), lambda i: (i, 0)), pl.BlockSpec((tm, hq), lambda i: (i, 0)),
                   pl.BlockSpec((MLA_HEADS * MLA_V, tm), lambda i: (0, i))],
        scratch_shapes=[pltpu.VMEM((MLA_Q_RANK, hq), BF16), pltpu.VMEM((MLA_KV_RANK, nkv), BF16)],
        compiler_params=_cparams(("arbitrary",)),
        name="mla_projections",
    )(u_main, u_main, kr, qa_g.reshape(1, -1), kva_g.reshape(1, -1), wq, wkv, *tabs)


MLA_KEY_CHUNK = 1408
MLA_Q_TILE = 512


def _flash_kernel(q_ref, k_ref, vt_ref, o_ref, *, tk, n_chunks):
    q = q_ref[...]
    tq = q.shape[0]
    m = jnp.full((1, tq), NEG_INF, F32)
    l = jnp.zeros((1, tq), F32)
    acc = jnp.zeros((vt_ref.shape[0], tq), F32)
    s_next = _nt_dot(k_ref[0:tk, :], q)
    for c in range(n_chunks):
        s = s_next
        if c + 1 < n_chunks:
            s_next = _nt_dot(k_ref[(c + 1) * tk:(c + 2) * tk, :], q)
        m_new = jnp.maximum(m, jnp.max(s, axis=0, keepdims=True))
        alpha = jnp.exp2(m - m_new)
        p = jnp.exp2(s - m_new)
        l = alpha * l + jnp.sum(p, axis=0, keepdims=True)
        acc = alpha * acc + jnp.dot(vt_ref[:, c * tk:(c + 1) * tk], p.astype(BF16), preferred_element_type=F32)
        m = m_new
    o_ref[...] = (acc / l).T.astype(o_ref.dtype)


def mla_attention(q_cat, k_cat, vt):
    m = k_cat.shape[0]
    tq, tk = MLA_Q_TILE, MLA_KEY_CHUNK
    return pl.pallas_call(
        functools.partial(_flash_kernel, tk=tk, n_chunks=m // tk),
        out_shape=jax.ShapeDtypeStruct((SEQ, MLA_HEADS * MLA_V), BF16),
        grid=(MLA_HEADS, SEQ // tq),
        in_specs=[
            pl.BlockSpec((tq, MLA_QK_PAD), lambda h, i: (i, h)),
            pl.BlockSpec((m, MLA_QK_PAD), lambda h, i: (0, h)),
            pl.BlockSpec((MLA_V, m), lambda h, i: (h, 0)),
        ],
        out_specs=pl.BlockSpec((tq, MLA_V), lambda h, i: (i, h)),
        compiler_params=_cparams(("arbitrary", "arbitrary")),
        name="mla_attention",
    )(q_cat, k_cat, vt)


def _ctx_attn_kernel(q_ref, k_ref, v_ref, o_ref, *, scale, base2, v_transposed):
    s = _nt_dot(q_ref[...], k_ref[...]) * scale
    z = s - jnp.max(s, axis=-1, keepdims=True)
    p = jnp.exp2(z) if base2 else jnp.exp(z)
    l = jnp.sum(p, axis=-1, keepdims=True)
    pv = _nt_dot(p.astype(BF16), v_ref[...]) if v_transposed else jnp.dot(p.astype(BF16), v_ref[...],
                                                                         preferred_element_type=F32)
    o_ref[...] = (pv / l).astype(o_ref.dtype)


def context_attention(q_arr, k_arr, v_arr, heads, dq, dv, q_col, k_col, v_col, scale, base2=False,
                      v_transposed=False):
    rb = SEQ // CTX_LEN
    if v_transposed:
        v_spec = pl.BlockSpec((dv, CTX_LEN), lambda h: (v_col + h, rb))
    else:
        v_spec = pl.BlockSpec((CTX_LEN, dv), lambda h: (rb, v_col + h))
    return pl.pallas_call(
        functools.partial(_ctx_attn_kernel, scale=scale, base2=base2, v_transposed=v_transposed),
        out_shape=jax.ShapeDtypeStruct((CTX_LEN, heads * dv), BF16),
        grid=(heads,),
        in_specs=[
            pl.BlockSpec((CTX_LEN, dq), lambda h: (rb, q_col + h)),
            pl.BlockSpec((CTX_LEN, dq), lambda h: (rb, k_col + h)),
            v_spec,
        ],
        out_specs=pl.BlockSpec((CTX_LEN, dv), lambda h: (0, h)),
        compiler_params=_cparams(("arbitrary",)),
        name="context_attention",
    )(q_arr, k_arr, v_arr)


CONV_HALO = 16
CONV_ROW_BLOCK = 64


def _conv_kernel(a_ref, g_ref, ap_ref, gp_ref, an_ref, gn_ref, w_ref, b_ref, cg_ref, cb_ref, o_ref, hbuf, cbuf):
    i = pl.program_id(0)
    tl = a_ref.shape[0]

    def glu(a, g):
        return a[...] * jax.nn.sigmoid(g[...])

    has_prev = jnp.logical_and(i != 0, i != N_LAT_TILES)
    has_next = jnp.logical_and(i != N_LAT_TILES - 1, i != N_ROW_TILES - 1)
    hbuf[0:CONV_HALO, :] = jnp.where(has_prev, glu(ap_ref, gp_ref), 0.0)
    hbuf[CONV_HALO:CONV_HALO + tl, :] = glu(a_ref, g_ref)
    hbuf[CONV_HALO + tl:, :] = jnp.where(has_next, glu(an_ref, gn_ref), 0.0)

    base = CONV_HALO - CONV_PAD
    for rb in range(tl // CONV_ROW_BLOCK):
        r0 = rb * CONV_ROW_BLOCK
        for c in range(CONV_CH // LANES):
            cs = slice(c * LANES, (c + 1) * LANES)
            acc = jnp.broadcast_to(b_ref[:, cs], (CONV_ROW_BLOCK, LANES))
            for j in range(CONV_WIDTH):
                acc = acc + hbuf[r0 + base + j:r0 + base + j + CONV_ROW_BLOCK, cs] * w_ref[j:j + 1, cs]
            cbuf[r0:r0 + CONV_ROW_BLOCK, cs] = acc

    y = cbuf[...]
    mu = jnp.mean(y, axis=-1, keepdims=True)
    yc = y - mu
    var = jnp.mean(yc * yc, axis=-1, keepdims=True)
    yn = yc * lax.rsqrt(var + LN_EPS) * cg_ref[...] + cb_ref[...]
    o_ref[...] = (yn * jax.nn.sigmoid(yn)).astype(o_ref.dtype)


def conformer_conv(u_main, w_dw, b_dw, cn_g, cn_b):
    m = u_main.shape[0]
    tl = ROW_TILE
    hpt = tl // CONV_HALO
    n_halo = m // CONV_HALO
    main = lambda col: pl.BlockSpec((tl, CONV_CH), lambda i: (i, col))
    prev = lambda col: pl.BlockSpec((CONV_HALO, CONV_CH), lambda i: (jnp.maximum(i * hpt - 1, 0), col))
    nxt = lambda col: pl.BlockSpec((CONV_HALO, CONV_CH), lambda i: (jnp.minimum((i + 1) * hpt, n_halo - 1), col))
    vec = pl.BlockSpec((1, CONV_CH), lambda i: (0, 0))
    return pl.pallas_call(
        _conv_kernel,
        out_shape=jax.ShapeDtypeStruct((m, CONV_CH), BF16),
        grid=(m // tl,),
        in_specs=[main(0), main(1), prev(0), prev(1), nxt(0), nxt(1),
                  pl.BlockSpec((CONV_WIDTH, CONV_CH), lambda i: (0, 0)), vec, vec, vec],
        out_specs=pl.BlockSpec((tl, CONV_CH), lambda i: (i, 0)),
        scratch_shapes=[pltpu.VMEM((tl + 2 * CONV_HALO, CONV_CH), F32), pltpu.VMEM((tl, CONV_CH), F32)],
        compiler_params=_cparams(("arbitrary",)),
        name="conformer_conv",
    )(u_main, u_main, u_main, u_main, u_main, u_main, w_dw, b_dw.reshape(1, -1), cn_g.reshape(1, -1), cn_b.reshape(1, -1))


NA_Q_ROWS = 8
NA_K_ROWS = 16
NA_KBLK = 4 * GRID_W
NA_QTOK = NA_Q_ROWS * GRID_W
NA_KTOK = NA_K_ROWS * GRID_W
NA_GROUPS = SEQ // NA_QTOK


NA_HEADS_PER_STEP = 2
NA_CLASSES = (0, 1, NA_GROUPS - 1)


def _na_window_rows():
    rows = SEQ // GRID_W
    out = []
    for g in NA_CLASSES:
        qr = (NA_Q_ROWS * g + np.arange(NA_Q_ROWS))[:, None]
        kr = (NA_Q_ROWS * g - (NA_K_ROWS - NA_Q_ROWS) // 2 + np.arange(NA_K_ROWS))[None, :]
        start = np.clip(qr - NA_WIN_ROWS // 2, 0, rows - NA_WIN_ROWS)
        out.append((kr >= start) & (kr < start + NA_WIN_ROWS))
    return np.stack(out)


def na_bias_strips(rpb):
    qc = np.arange(GRID_W)[:, None]
    kc = np.arange(GRID_W)[None, :]
    ws = np.clip(qc - NA_WIN_COLS // 2, 0, GRID_W - NA_WIN_COLS)
    col_valid = (kc >= ws) & (kc < ws + NA_WIN_COLS)
    col_idx = np.clip(kc - qc + NA_WIN_COLS - 1, 0, 2 * NA_WIN_COLS - 2)
    onehot = jnp.asarray((col_idx[None] == np.arange(2 * NA_WIN_COLS - 1)[:, None, None]).astype(np.float32))
    cm = jnp.einsum('hrd,dqk->hrqk', rpb, onehot, precision=lax.Precision.HIGHEST) * LOG2_E
    cm = jnp.where(col_valid[None, None], cm, NEG_INF)
    neg = jnp.full_like(cm, NEG_INF)
    return jnp.concatenate([cm, neg], axis=-1), jnp.concatenate([neg, cm], axis=-1)


def _na_kernel(q_ref, k0, k1, k2, k3, v0, v1, v2, v3, kc_ref, vc_ref, bl_ref, br_ref, o_ref, tbl_ref):
    g = pl.program_id(1)
    row_valid = _na_window_rows()

    @pl.when(g == 0)
    def _():
        neg_tile = jnp.full((GRID_W, LANES), NEG_INF, F32)
        for hh in range(NA_HEADS_PER_STEP):
            for c in range(len(NA_CLASSES)):
                for ql in range(NA_Q_ROWS):
                    for kp in range(NA_K_ROWS // 2):
                        ka, kb = 2 * kp, 2 * kp + 1
                        da = ka - ql + NA_WIN_ROWS - 1 - (NA_K_ROWS - NA_Q_ROWS) // 2
                        ta = bl_ref[hh, da] if row_valid[c, ql, ka] else neg_tile
                        tb = br_ref[hh, da + 1] if row_valid[c, ql, kb] else neg_tile
                        tbl_ref[hh, c, ql * GRID_W:(ql + 1) * GRID_W, kp * LANES:(kp + 1) * LANES] = jnp.maximum(ta, tb)

    cls = jnp.where(g == 0, 0, jnp.where(g == NA_GROUPS - 1, 2, 1))
    for hh in range(NA_HEADS_PER_STEP):
        hs = slice(hh * NA_HEAD_DIM, (hh + 1) * NA_HEAD_DIM)
        q = q_ref[:, hs]
        s = jnp.concatenate([_nt_dot(q, k[:, hs]) for k in (k0, k1, k2, k3)], axis=1) + tbl_ref[hh, cls]
        sc = _nt_dot(q, kc_ref[:, hs])
        m = jnp.maximum(jnp.max(s, axis=-1, keepdims=True), jnp.max(sc, axis=-1, keepdims=True))
        p = jnp.exp2(s - m)
        pc = jnp.exp2(sc - m)
        l = jnp.sum(p, axis=-1, keepdims=True) + jnp.sum(pc, axis=-1, keepdims=True)
        o = jnp.dot(pc.astype(BF16), vc_ref[:, hs], preferred_element_type=F32)
        for j, v in enumerate((v0, v1, v2, v3)):
            o = o + jnp.dot(p[:, j * NA_KBLK:(j + 1) * NA_KBLK].astype(BF16), v[:, hs], preferred_element_type=F32)
        o_ref[:, hs] = (o / l).astype(o_ref.dtype)


def neighbourhood_attention(qkv, bias_left, bias_right):
    n_kblk = SEQ // NA_KBLK
    ctx_blk = SEQ // CTX_LEN
    hp = NA_HEADS_PER_STEP
    width = hp * NA_HEAD_DIM
    n_pairs = NA_HEADS // hp

    def kv_spec(j, col0):
        return pl.BlockSpec((NA_KBLK, width), lambda h, g: (jnp.clip(2 * g - 1 + j, 0, n_kblk - 1), col0 + h))

    strip_spec = pl.BlockSpec((hp, 2 * NA_WIN_ROWS - 1, GRID_W, LANES), lambda h, g: (h, 0, 0, 0))
    return pl.pallas_call(
        _na_kernel,
        out_shape=jax.ShapeDtypeStruct((SEQ, NA_HEADS * NA_HEAD_DIM), BF16),
        grid=(n_pairs, NA_GROUPS),
        in_specs=[pl.BlockSpec((NA_QTOK, width), lambda h, g: (g, h))]
        + [kv_spec(j, n_pairs) for j in range(4)]
        + [kv_spec(j, 2 * n_pairs) for j in range(4)]
        + [pl.BlockSpec((CTX_LEN, width), lambda h, g: (ctx_blk, n_pairs + h)),
           pl.BlockSpec((CTX_LEN, width), lambda h, g: (ctx_blk, 2 * n_pairs + h)),
           strip_spec, strip_spec],
        out_specs=pl.BlockSpec((NA_QTOK, width), lambda h, g: (g, h)),
        scratch_shapes=[pltpu.VMEM((hp, len(NA_CLASSES), NA_QTOK, NA_KTOK), F32)],
        compiler_params=_cparams(("arbitrary", "arbitrary")),
        name="neighbourhood_attention",
    )(*([qkv] * 11), bias_left, bias_right)


def _select_kernel(aff_ref, posm_ref, *, cap):
    bits = pltpu.bitcast(aff_ref[...], I32)
    n = bits.shape[1]

    def search(i, thr):
        cand = thr | (jnp.int32(1) << (30 - i))
        cnt = jnp.sum((bits >= cand).astype(I32), axis=1, keepdims=True)
        return jnp.where(cnt >= cap, cand, thr)

    thr = lax.fori_loop(0, 31, search, jnp.zeros((N_EXPERTS, 1), I32))
    gt = bits > thr
    eq = bits == thr
    need = (cap - jnp.sum(gt.astype(I32), axis=1, keepdims=True)).astype(F32)
    tri = jnp.where(lax.broadcasted_iota(I32, (LANES, LANES), 0) <= lax.broadcasted_iota(I32, (LANES, LANES), 1),
                    1.0, 0.0).astype(BF16)
    off_eq = jnp.zeros((N_EXPERTS, 1), F32)
    off_sel = jnp.zeros((N_EXPERTS, 1), F32)
    for b in range(n // LANES):
        cs = slice(b * LANES, (b + 1) * LANES)
        eq_b = jnp.where(eq[:, cs], 1.0, 0.0)
        inc_eq = jnp.dot(eq_b.astype(BF16), tri, preferred_element_type=F32)
        rank = off_eq + inc_eq - eq_b
        sel_b = jnp.logical_or(gt[:, cs], jnp.logical_and(eq[:, cs], rank < need))
        sel_f = jnp.where(sel_b, 1.0, 0.0)
        inc_sel = jnp.dot(sel_f.astype(BF16), tri, preferred_element_type=F32)
        pos = off_sel + inc_sel - sel_f
        posm_ref[:, cs] = jnp.where(sel_b, pos.astype(I32), -1)
        off_eq = off_eq + inc_eq[:, LANES - 1:LANES]
        off_sel = off_sel + inc_sel[:, LANES - 1:LANES]


def expert_choice_select(aff_t, cap):
    return pl.pallas_call(
        functools.partial(_select_kernel, cap=cap),
        out_shape=jax.ShapeDtypeStruct(aff_t.shape, I32),
        compiler_params=_cparams(None),
        name="expert_choice_select",
    )(aff_t)


def route_tile_meta(posm_all):
    cnt = (posm_all >= 0).astype(I32).reshape(N_EXPERTS, N_ROUTE_TILES, ROUTE_TILE).sum(-1).T
    lat, ctx = cnt[:N_ROUTE_LAT], cnt[N_ROUTE_LAT:]
    s0 = jnp.concatenate([jnp.cumsum(lat, 0) - lat, CAP_LAT + jnp.cumsum(ctx, 0) - ctx], axis=0)
    return s0.reshape(-1), cnt.reshape(-1)


DISPATCH_PIECE = 16
DISPATCH_BLOCK = 256
DISPATCH_ROWS = N_EXPERTS * ((ROUTE_TILE + SUBLANES - 1 + DISPATCH_PIECE - 1) // DISPATCH_PIECE) * DISPATCH_PIECE


XIN_WIDTH = D_MODEL + LANES


def _dispatch_kernel(s0_ref, cnt_ref, h_ref, posm_ref, aff3_ref, xin_ref, pt_ref, x_ref, carry_ref, sem):
    t = pl.program_id(0)

    @pl.when(t == 0)
    def _():
        carry_ref[...] = jnp.zeros(carry_ref.shape, F32)

    pt_ref[...] = jnp.zeros(pt_ref.shape, pt_ref.dtype)
    iota = lax.broadcasted_iota(I32, (DISPATCH_PIECE, ROUTE_TILE), 0)
    offs = []
    o = jnp.int32(0)
    for e in range(N_EXPERTS):
        n = cnt_ref[t * N_EXPERTS + e]
        s0 = s0_ref[t * N_EXPERTS + e]
        a0 = (s0 // SUBLANES) * SUBLANES
        npc = jnp.where(n > 0, (s0 + n - a0 + DISPATCH_PIECE - 1) // DISPATCH_PIECE, 0)
        row = posm_ref[e:e + 1, :]
        offs.append(o)

        def piece(p, carry, row=row, a0=a0, o=o):
            r0 = pl.multiple_of(o + p * DISPATCH_PIECE, DISPATCH_PIECE)
            hit = row == (a0 + p * DISPATCH_PIECE + iota)
            pt_ref[pl.ds(r0, DISPATCH_PIECE), :] = jnp.where(hit, 1.0, 0.0).astype(BF16)
            return carry

        lax.fori_loop(0, npc, piece, 0)
        o = o + npc * DISPATCH_PIECE

    def block(b, carry):
        r0 = pl.multiple_of(b * DISPATCH_BLOCK, DISPATCH_BLOCK)
        onehot = pt_ref[pl.ds(r0, DISPATCH_BLOCK), :]
        x_ref[pl.ds(r0, DISPATCH_BLOCK), 0:D_MODEL] = jnp.dot(onehot, h_ref[...], preferred_element_type=F32)
        x_ref[pl.ds(r0, DISPATCH_BLOCK), D_MODEL:] = jnp.dot(onehot, aff3_ref[...], preferred_element_type=F32)
        return carry

    lax.fori_loop(0, (o + DISPATCH_BLOCK - 1) // DISPATCH_BLOCK, block, 0)

    def tile_copy(src_row, e, dst_row):
        return pltpu.make_async_copy(x_ref.at[pl.ds(src_row, SUBLANES)], xin_ref.at[e, pl.ds(dst_row, SUBLANES)], sem)

    total = jnp.int32(0)
    for e in range(N_EXPERTS):
        n = cnt_ref[t * N_EXPERTS + e]
        s0 = s0_ref[t * N_EXPERTS + e]
        a0 = (s0 // SUBLANES) * SUBLANES
        span = s0 + n - a0
        full = jnp.where(n > 0, span // SUBLANES, 0)
        o_e = offs[e]

        @pl.when(n > 0)
        def _(e=e, a0=a0, span=span, full=full, o_e=o_e):
            head = pl.multiple_of(o_e, SUBLANES)
            x_ref[pl.ds(head, SUBLANES), :] += carry_ref[e]

            def issue(g, carry):
                tile_copy(pl.multiple_of(o_e + g * SUBLANES, SUBLANES), e,
                          pl.multiple_of(a0 + g * SUBLANES, SUBLANES)).start()
                return carry

            lax.fori_loop(0, full, issue, 0)
            tail = x_ref[pl.ds(pl.multiple_of(o_e + full * SUBLANES, SUBLANES), SUBLANES), :]
            carry_ref[e] = jnp.where(span - full * SUBLANES > 0, tail, 0.0)

        total = total + full

    def drain(g, carry):
        tile_copy(0, 0, 0).wait()
        return carry

    lax.fori_loop(0, total, drain, 0)


def moe_dispatch(h, posm_all, aff3, s0, cnt):
    return pl.pallas_call(
        _dispatch_kernel,
        out_shape=jax.ShapeDtypeStruct((N_EXPERTS, CAP_ALL, XIN_WIDTH), F32),
        grid_spec=pltpu.PrefetchScalarGridSpec(
            num_scalar_prefetch=2,
            grid=(N_ROUTE_TILES,),
            in_specs=[pl.BlockSpec((ROUTE_TILE, D_MODEL), lambda t, s0, cnt: (t, 0)),
                      pl.BlockSpec((N_EXPERTS, ROUTE_TILE), lambda t, s0, cnt: (0, t)),
                      pl.BlockSpec((ROUTE_TILE, LANES), lambda t, s0, cnt: (t, 0))],
            out_specs=pl.BlockSpec(memory_space=pl.ANY),
            scratch_shapes=[pltpu.VMEM((DISPATCH_ROWS, ROUTE_TILE), BF16),
                            pltpu.VMEM((DISPATCH_ROWS + SUBLANES, XIN_WIDTH), F32),
                            pltpu.VMEM((N_EXPERTS, SUBLANES, XIN_WIDTH), F32),
                            pltpu.SemaphoreType.DMA(())]),
        compiler_params=_cparams(("arbitrary",)),
        name="moe_dispatch",
    )(s0, cnt, h, posm_all, aff3)


def _ffn_up_kernel(x_ref, wg_ref, wu_ref, o_ref, ag_ref, au_ref):
    k = pl.program_id(1)
    x = x_ref[0].astype(BF16)
    pg = jnp.dot(x, wg_ref[0].astype(BF16), preferred_element_type=F32)
    pu = jnp.dot(x, wu_ref[0].astype(BF16), preferred_element_type=F32)

    @pl.when(k == 0)
    def _():
        ag_ref[...] = pg
        au_ref[...] = pu

    @pl.when(k > 0)
    def _():
        ag_ref[...] += pg
        au_ref[...] += pu

    @pl.when(k == pl.num_programs(1) - 1)
    def _():
        a = ag_ref[...]
        o_ref[0] = (a * jax.nn.sigmoid(a) * au_ref[...]).astype(o_ref.dtype)


def expert_ffn_up(xin, w_gate, w_up, layer):
    e, cap, _ = xin.shape
    d, ff = w_gate.shape[2:]
    tk = 1024
    return pl.pallas_call(
        _ffn_up_kernel,
        out_shape=jax.ShapeDtypeStruct((e, cap, ff), BF16),
        grid=(e, d // tk),
        in_specs=[pl.BlockSpec((1, cap, tk), lambda i, k: (i, 0, k)),
                  pl.BlockSpec((None, 1, tk, ff), lambda i, k: (layer, i, k, 0)),
                  pl.BlockSpec((None, 1, tk, ff), lambda i, k: (layer, i, k, 0))],
        out_specs=pl.BlockSpec((1, cap, ff), lambda i, k: (i, 0, 0)),
        scratch_shapes=[pltpu.VMEM((cap, ff), F32), pltpu.VMEM((cap, ff), F32)],
        compiler_params=_cparams(("arbitrary", "arbitrary"), vmem=62 * 1024 * 1024),
        name="expert_ffn_up",
    )(xin, w_gate, w_up)


def _ffn_down_kernel(a_ref, w_ref, g_ref, hi_ref, lo_ref):
    cap = a_ref.shape[1]
    g = g_ref[0]
    lane = lax.broadcasted_iota(I32, g.shape, 1)
    mine = jnp.logical_and(lane % N_EXPERTS == pl.program_id(0), lane < 3 * N_EXPERTS)
    gate = jnp.sum(jnp.where(mine, g, 0.0), axis=1, keepdims=True)
    y = jnp.dot(a_ref[0], w_ref[0].astype(BF16), preferred_element_type=F32) * gate
    hi = y.astype(BF16)
    hi_ref[0, 0:cap, :] = hi
    lo_ref[0, 0:cap, :] = (y - hi.astype(F32)).astype(BF16)
    pad = jnp.zeros((hi_ref.shape[1] - cap, hi_ref.shape[2]), BF16)
    hi_ref[0, cap:, :] = pad
    lo_ref[0, cap:, :] = pad


def expert_ffn_down(act, xin, w_down, layer):
    e, cap, ff = act.shape
    d = w_down.shape[3]
    tn = 512
    out = jax.ShapeDtypeStruct((e, Y_ROWS, d), BF16)
    out_spec = pl.BlockSpec((1, Y_ROWS, tn), lambda i, j: (i, 0, j))
    return pl.pallas_call(
        _ffn_down_kernel,
        out_shape=[out, out],
        grid=(e, d // tn),
        in_specs=[pl.BlockSpec((1, cap, ff), lambda i, j: (i, 0, 0)),
                  pl.BlockSpec((None, 1, ff, tn), lambda i, j: (layer, i, 0, j)),
                  pl.BlockSpec((1, cap, LANES), lambda i, j: (i, 0, D_MODEL // LANES))],
        out_specs=[out_spec, out_spec],
        compiler_params=_cparams(("arbitrary", "arbitrary")),
        name="expert_ffn_down",
    )(act, w_down, xin)


COMBINE_PIECE = 32
COMBINE_ALIGN = 16
COMBINE_BLOCK = 256
COMBINE_PIECES_PER_BLOCK = COMBINE_BLOCK // COMBINE_PIECE
COMBINE_ROWS = N_EXPERTS * ((ROUTE_TILE + COMBINE_ALIGN - 1 + COMBINE_PIECE - 1) // COMBINE_PIECE) * COMBINE_PIECE
COMBINE_BLOCKS = COMBINE_ROWS // COMBINE_BLOCK


def _combine_kernel(s0_ref, cnt_ref, yh_ref, yl_ref, posm_ref, x_ref, gate_ref, lg_ref, lb_ref, sh_ref, sc_ref,
                    xo_ref, h_ref, sth_ref, stl_ref, pt_ref, acc_ref, sems):
    t = pl.program_id(0)
    slot = t % 2
    iota = lax.broadcasted_iota(I32, (COMBINE_PIECE, ROUTE_TILE), 0)

    def window_copies(slot, e, w, k):
        r0 = pl.multiple_of(k * COMBINE_PIECE, COMBINE_PIECE)
        sem = sems.at[slot, k // COMBINE_PIECES_PER_BLOCK]
        return (pltpu.make_async_copy(yh_ref.at[e, pl.ds(w, COMBINE_PIECE)],
                                      sth_ref.at[slot, pl.ds(r0, COMBINE_PIECE)], sem),
                pltpu.make_async_copy(yl_ref.at[e, pl.ds(w, COMBINE_PIECE)],
                                      stl_ref.at[slot, pl.ds(r0, COMBINE_PIECE)], sem))

    def walk(tile, slot, fetch):
        kk = jnp.int32(0)
        for e in range(N_EXPERTS):
            n = cnt_ref[tile * N_EXPERTS + e]
            s0 = s0_ref[tile * N_EXPERTS + e]
            a0 = (s0 // COMBINE_ALIGN) * COMBINE_ALIGN
            npc = jnp.where(n > 0, (s0 + n - a0 + COMBINE_PIECE - 1) // COMBINE_PIECE, 0)
            prow = None if fetch else posm_ref[e:e + 1, :]

            def piece(p, carry, e=e, a0=a0, kk=kk, prow=prow):
                w = pl.multiple_of(a0 + p * COMBINE_PIECE, COMBINE_ALIGN)
                k = kk + p
                if fetch:
                    for cp in window_copies(slot, e, w, k):
                        cp.start()
                else:
                    r0 = pl.multiple_of(k * COMBINE_PIECE, COMBINE_PIECE)
                    pt_ref[pl.ds(r0, COMBINE_PIECE), :] = jnp.where(prow == (w + iota), 1.0, 0.0).astype(BF16)
                return carry

            lax.fori_loop(0, npc, piece, 0)
            kk = kk + npc
        return kk

    @pl.when(t == 0)
    def _():
        walk(t, slot, fetch=True)

    @pl.when(t + 1 < pl.num_programs(0))
    def _():
        walk(t + 1, 1 - slot, fetch=True)

    kk = walk(t, slot, fetch=False)
    nblk = (kk + COMBINE_PIECES_PER_BLOCK - 1) // COMBINE_PIECES_PER_BLOCK

    def zero_tail(p, carry):
        r0 = pl.multiple_of((kk + p) * COMBINE_PIECE, COMBINE_PIECE)
        sth_ref[slot, pl.ds(r0, COMBINE_PIECE), :] = jnp.zeros((COMBINE_PIECE, D_MODEL), BF16)
        stl_ref[slot, pl.ds(r0, COMBINE_PIECE), :] = jnp.zeros((COMBINE_PIECE, D_MODEL), BF16)
        pt_ref[pl.ds(r0, COMBINE_PIECE), :] = jnp.zeros((COMBINE_PIECE, ROUTE_TILE), BF16)
        return carry

    lax.fori_loop(0, nblk * COMBINE_PIECES_PER_BLOCK - kk, zero_tail, 0)
    acc_ref[...] = jnp.zeros(acc_ref.shape, F32)

    def block(b, carry):
        def drain(p, c):
            for cp in window_copies(slot, 0, 0, b * COMBINE_PIECES_PER_BLOCK):
                cp.wait()
            return c

        lax.fori_loop(0, jnp.minimum(kk - b * COMBINE_PIECES_PER_BLOCK, COMBINE_PIECES_PER_BLOCK), drain, 0)
        r0 = pl.multiple_of(b * COMBINE_BLOCK, COMBINE_BLOCK)
        onehot = pt_ref[pl.ds(r0, COMBINE_BLOCK), :]
        acc_ref[...] += (_tn_dot(onehot, sth_ref[slot, pl.ds(r0, COMBINE_BLOCK), :])
                         + _tn_dot(onehot, stl_ref[slot, pl.ds(r0, COMBINE_BLOCK), :]))
        return carry

    lax.fori_loop(0, nblk, block, 0)

    z = DN_ALPHA * x_ref[...] + gate_ref[0] * acc_ref[...]
    mu = jnp.mean(z, axis=-1, keepdims=True)
    zc = z - mu
    var = jnp.mean(zc * zc, axis=-1, keepdims=True)
    xn = zc * lax.rsqrt(var + LN_EPS) * lg_ref[...] + lb_ref[...]
    xo_ref[...] = xn
    h_ref[...] = (xn * (1.0 + sc_ref[0]) + sh_ref[0]).astype(h_ref.dtype)


def moe_combine_post_norm(yh, yl, posm_all, s0, cnt, x, gate, ln_g, ln_b, sh, sc):
    tile = lambda width: pl.BlockSpec((ROUTE_TILE, width), lambda t, s0, cnt: (t, 0))
    route = pl.BlockSpec((N_EXPERTS, ROUTE_TILE), lambda t, s0, cnt: (0, t))
    stream_vec = pl.BlockSpec((1, 1, D_MODEL), lambda t, s0, cnt: (jnp.minimum(t // N_ROUTE_LAT, 1), 0, 0))
    vec = pl.BlockSpec((1, D_MODEL), lambda t, s0, cnt: (0, 0))
    return pl.pallas_call(
        _combine_kernel,
        out_shape=[jax.ShapeDtypeStruct((M_ALL, D_MODEL), F32), jax.ShapeDtypeStruct((M_ALL, D_MODEL), BF16)],
        grid_spec=pltpu.PrefetchScalarGridSpec(
            num_scalar_prefetch=2,
            grid=(N_ROUTE_TILES,),
            in_specs=[pl.BlockSpec(memory_space=pl.ANY), pl.BlockSpec(memory_space=pl.ANY), route,
                      tile(D_MODEL), stream_vec, vec, vec, stream_vec, stream_vec],
            out_specs=[tile(D_MODEL), tile(D_MODEL)],
            scratch_shapes=[pltpu.VMEM((2, COMBINE_ROWS, D_MODEL), BF16),
                            pltpu.VMEM((2, COMBINE_ROWS, D_MODEL), BF16),
                            pltpu.VMEM((COMBINE_ROWS, ROUTE_TILE), BF16),
                            pltpu.VMEM((ROUTE_TILE, D_MODEL), F32),
                            pltpu.SemaphoreType.DMA((2, COMBINE_BLOCKS))]),
        compiler_params=_cparams(("arbitrary",)),
        name="moe_combine_post_norm",
    )(s0, cnt, yh, yl, posm_all, x, gate, ln_g.reshape(1, -1), ln_b.reshape(1, -1), sh, sc)


def expert_choice_moe_post_norm(h, aff, aff3, w_gate, w_up, w_down, layer, x, gate, ln_g, ln_b, sh, sc):
    aff_t = aff[:, :N_EXPERTS].T
    posm_lat = expert_choice_select(aff_t[:, :SEQ], CAP_LAT)
    posm_ctx = expert_choice_select(aff_t[:, SEQ:], CAP_CTX)
    posm_all = jnp.concatenate([posm_lat, jnp.where(posm_ctx >= 0, posm_ctx + CAP_LAT, -1)], axis=1)
    s0, cnt = route_tile_meta(posm_all)
    xin = moe_dispatch(h, posm_all, aff3, s0, cnt)
    act = expert_ffn_up(xin, w_gate, w_up, layer)
    yh, yl = expert_ffn_down(act, xin, w_down, layer)
    return moe_combine_post_norm(yh, yl, posm_all, s0, cnt, x, gate, ln_g, ln_b, sh, sc)


def even_mixer(h, tabs, j, w_in, w_dw, b_dw, cn_g, cn_b, qa_g, kva_g, w_uq, w_ukv, w_o):
    u_main = matmul([h], w_in, F32, tn=768, layer=j, n_cols=IN_A_MAIN)
    w_kr = lax.slice(w_in, (j, 0, IN_A_MAIN), (j + 1, D_MODEL, IN_A))[0]
    w_kr = jnp.pad(w_kr, ((0, 0), (0, LANES - MLA_ROPE)))
    kr = matmul([h], w_kr, F32, tn=LANES)
    q_cat, k_cat, vt = mla_projections(u_main, kr, qa_g, kva_g, w_uq, w_ukv, tabs)
    att = mla_attention(q_cat, k_cat, vt)
    att_c = context_attention(q_cat, k_cat, vt, MLA_HEADS, MLA_QK_PAD, MLA_V, 0, 0, 0, 1.0, base2=True,
                              v_transposed=True)
    conv = conformer_conv(u_main, w_dw, b_dw, cn_g, cn_b)
    return matmul([conv, (att, att_c)], w_o, F32, tn=1024, tm=ROW_TILE, layer=j)


def odd_mixer(h, j, w_qkv, rpb, w_o):
    tn = 1024
    q_tiles = NA_HEADS * NA_HEAD_DIM // tn
    qkv = matmul([h], w_qkv, BF16, tn=tn, layer=j, lead_scale=(q_tiles, NA_HEAD_DIM ** -0.5 * LOG2_E))
    o = neighbourhood_attention(qkv, *na_bias_strips(rpb))
    o_c = context_attention(qkv, qkv, qkv, NA_HEADS, NA_HEAD_DIM, NA_HEAD_DIM, 0, NA_HEADS, 2 * NA_HEADS, 1.0,
                            base2=True)
    return matmul([(o, o_c)], w_o, F32, tn=1024, tm=ROW_TILE, layer=j)


def kernel(x, c, ctx, c_ctx, w_ada, b_ada, ln1_g, ln1_b, ln2_g, ln2_b, a_w_in, a_w_dw, a_b_dw, a_cn_g, a_cn_b,
           b_qa_g, b_kva_g, b_w_uq, b_w_ukv, ab_w_o, c_w_qkv, c_rpb, c_w_o, moe_w_router, moe_w_gate, moe_w_up,
           moe_w_down):
    assert x.shape == (1, SEQ, D_MODEL) and ctx.shape == (1, CTX_LEN, D_MODEL)
    xs = jnp.concatenate([x[0], ctx[0]], axis=0)
    cc = jnp.concatenate([c, c_ctx[None], jnp.zeros((SUBLANES - 2, D_MODEL), F32)], axis=0)
    mod = ada_modulation(cc, w_ada, b_ada)

    def vec(layer, k):
        return mod[layer, :2, k * D_MODEL:(k + 1) * D_MODEL].reshape(2, 1, D_MODEL)

    tabs = rope_tables()
    h = modulate_rows(xs, vec(0, 0), vec(0, 1))
    for layer in range(DEPTH):
        j = layer // 2
        if layer % 2 == 0:
            o = even_mixer(h, tabs, j, a_w_in, a_w_dw[j], a_b_dw[j], a_cn_g[j], a_cn_b[j], b_qa_g[j], b_kva_g[j],
                           b_w_uq[j], b_w_ukv[j], ab_w_o)
        else:
            o = odd_mixer(h, j, c_w_qkv, c_rpb[j], c_w_o)
        xs, h2, aff, aff3 = post_norm_rows(xs, o, vec(layer, 2), ln1_g[layer], ln1_b[layer], vec(layer, 3), vec(layer, 4),
                                     moe_w_router[layer])
        nxt = min(layer + 1, DEPTH - 1)
        xs, h = expert_choice_moe_post_norm(h2, aff, aff3, moe_w_gate, moe_w_up, moe_w_down, layer, xs, vec(layer, 5),
                                            ln2_g[layer], ln2_b[layer], vec(nxt, 0), vec(nxt, 1))
    return xs[:SEQ][None]
```

```python
import functools

import numpy as np
import jax
import jax.numpy as jnp
from jax import lax
from jax.experimental import pallas as pl
from jax.experimental.pallas import tpu as pltpu

F32 = jnp.float32
BF16 = jnp.bfloat16
I32 = jnp.int32

D_MODEL = 2048
SEQ = 8192
DEPTH = 4
GRID_W = 64
CTX_LEN = 256
M_ALL = SEQ + CTX_LEN

CONV_CH = 1024
CONV_WIDTH = 31
CONV_PAD = CONV_WIDTH // 2
MLA_HEADS = 8
MLA_Q_RANK = 512
MLA_KV_RANK = 512
MLA_NOPE = 128
MLA_ROPE = 64
MLA_V = 128
MLA_QK_PAD = 256
ROPE_BASE = 10000.0
NA_HEADS = 16
NA_HEAD_DIM = 128
NA_WIN_ROWS = 8
NA_WIN_COLS = 16
N_EXPERTS = 16
EXPERT_FF = 1408
EC_CAPACITY_FACTOR = 2
CAP_LAT = EC_CAPACITY_FACTOR * SEQ // N_EXPERTS
CAP_CTX = EC_CAPACITY_FACTOR * CTX_LEN // N_EXPERTS
CAP_ALL = CAP_LAT + CAP_CTX
IN_A = 2 * CONV_CH + MLA_Q_RANK + MLA_KV_RANK + MLA_ROPE
IN_A_MAIN = IN_A - MLA_ROPE
LOG2_E = 1.4426950408889634
DN_ALPHA = (2 * DEPTH) ** 0.25
LN_EPS = 1e-5
RMS_EPS = 1e-6
NEG_INF = -1e30

LANES = 128
SUBLANES = 8
VMEM_LIMIT = 56 * 1024 * 1024

ROW_TILE = 256
N_LAT_TILES = SEQ // ROW_TILE
N_ROW_TILES = M_ALL // ROW_TILE
MM_ROW_TILE = 768
ROUTE_TILE = 128
N_ROUTE_TILES = M_ALL // ROUTE_TILE
N_ROUTE_LAT = SEQ // ROUTE_TILE
Y_ROWS = CAP_ALL + 32


def _cparams(sem, vmem=VMEM_LIMIT):
    return pltpu.CompilerParams(dimension_semantics=sem, vmem_limit_bytes=vmem)


def _nt_dot(a, b):
    return lax.dot_general(a, b, (((1,), (1,)), ((), ())), preferred_element_type=F32)


def _tn_dot(a, b):
    return lax.dot_general(a, b, (((0,), (0,)), ((), ())), preferred_element_type=F32)


def _ada_kernel(c_ref, w_ref, b_ref, o_ref):
    c = c_ref[...]
    a = (c * jax.nn.sigmoid(c)).astype(BF16)
    o_ref[0] = jnp.dot(a, w_ref[0].astype(BF16), preferred_element_type=F32) + b_ref[0]


def ada_modulation(cc, w_ada, b_ada):
    depth, d, n = w_ada.shape
    tn = 1024
    return pl.pallas_call(
        _ada_kernel,
        out_shape=jax.ShapeDtypeStruct((depth, SUBLANES, n), F32),
        grid=(depth, n // tn),
        in_specs=[
            pl.BlockSpec((SUBLANES, d), lambda l, j: (0, 0)),
            pl.BlockSpec((1, d, tn), lambda l, j: (l, 0, j)),
            pl.BlockSpec((1, 1, tn), lambda l, j: (l, 0, j)),
        ],
        out_specs=pl.BlockSpec((1, SUBLANES, tn), lambda l, j: (l, 0, j)),
        compiler_params=_cparams(("arbitrary", "arbitrary")),
        name="ada_modulation",
    )(cc, w_ada, b_ada.reshape(depth, 1, n))


def _stream_vec_spec():
    return pl.BlockSpec((1, 1, D_MODEL), lambda i: (jnp.minimum(i // N_LAT_TILES, 1), 0, 0))


def _row_spec(width=D_MODEL):
    return pl.BlockSpec((ROW_TILE, width), lambda i: (i, 0))


def _modulate_kernel(x_ref, sh_ref, sc_ref, h_ref):
    h_ref[...] = (x_ref[...] * (1.0 + sc_ref[0]) + sh_ref[0]).astype(h_ref.dtype)


def modulate_rows(x, sh, sc):
    return pl.pallas_call(
        _modulate_kernel,
        out_shape=jax.ShapeDtypeStruct(x.shape, BF16),
        grid=(N_ROW_TILES,),
        in_specs=[_row_spec(), _stream_vec_spec(), _stream_vec_spec()],
        out_specs=_row_spec(),
        compiler_params=_cparams(("arbitrary",)),
        name="modulate_rows",
    )(x, sh, sc)


def _post_norm_kernel(*refs, with_router):
    if with_router:
        x_ref, y_ref, gate_ref, lg_ref, lb_ref, sh_ref, sc_ref, wr_ref, xo_ref, h_ref, aff_ref, aff3_ref = refs
    else:
        x_ref, y_ref, gate_ref, lg_ref, lb_ref, sh_ref, sc_ref, xo_ref, h_ref = refs
    z = DN_ALPHA * x_ref[...] + gate_ref[0] * y_ref[...]
    mu = jnp.mean(z, axis=-1, keepdims=True)
    zc = z - mu
    var = jnp.mean(zc * zc, axis=-1, keepdims=True)
    xn = zc * lax.rsqrt(var + LN_EPS) * lg_ref[...] + lb_ref[...]
    xo_ref[...] = xn
    h = (xn * (1.0 + sc_ref[0]) + sh_ref[0]).astype(BF16)
    h_ref[...] = h
    if with_router:
        logits = jnp.dot(h, wr_ref[...].astype(BF16), preferred_element_type=F32)
        lane = lax.broadcasted_iota(I32, logits.shape, 1)
        logits = jnp.where(lane < N_EXPERTS, logits, NEG_INF)
        e = jnp.exp(logits - jnp.max(logits, axis=-1, keepdims=True))
        aff = e / jnp.sum(e, axis=-1, keepdims=True)
        aff_ref[...] = aff
        hi = aff.astype(BF16).astype(F32)
        mid = (aff - hi).astype(BF16).astype(F32)
        lo = (aff - hi - mid).astype(BF16).astype(F32)
        aff3_ref[...] = (hi + pltpu.roll(mid, N_EXPERTS, 1) + pltpu.roll(lo, 2 * N_EXPERTS, 1)).astype(BF16)


def post_norm_rows(x, y, gate, ln_g, ln_b, sh, sc, w_router=None):
    with_router = w_router is not None
    vec = pl.BlockSpec((1, D_MODEL), lambda i: (0, 0))
    in_specs = [_row_spec(), _row_spec(), _stream_vec_spec(), vec, vec, _stream_vec_spec(), _stream_vec_spec()]
    out_shape = [jax.ShapeDtypeStruct(x.shape, F32), jax.ShapeDtypeStruct(x.shape, BF16)]
    out_specs = [_row_spec(), _row_spec()]
    args = [x, y, gate, ln_g.reshape(1, -1), ln_b.reshape(1, -1), sh, sc]
    if with_router:
        in_specs.append(pl.BlockSpec((D_MODEL, LANES), lambda i: (0, 0)))
        out_shape += [jax.ShapeDtypeStruct((x.shape[0], LANES), F32), jax.ShapeDtypeStruct((x.shape[0], LANES), BF16)]
        out_specs += [_row_spec(LANES), _row_spec(LANES)]
        args.append(jnp.pad(w_router, ((0, 0), (0, LANES - N_EXPERTS))))
    return pl.pallas_call(
        functools.partial(_post_norm_kernel, with_router=with_router),
        out_shape=out_shape,
        grid=(N_ROW_TILES,),
        in_specs=in_specs,
        out_specs=out_specs,
        compiler_params=_cparams(("arbitrary",)),
        name="post_norm_router" if with_router else "post_norm",
    )(*args)


def _matmul_kernel(*refs, splits, paired, lead_scale):
    n_refs = len(splits) + sum(paired)
    a_refs, w_ref, o_ref, wb_ref = list(refs[:n_refs]), refs[n_refs], refs[n_refs + 1], refs[n_refs + 2]

    @pl.when(pl.program_id(1) == 0)
    def _():
        wb_ref[...] = w_ref[...].astype(BF16)

    acc = None
    off = 0
    for k, pair in zip(splits, paired):
        a = a_refs.pop(0)[...]
        if pair:
            a = jnp.where(pl.program_id(1) < N_LAT_TILES, a, a_refs.pop(0)[...])
        part = jnp.dot(a.astype(BF16), wb_ref[off:off + k, :], preferred_element_type=F32)
        acc = part if acc is None else acc + part
        off += k
    if lead_scale is not None:
        n_tiles, value = lead_scale
        acc = acc * jnp.where(pl.program_id(0) < n_tiles, value, 1.0)
    o_ref[...] = acc.astype(o_ref.dtype)


def matmul(a_list, w, out_dtype, tn, tm=MM_ROW_TILE, layer=None, n_cols=None, lead_scale=None):
    paired = tuple(isinstance(a, tuple) for a in a_list)
    splits = tuple(a[0].shape[1] if p else a.shape[1] for a, p in zip(a_list, paired))
    m = M_ALL if any(paired) else a_list[0].shape[0]
    k, n = w.shape[-2:]
    n = n if n_cols is None else n_cols
    assert sum(splits) == k and n % tn == 0 and m % tm == 0 and (tm == ROW_TILE or not any(paired))
    in_specs, args = [], []
    for a, ki, p in zip(a_list, splits, paired):
        if p:
            in_specs += [pl.BlockSpec((tm, ki), lambda j, i: (jnp.minimum(i, N_LAT_TILES - 1), 0)),
                         pl.BlockSpec((tm, ki), lambda j, i: (0, 0))]
            args += list(a)
        else:
            in_specs.append(pl.BlockSpec((tm, ki), lambda j, i: (i, 0)))
            args.append(a)
    if layer is None:
        in_specs.append(pl.BlockSpec((k, tn), lambda j, i: (0, j)))
    else:
        in_specs.append(pl.BlockSpec((None, k, tn), lambda j, i: (layer, 0, j)))
    return pl.pallas_call(
        functools.partial(_matmul_kernel, splits=splits, paired=paired, lead_scale=lead_scale),
        out_shape=jax.ShapeDtypeStruct((m, n), out_dtype),
        grid=(n // tn, m // tm),
        in_specs=in_specs,
        out_specs=pl.BlockSpec((tm, tn), lambda j, i: (i, j)),
        scratch_shapes=[pltpu.VMEM((k, tn), BF16)],
        compiler_params=_cparams(("arbitrary", "arbitrary")),
        name="matmul",
    )(*args, w)


def rope_tables():
    t = jnp.arange(SEQ, dtype=I32)
    row = (t // GRID_W).astype(F32)
    col = (t % GRID_W).astype(F32)
    n_freq = MLA_ROPE // 4
    inv_freq = ROPE_BASE ** (-jnp.arange(n_freq, dtype=F32) / n_freq)
    ang = jnp.concatenate([row[:, None] * inv_freq, col[:, None] * inv_freq], axis=-1)
    cos, sin = jnp.cos(ang), jnp.sin(ang)
    half = MLA_ROPE // 2
    cos = jnp.concatenate([cos, jnp.ones((CTX_LEN, half), F32)], axis=0)
    sin = jnp.concatenate([sin, jnp.zeros((CTX_LEN, half), F32)], axis=0)
    z = jnp.zeros_like(cos)
    c_tab = jnp.concatenate([cos, cos, z, z], axis=1)
    sa_tab = jnp.concatenate([z, sin, z, z], axis=1)
    sb_tab = jnp.concatenate([-sin, z, z, z], axis=1)
    return c_tab, sa_tab, sb_tab


def _mla_proj_kernel(cq_ref, ckv_ref, kr_ref, qg_ref, kvg_ref, wq_ref, wkv_ref, c_ref, sa_ref, sb_ref,
                     q_out, k_out, vt_out, wqb, wkvb):
    @pl.when(pl.program_id(0) == 0)
    def _():
        wqb[...] = wq_ref[...].astype(BF16)
        wkvb[...] = wkv_ref[...].astype(BF16)

    def rms(x, g):
        return x * lax.rsqrt(jnp.mean(x * x, axis=-1, keepdims=True) + RMS_EPS) * g

    c_tab, sa_tab, sb_tab = c_ref[...], sa_ref[...], sb_ref[...]
    half = MLA_ROPE // 2

    def rope(g):
        return g * c_tab + pltpu.roll(g, half, 1) * sa_tab + pltpu.roll(g, LANES - half, 1) * sb_tab

    qk_scale = (MLA_NOPE + MLA_ROPE) ** -0.5 * LOG2_E
    q = jnp.dot(rms(cq_ref[...], qg_ref[...]).astype(BF16), wqb[...], preferred_element_type=F32) * qk_scale
    kv = jnp.dot(rms(ckv_ref[...], kvg_ref[...]).astype(BF16), wkvb[...], preferred_element_type=F32)
    k_rope = rope(kr_ref[...]).astype(BF16)
    for h in range(MLA_HEADS):
        lo = h * MLA_QK_PAD
        q_out[:, lo:lo + LANES] = q[:, lo:lo + LANES].astype(BF16)
        q_out[:, lo + LANES:lo + 2 * LANES] = rope(q[:, lo + LANES:lo + 2 * LANES]).astype(BF16)
        k_out[:, lo:lo + LANES] = kv[:, h * MLA_NOPE:(h + 1) * MLA_NOPE].astype(BF16)
        k_out[:, lo + LANES:lo + 2 * LANES] = k_rope
    vt_out[...] = kv[:, MLA_HEADS * MLA_NOPE:].T.astype(BF16)


def mla_projections(u_main, kr, qa_g, kva_g, w_uq, w_ukv, tabs):
    m = u_main.shape[0]
    tm = 384
    hq = MLA_HEADS * MLA_QK_PAD
    wq = w_uq.reshape(MLA_Q_RANK, MLA_HEADS, MLA_NOPE + MLA_ROPE)
    wq = jnp.pad(wq, ((0, 0), (0, 0), (0, MLA_QK_PAD - MLA_NOPE - MLA_ROPE))).reshape(MLA_Q_RANK, hq)
    wkv = w_ukv.reshape(MLA_KV_RANK, MLA_HEADS, MLA_NOPE + MLA_V)
    wkv = jnp.concatenate([wkv[..., :MLA_NOPE].reshape(MLA_KV_RANK, -1), wkv[..., MLA_NOPE:].reshape(MLA_KV_RANK, -1)], axis=1)
    nkv = wkv.shape[1]
    cq_blk = 2 * CONV_CH // MLA_Q_RANK
    tab_spec = pl.BlockSpec((tm, LANES), lambda i: (i, 0))
    return pl.pallas_call(
        _mla_proj_kernel,
        out_shape=[jax.ShapeDtypeStruct((m, hq), BF16), jax.ShapeDtypeStruct((m, hq), BF16),
                   jax.ShapeDtypeStruct((MLA_HEADS * MLA_V, m), BF16)],
        grid=(m // tm,),
        in_specs=[
            pl.BlockSpec((tm, MLA_Q_RANK), lambda i: (i, cq_blk)),
            pl.BlockSpec((tm, MLA_KV_RANK), lambda i: (i, cq_blk + 1)),
            pl.BlockSpec((tm, LANES), lambda i: (i, 0)),
            pl.BlockSpec((1, MLA_Q_RANK), lambda i: (0, 0)),
            pl.BlockSpec((1, MLA_KV_RANK), lambda i: (0, 0)),
            pl.BlockSpec((MLA_Q_RANK, hq), lambda i: (0, 0)),
            pl.BlockSpec((MLA_KV_RANK, nkv), lambda i: (0, 0)),
            tab_spec, tab_spec, tab_spec,
        ],
        out_specs=[pl.BlockSpec((tm, hq), lambda i: (i, 0)), pl.BlockSpec((tm, hq), lambda i: (i, 0)),
                   pl.BlockSpec((MLA_HEADS * MLA_V, tm), lambda i: (0, i))],
        scratch_shapes=[pltpu.VMEM((MLA_Q_RANK, hq), BF16), pltpu.VMEM((MLA_KV_RANK, nkv), BF16)],
        compiler_params=_cparams(("arbitrary",)),
        name="mla_projections",
    )(u_main, u_main, kr, qa_g.reshape(1, -1), kva_g.reshape(1, -1), wq, wkv, *tabs)


MLA_KEY_CHUNK = 1408
MLA_Q_TILE = 512


def _flash_kernel(q_ref, k_ref, vt_ref, o_ref, *, tk, n_chunks):
    q = q_ref[...]
    tq = q.shape[0]
    m = jnp.full((1, tq), NEG_INF, F32)
    l = jnp.zeros((1, tq), F32)
    acc = jnp.zeros((vt_ref.shape[0], tq), F32)
    s_next = _nt_dot(k_ref[0:tk, :], q)
    for c in range(n_chunks):
        s = s_next
        if c + 1 < n_chunks:
            s_next = _nt_dot(k_ref[(c + 1) * tk:(c + 2) * tk, :], q)
        m_new = jnp.maximum(m, jnp.max(s, axis=0, keepdims=True))
        alpha = jnp.exp2(m - m_new)
        p = jnp.exp2(s - m_new)
        l = alpha * l + jnp.sum(p, axis=0, keepdims=True)
        acc = alpha * acc + jnp.dot(vt_ref[:, c * tk:(c + 1) * tk], p.astype(BF16), preferred_element_type=F32)
        m = m_new
    o_ref[...] = (acc / l).T.astype(o_ref.dtype)


def mla_attention(q_cat, k_cat, vt):
    m = k_cat.shape[0]
    tq, tk = MLA_Q_TILE, MLA_KEY_CHUNK
    return pl.pallas_call(
        functools.partial(_flash_kernel, tk=tk, n_chunks=m // tk),
        out_shape=jax.ShapeDtypeStruct((SEQ, MLA_HEADS * MLA_V), BF16),
        grid=(MLA_HEADS, SEQ // tq),
        in_specs=[
            pl.BlockSpec((tq, MLA_QK_PAD), lambda h, i: (i, h)),
            pl.BlockSpec((m, MLA_QK_PAD), lambda h, i: (0, h)),
            pl.BlockSpec((MLA_V, m), lambda h, i: (h, 0)),
        ],
        out_specs=pl.BlockSpec((tq, MLA_V), lambda h, i: (i, h)),
        compiler_params=_cparams(("arbitrary", "arbitrary")),
        name="mla_attention",
    )(q_cat, k_cat, vt)


def _ctx_attn_kernel(q_ref, k_ref, v_ref, o_ref, *, scale, base2, v_transposed):
    s = _nt_dot(q_ref[...], k_ref[...]) * scale
    z = s - jnp.max(s, axis=-1, keepdims=True)
    p = jnp.exp2(z) if base2 else jnp.exp(z)
    l = jnp.sum(p, axis=-1, keepdims=True)
    pv = _nt_dot(p.astype(BF16), v_ref[...]) if v_transposed else jnp.dot(p.astype(BF16), v_ref[...],
                                                                         preferred_element_type=F32)
    o_ref[...] = (pv / l).astype(o_ref.dtype)


def context_attention(q_arr, k_arr, v_arr, heads, dq, dv, q_col, k_col, v_col, scale, base2=False,
                      v_transposed=False):
    rb = SEQ // CTX_LEN
    if v_transposed:
        v_spec = pl.BlockSpec((dv, CTX_LEN), lambda h: (v_col + h, rb))
    else:
        v_spec = pl.BlockSpec((CTX_LEN, dv), lambda h: (rb, v_col + h))
    return pl.pallas_call(
        functools.partial(_ctx_attn_kernel, scale=scale, base2=base2, v_transposed=v_transposed),
        out_shape=jax.ShapeDtypeStruct((CTX_LEN, heads * dv), BF16),
        grid=(heads,),
        in_specs=[
            pl.BlockSpec((CTX_LEN, dq), lambda h: (rb, q_col + h)),
            pl.BlockSpec((CTX_LEN, dq), lambda h: (rb, k_col + h)),
            v_spec,
        ],
        out_specs=pl.BlockSpec((CTX_LEN, dv), lambda h: (0, h)),
        compiler_params=_cparams(("arbitrary",)),
        name="context_attention",
    )(q_arr, k_arr, v_arr)


CONV_HALO = 16
CONV_ROW_BLOCK = 64


def _conv_kernel(a_ref, g_ref, ap_ref, gp_ref, an_ref, gn_ref, w_ref, b_ref, cg_ref, cb_ref, o_ref, hbuf, hshift,
                 cbuf):
    i = pl.program_id(0)
    tl = a_ref.shape[0]

    def glu(a, g):
        return a[...] * jax.nn.sigmoid(g[...])

    has_prev = jnp.logical_and(i != 0, i != N_LAT_TILES)
    has_next = jnp.logical_and(i != N_LAT_TILES - 1, i != N_ROW_TILES - 1)
    hbuf[0:CONV_HALO, :] = jnp.where(has_prev, glu(ap_ref, gp_ref), 0.0)
    hbuf[CONV_HALO:CONV_HALO + tl, :] = glu(a_ref, g_ref)
    hbuf[CONV_HALO + tl:, :] = jnp.where(has_next, glu(an_ref, gn_ref), 0.0)

    span = hshift.shape[1]
    for s in range(SUBLANES):
        hshift[s] = hbuf[s:s + span, :]

    base = CONV_HALO - CONV_PAD
    for rb in range(tl // CONV_ROW_BLOCK):
        r0 = rb * CONV_ROW_BLOCK
        for c in range(CONV_CH // LANES):
            cs = slice(c * LANES, (c + 1) * LANES)
            acc = jnp.broadcast_to(b_ref[:, cs], (CONV_ROW_BLOCK, LANES))
            for j in range(CONV_WIDTH):
                s = (base + j) % SUBLANES
                a = r0 + base + j - s
                acc = acc + hshift[s, a:a + CONV_ROW_BLOCK, cs] * w_ref[j:j + 1, cs]
            cbuf[r0:r0 + CONV_ROW_BLOCK, cs] = acc

    y = cbuf[...]
    mu = jnp.mean(y, axis=-1, keepdims=True)
    yc = y - mu
    var = jnp.mean(yc * yc, axis=-1, keepdims=True)
    yn = yc * lax.rsqrt(var + LN_EPS) * cg_ref[...] + cb_ref[...]
    o_ref[...] = (yn * jax.nn.sigmoid(yn)).astype(o_ref.dtype)


def conformer_conv(u_main, w_dw, b_dw, cn_g, cn_b):
    m = u_main.shape[0]
    tl = ROW_TILE
    hpt = tl // CONV_HALO
    n_halo = m // CONV_HALO
    main = lambda col: pl.BlockSpec((tl, CONV_CH), lambda i: (i, col))
    prev = lambda col: pl.BlockSpec((CONV_HALO, CONV_CH), lambda i: (jnp.maximum(i * hpt - 1, 0), col))
    nxt = lambda col: pl.BlockSpec((CONV_HALO, CONV_CH), lambda i: (jnp.minimum((i + 1) * hpt, n_halo - 1), col))
    vec = pl.BlockSpec((1, CONV_CH), lambda i: (0, 0))
    return pl.pallas_call(
        _conv_kernel,
        out_shape=jax.ShapeDtypeStruct((m, CONV_CH), BF16),
        grid=(m // tl,),
        in_specs=[main(0), main(1), prev(0), prev(1), nxt(0), nxt(1),
                  pl.BlockSpec((CONV_WIDTH, CONV_CH), lambda i: (0, 0)), vec, vec, vec],
        out_specs=pl.BlockSpec((tl, CONV_CH), lambda i: (i, 0)),
        scratch_shapes=[pltpu.VMEM((tl + 2 * CONV_HALO, CONV_CH), F32),
                        pltpu.VMEM((SUBLANES, tl + 2 * CONV_HALO - SUBLANES, CONV_CH), F32),
                        pltpu.VMEM((tl, CONV_CH), F32)],
        compiler_params=_cparams(("arbitrary",)),
        name="conformer_conv",
    )(u_main, u_main, u_main, u_main, u_main, u_main, w_dw, b_dw.reshape(1, -1), cn_g.reshape(1, -1), cn_b.reshape(1, -1))


NA_Q_ROWS = 8
NA_K_ROWS = 16
NA_KBLK = 4 * GRID_W
NA_QTOK = NA_Q_ROWS * GRID_W
NA_KTOK = NA_K_ROWS * GRID_W
NA_GROUPS = SEQ // NA_QTOK


NA_HEADS_PER_STEP = 2
NA_CLASSES = (0, 1, NA_GROUPS - 1)


def _na_window_rows():
    rows = SEQ // GRID_W
    out = []
    for g in NA_CLASSES:
        qr = (NA_Q_ROWS * g + np.arange(NA_Q_ROWS))[:, None]
        kr = (NA_Q_ROWS * g - (NA_K_ROWS - NA_Q_ROWS) // 2 + np.arange(NA_K_ROWS))[None, :]
        start = np.clip(qr - NA_WIN_ROWS // 2, 0, rows - NA_WIN_ROWS)
        out.append((kr >= start) & (kr < start + NA_WIN_ROWS))
    return np.stack(out)


def na_bias_strips(rpb):
    qc = np.arange(GRID_W)[:, None]
    kc = np.arange(GRID_W)[None, :]
    ws = np.clip(qc - NA_WIN_COLS // 2, 0, GRID_W - NA_WIN_COLS)
    col_valid = (kc >= ws) & (kc < ws + NA_WIN_COLS)
    col_idx = np.clip(kc - qc + NA_WIN_COLS - 1, 0, 2 * NA_WIN_COLS - 2)
    onehot = jnp.asarray((col_idx[None] == np.arange(2 * NA_WIN_COLS - 1)[:, None, None]).astype(np.float32))
    cm = jnp.einsum('hrd,dqk->hrqk', rpb, onehot, precision=lax.Precision.HIGHEST) * LOG2_E
    cm = jnp.where(col_valid[None, None], cm, NEG_INF)
    neg = jnp.full_like(cm, NEG_INF)
    return jnp.concatenate([cm, neg], axis=-1), jnp.concatenate([neg, cm], axis=-1)


def _na_kernel(q_ref, k0, k1, k2, k3, v0, v1, v2, v3, kc_ref, vc_ref, bl_ref, br_ref, o_ref, tbl_ref):
    g = pl.program_id(1)
    row_valid = _na_window_rows()

    @pl.when(g == 0)
    def _():
        neg_tile = jnp.full((GRID_W, LANES), NEG_INF, F32)
        for hh in range(NA_HEADS_PER_STEP):
            for c in range(len(NA_CLASSES)):
                for ql in range(NA_Q_ROWS):
                    for kp in range(NA_K_ROWS // 2):
                        ka, kb = 2 * kp, 2 * kp + 1
                        da = ka - ql + NA_WIN_ROWS - 1 - (NA_K_ROWS - NA_Q_ROWS) // 2
                        ta = bl_ref[hh, da] if row_valid[c, ql, ka] else neg_tile
                        tb = br_ref[hh, da + 1] if row_valid[c, ql, kb] else neg_tile
                        tbl_ref[hh, c, ql * GRID_W:(ql + 1) * GRID_W, kp * LANES:(kp + 1) * LANES] = jnp.maximum(ta, tb)

    cls = jnp.where(g == 0, 0, jnp.where(g == NA_GROUPS - 1, 2, 1))
    for hh in range(NA_HEADS_PER_STEP):
        hs = slice(hh * NA_HEAD_DIM, (hh + 1) * NA_HEAD_DIM)
        q = q_ref[:, hs]
        s = jnp.concatenate([_nt_dot(q, k[:, hs]) for k in (k0, k1, k2, k3)], axis=1) + tbl_ref[hh, cls]
        sc = _nt_dot(q, kc_ref[:, hs])
        m = jnp.maximum(jnp.max(s, axis=-1, keepdims=True), jnp.max(sc, axis=-1, keepdims=True))
        p = jnp.exp2(s - m)
        pc = jnp.exp2(sc - m)
        l = jnp.sum(p, axis=-1, keepdims=True) + jnp.sum(pc, axis=-1, keepdims=True)
        o = jnp.dot(pc.astype(BF16), vc_ref[:, hs], preferred_element_type=F32)
        for j, v in enumerate((v0, v1, v2, v3)):
            o = o + jnp.dot(p[:, j * NA_KBLK:(j + 1) * NA_KBLK].astype(BF16), v[:, hs], preferred_element_type=F32)
        o_ref[:, hs] = (o / l).astype(o_ref.dtype)


def neighbourhood_attention(qkv, bias_left, bias_right):
    n_kblk = SEQ // NA_KBLK
    ctx_blk = SEQ // CTX_LEN
    hp = NA_HEADS_PER_STEP
    width = hp * NA_HEAD_DIM
    n_pairs = NA_HEADS // hp

    def kv_spec(j, col0):
        return pl.BlockSpec((NA_KBLK, width), lambda h, g: (jnp.clip(2 * g - 1 + j, 0, n_kblk - 1), col0 + h))

    strip_spec = pl.BlockSpec((hp, 2 * NA_WIN_ROWS - 1, GRID_W, LANES), lambda h, g: (h, 0, 0, 0))
    return pl.pallas_call(
        _na_kernel,
        out_shape=jax.ShapeDtypeStruct((SEQ, NA_HEADS * NA_HEAD_DIM), BF16),
        grid=(n_pairs, NA_GROUPS),
        in_specs=[pl.BlockSpec((NA_QTOK, width), lambda h, g: (g, h))]
        + [kv_spec(j, n_pairs) for j in range(4)]
        + [kv_spec(j, 2 * n_pairs) for j in range(4)]
        + [pl.BlockSpec((CTX_LEN, width), lambda h, g: (ctx_blk, n_pairs + h)),
           pl.BlockSpec((CTX_LEN, width), lambda h, g: (ctx_blk, 2 * n_pairs + h)),
           strip_spec, strip_spec],
        out_specs=pl.BlockSpec((NA_QTOK, width), lambda h, g: (g, h)),
        scratch_shapes=[pltpu.VMEM((hp, len(NA_CLASSES), NA_QTOK, NA_KTOK), F32)],
        compiler_params=_cparams(("arbitrary", "arbitrary")),
        name="neighbourhood_attention",
    )(*([qkv] * 11), bias_left, bias_right)


def _select_kernel(aff_ref, posm_ref, *, cap):
    bits = pltpu.bitcast(aff_ref[...], I32)
    n = bits.shape[1]

    def search(i, thr):
        cand = thr | (jnp.int32(1) << (30 - i))
        cnt = jnp.sum((bits >= cand).astype(I32), axis=1, keepdims=True)
        return jnp.where(cnt >= cap, cand, thr)

    thr = lax.fori_loop(0, 31, search, jnp.zeros((N_EXPERTS, 1), I32))
    gt = bits > thr
    eq = bits == thr
    need = (cap - jnp.sum(gt.astype(I32), axis=1, keepdims=True)).astype(F32)
    tri = jnp.where(lax.broadcasted_iota(I32, (LANES, LANES), 0) <= lax.broadcasted_iota(I32, (LANES, LANES), 1),
                    1.0, 0.0).astype(BF16)
    off_eq = jnp.zeros((N_EXPERTS, 1), F32)
    off_sel = jnp.zeros((N_EXPERTS, 1), F32)
    for b in range(n // LANES):
        cs = slice(b * LANES, (b + 1) * LANES)
        eq_b = jnp.where(eq[:, cs], 1.0, 0.0)
        inc_eq = jnp.dot(eq_b.astype(BF16), tri, preferred_element_type=F32)
        rank = off_eq + inc_eq - eq_b
        sel_b = jnp.logical_or(gt[:, cs], jnp.logical_and(eq[:, cs], rank < need))
        sel_f = jnp.where(sel_b, 1.0, 0.0)
        inc_sel = jnp.dot(sel_f.astype(BF16), tri, preferred_element_type=F32)
        pos = off_sel + inc_sel - sel_f
        posm_ref[:, cs] = jnp.where(sel_b, pos.astype(I32), -1)
        off_eq = off_eq + inc_eq[:, LANES - 1:LANES]
        off_sel = off_sel + inc_sel[:, LANES - 1:LANES]


def expert_choice_select(aff_t, cap):
    return pl.pallas_call(
        functools.partial(_select_kernel, cap=cap),
        out_shape=jax.ShapeDtypeStruct(aff_t.shape, I32),
        compiler_params=_cparams(None),
        name="expert_choice_select",
    )(aff_t)


def route_tile_meta(posm_all):
    cnt = (posm_all >= 0).astype(I32).reshape(N_EXPERTS, N_ROUTE_TILES, ROUTE_TILE).sum(-1).T
    lat, ctx = cnt[:N_ROUTE_LAT], cnt[N_ROUTE_LAT:]
    s0 = jnp.concatenate([jnp.cumsum(lat, 0) - lat, CAP_LAT + jnp.cumsum(ctx, 0) - ctx], axis=0)
    return s0.reshape(-1), cnt.reshape(-1)


DISPATCH_PIECE = 16
DISPATCH_BLOCK = 256
DISPATCH_ROWS = N_EXPERTS * ((ROUTE_TILE + SUBLANES - 1 + DISPATCH_PIECE - 1) // DISPATCH_PIECE) * DISPATCH_PIECE


XIN_WIDTH = D_MODEL + LANES


def _dispatch_kernel(s0_ref, cnt_ref, h_ref, posm_ref, aff3_ref, xin_ref, pt_ref, x_ref, carry_ref, sem):
    t = pl.program_id(0)

    @pl.when(t == 0)
    def _():
        carry_ref[...] = jnp.zeros(carry_ref.shape, F32)

    pt_ref[...] = jnp.zeros(pt_ref.shape, pt_ref.dtype)
    iota = lax.broadcasted_iota(I32, (DISPATCH_PIECE, ROUTE_TILE), 0)
    offs = []
    o = jnp.int32(0)
    for e in range(N_EXPERTS):
        n = cnt_ref[t * N_EXPERTS + e]
        s0 = s0_ref[t * N_EXPERTS + e]
        a0 = (s0 // SUBLANES) * SUBLANES
        npc = jnp.where(n > 0, (s0 + n - a0 + DISPATCH_PIECE - 1) // DISPATCH_PIECE, 0)
        row = posm_ref[e:e + 1, :]
        offs.append(o)

        def piece(p, carry, row=row, a0=a0, o=o):
            r0 = pl.multiple_of(o + p * DISPATCH_PIECE, DISPATCH_PIECE)
            hit = row == (a0 + p * DISPATCH_PIECE + iota)
            pt_ref[pl.ds(r0, DISPATCH_PIECE), :] = jnp.where(hit, 1.0, 0.0).astype(BF16)
            return carry

        lax.fori_loop(0, npc, piece, 0)
        o = o + npc * DISPATCH_PIECE

    def block(b, carry):
        r0 = pl.multiple_of(b * DISPATCH_BLOCK, DISPATCH_BLOCK)
        onehot = pt_ref[pl.ds(r0, DISPATCH_BLOCK), :]
        x_ref[pl.ds(r0, DISPATCH_BLOCK), 0:D_MODEL] = jnp.dot(onehot, h_ref[...], preferred_element_type=F32)
        x_ref[pl.ds(r0, DISPATCH_BLOCK), D_MODEL:] = jnp.dot(onehot, aff3_ref[...], preferred_element_type=F32)
        return carry

    lax.fori_loop(0, (o + DISPATCH_BLOCK - 1) // DISPATCH_BLOCK, block, 0)

    def tile_copy(src_row, e, dst_row):
        return pltpu.make_async_copy(x_ref.at[pl.ds(src_row, SUBLANES)], xin_ref.at[e, pl.ds(dst_row, SUBLANES)], sem)

    total = jnp.int32(0)
    for e in range(N_EXPERTS):
        n = cnt_ref[t * N_EXPERTS + e]
        s0 = s0_ref[t * N_EXPERTS + e]
        a0 = (s0 // SUBLANES) * SUBLANES
        span = s0 + n - a0
        full = jnp.where(n > 0, span // SUBLANES, 0)
        o_e = offs[e]

        @pl.when(n > 0)
        def _(e=e, a0=a0, span=span, full=full, o_e=o_e):
            head = pl.multiple_of(o_e, SUBLANES)
            x_ref[pl.ds(head, SUBLANES), :] += carry_ref[e]

            def issue(g, carry):
                tile_copy(pl.multiple_of(o_e + g * SUBLANES, SUBLANES), e,
                          pl.multiple_of(a0 + g * SUBLANES, SUBLANES)).start()
                return carry

            lax.fori_loop(0, full, issue, 0)
            tail = x_ref[pl.ds(pl.multiple_of(o_e + full * SUBLANES, SUBLANES), SUBLANES), :]
            carry_ref[e] = jnp.where(span - full * SUBLANES > 0, tail, 0.0)

        total = total + full

    def drain(g, carry):
        tile_copy(0, 0, 0).wait()
        return carry

    lax.fori_loop(0, total, drain, 0)


def moe_dispatch(h, posm_all, aff3, s0, cnt):
    return pl.pallas_call(
        _dispatch_kernel,
        out_shape=jax.ShapeDtypeStruct((N_EXPERTS, CAP_ALL, XIN_WIDTH), F32),
        grid_spec=pltpu.PrefetchScalarGridSpec(
            num_scalar_prefetch=2,
            grid=(N_ROUTE_TILES,),
            in_specs=[pl.BlockSpec((ROUTE_TILE, D_MODEL), lambda t, s0, cnt: (t, 0)),
                      pl.BlockSpec((N_EXPERTS, ROUTE_TILE), lambda t, s0, cnt: (0, t)),
                      pl.BlockSpec((ROUTE_TILE, LANES), lambda t, s0, cnt: (t, 0))],
            out_specs=pl.BlockSpec(memory_space=pl.ANY),
            scratch_shapes=[pltpu.VMEM((DISPATCH_ROWS, ROUTE_TILE), BF16),
                            pltpu.VMEM((DISPATCH_ROWS + SUBLANES, XIN_WIDTH), F32),
                            pltpu.VMEM((N_EXPERTS, SUBLANES, XIN_WIDTH), F32),
                            pltpu.SemaphoreType.DMA(())]),
        compiler_params=_cparams(("arbitrary",)),
        name="moe_dispatch",
    )(s0, cnt, h, posm_all, aff3)


def _ffn_up_kernel(x_ref, wg_ref, wu_ref, o_ref, ag_ref, au_ref):
    k = pl.program_id(1)
    x = x_ref[0].astype(BF16)
    pg = jnp.dot(x, wg_ref[0].astype(BF16), preferred_element_type=F32)
    pu = jnp.dot(x, wu_ref[0].astype(BF16), preferred_element_type=F32)

    @pl.when(k == 0)
    def _():
        ag_ref[...] = pg
        au_ref[...] = pu

    @pl.when(k > 0)
    def _():
        ag_ref[...] += pg
        au_ref[...] += pu

    @pl.when(k == pl.num_programs(1) - 1)
    def _():
        a = ag_ref[...]
        o_ref[0] = (a * jax.nn.sigmoid(a) * au_ref[...]).astype(o_ref.dtype)


def expert_ffn_up(xin, w_gate, w_up, layer):
    e, cap, _ = xin.shape
    d, ff = w_gate.shape[2:]
    tk = 1024
    return pl.pallas_call(
        _ffn_up_kernel,
        out_shape=jax.ShapeDtypeStruct((e, cap, ff), BF16),
        grid=(e, d // tk),
        in_specs=[pl.BlockSpec((1, cap, tk), lambda i, k: (i, 0, k)),
                  pl.BlockSpec((None, 1, tk, ff), lambda i, k: (layer, i, k, 0)),
                  pl.BlockSpec((None, 1, tk, ff), lambda i, k: (layer, i, k, 0))],
        out_specs=pl.BlockSpec((1, cap, ff), lambda i, k: (i, 0, 0)),
        scratch_shapes=[pltpu.VMEM((cap, ff), F32), pltpu.VMEM((cap, ff), F32)],
        compiler_params=_cparams(("arbitrary", "arbitrary"), vmem=62 * 1024 * 1024),
        name="expert_ffn_up",
    )(xin, w_gate, w_up)


def _ffn_down_kernel(a_ref, w_ref, g_ref, hi_ref, lo_ref):
    cap = a_ref.shape[1]
    g = g_ref[0]
    lane = lax.broadcasted_iota(I32, g.shape, 1)
    mine = jnp.logical_and(lane % N_EXPERTS == pl.program_id(0), lane < 3 * N_EXPERTS)
    gate = jnp.sum(jnp.where(mine, g, 0.0), axis=1, keepdims=True)
    y = jnp.dot(a_ref[0], w_ref[0].astype(BF16), preferred_element_type=F32) * gate
    hi = y.astype(BF16)
    hi_ref[0, 0:cap, :] = hi
    lo_ref[0, 0:cap, :] = (y - hi.astype(F32)).astype(BF16)
    pad = jnp.zeros((hi_ref.shape[1] - cap, hi_ref.shape[2]), BF16)
    hi_ref[0, cap:, :] = pad
    lo_ref[0, cap:, :] = pad


def expert_ffn_down(act, xin, w_down, layer):
    e, cap, ff = act.shape
    d = w_down.shape[3]
    tn = 512
    out = jax.ShapeDtypeStruct((e, Y_ROWS, d), BF16)
    out_spec = pl.BlockSpec((1, Y_ROWS, tn), lambda i, j: (i, 0, j))
    return pl.pallas_call(
        _ffn_down_kernel,
        out_shape=[out, out],
        grid=(e, d // tn),
        in_specs=[pl.BlockSpec((1, cap, ff), lambda i, j: (i, 0, 0)),
                  pl.BlockSpec((None, 1, ff, tn), lambda i, j: (layer, i, 0, j)),
                  pl.BlockSpec((1, cap, LANES), lambda i, j: (i, 0, D_MODEL // LANES))],
        out_specs=[out_spec, out_spec],
        compiler_params=_cparams(("arbitrary", "arbitrary")),
        name="expert_ffn_down",
    )(act, w_down, xin)


COMBINE_PIECE = 32
COMBINE_ALIGN = 16
COMBINE_BLOCK = 256
COMBINE_PIECES_PER_BLOCK = COMBINE_BLOCK // COMBINE_PIECE
COMBINE_ROWS = N_EXPERTS * ((ROUTE_TILE + COMBINE_ALIGN - 1 + COMBINE_PIECE - 1) // COMBINE_PIECE) * COMBINE_PIECE
COMBINE_BLOCKS = COMBINE_ROWS // COMBINE_BLOCK


def _combine_kernel(s0_ref, cnt_ref, yh_ref, yl_ref, posm_ref, x_ref, gate_ref, lg_ref, lb_ref, *refs, final):
    if final:
        xo_ref, sth_ref, stl_ref, pt_ref, acc_ref, sems = refs
    else:
        sh_ref, sc_ref, xo_ref, h_ref, sth_ref, stl_ref, pt_ref, acc_ref, sems = refs
    t = pl.program_id(0)
    slot = t % 2
    iota = lax.broadcasted_iota(I32, (COMBINE_PIECE, ROUTE_TILE), 0)

    def window_copies(slot, e, w, k):
        r0 = pl.multiple_of(k * COMBINE_PIECE, COMBINE_PIECE)
        sem = sems.at[slot, k // COMBINE_PIECES_PER_BLOCK]
        return (pltpu.make_async_copy(yh_ref.at[e, pl.ds(w, COMBINE_PIECE)],
                                      sth_ref.at[slot, pl.ds(r0, COMBINE_PIECE)], sem),
                pltpu.make_async_copy(yl_ref.at[e, pl.ds(w, COMBINE_PIECE)],
                                      stl_ref.at[slot, pl.ds(r0, COMBINE_PIECE)], sem))

    def walk(tile, slot, fetch):
        kk = jnp.int32(0)
        for e in range(N_EXPERTS):
            n = cnt_ref[tile * N_EXPERTS + e]
            s0 = s0_ref[tile * N_EXPERTS + e]
            a0 = (s0 // COMBINE_ALIGN) * COMBINE_ALIGN
            npc = jnp.where(n > 0, (s0 + n - a0 + COMBINE_PIECE - 1) // COMBINE_PIECE, 0)
            prow = None if fetch else posm_ref[e:e + 1, :]

            def piece(p, carry, e=e, a0=a0, kk=kk, prow=prow):
                w = pl.multiple_of(a0 + p * COMBINE_PIECE, COMBINE_ALIGN)
                k = kk + p
                if fetch:
                    for cp in window_copies(slot, e, w, k):
                        cp.start()
                else:
                    r0 = pl.multiple_of(k * COMBINE_PIECE, COMBINE_PIECE)
                    pt_ref[pl.ds(r0, COMBINE_PIECE), :] = jnp.where(prow == (w + iota), 1.0, 0.0).astype(BF16)
                return carry

            lax.fori_loop(0, npc, piece, 0)
            kk = kk + npc
        return kk

    @pl.when(t == 0)
    def _():
        walk(t, slot, fetch=True)

    @pl.when(t + 1 < pl.num_programs(0))
    def _():
        walk(t + 1, 1 - slot, fetch=True)

    kk = walk(t, slot, fetch=False)
    nblk = (kk + COMBINE_PIECES_PER_BLOCK - 1) // COMBINE_PIECES_PER_BLOCK

    def zero_tail(p, carry):
        r0 = pl.multiple_of((kk + p) * COMBINE_PIECE, COMBINE_PIECE)
        sth_ref[slot, pl.ds(r0, COMBINE_PIECE), :] = jnp.zeros((COMBINE_PIECE, D_MODEL), BF16)
        stl_ref[slot, pl.ds(r0, COMBINE_PIECE), :] = jnp.zeros((COMBINE_PIECE, D_MODEL), BF16)
        pt_ref[pl.ds(r0, COMBINE_PIECE), :] = jnp.zeros((COMBINE_PIECE, ROUTE_TILE), BF16)
        return carry

    lax.fori_loop(0, nblk * COMBINE_PIECES_PER_BLOCK - kk, zero_tail, 0)
    acc_ref[...] = jnp.zeros(acc_ref.shape, F32)

    def block(b, carry):
        def drain(p, c):
            for cp in window_copies(slot, 0, 0, b * COMBINE_PIECES_PER_BLOCK):
                cp.wait()
            return c

        lax.fori_loop(0, jnp.minimum(kk - b * COMBINE_PIECES_PER_BLOCK, COMBINE_PIECES_PER_BLOCK), drain, 0)
        r0 = pl.multiple_of(b * COMBINE_BLOCK, COMBINE_BLOCK)
        onehot = pt_ref[pl.ds(r0, COMBINE_BLOCK), :]
        acc_ref[...] += (_tn_dot(onehot, sth_ref[slot, pl.ds(r0, COMBINE_BLOCK), :])
                         + _tn_dot(onehot, stl_ref[slot, pl.ds(r0, COMBINE_BLOCK), :]))
        return carry

    lax.fori_loop(0, nblk, block, 0)

    z = DN_ALPHA * x_ref[...] + gate_ref[0] * acc_ref[...]
    mu = jnp.mean(z, axis=-1, keepdims=True)
    zc = z - mu
    var = jnp.mean(zc * zc, axis=-1, keepdims=True)
    xn = zc * lax.rsqrt(var + LN_EPS) * lg_ref[...] + lb_ref[...]
    xo_ref[...] = xn
    if not final:
        h_ref[...] = (xn * (1.0 + sc_ref[0]) + sh_ref[0]).astype(h_ref.dtype)


def moe_combine_post_norm(yh, yl, posm_all, s0, cnt, x, gate, ln_g, ln_b, sh=None, sc=None):
    final = sh is None
    rows, n_tiles = (SEQ, N_ROUTE_LAT) if final else (M_ALL, N_ROUTE_TILES)
    tile = lambda width: pl.BlockSpec((ROUTE_TILE, width), lambda t, s0, cnt: (t, 0))
    route = pl.BlockSpec((N_EXPERTS, ROUTE_TILE), lambda t, s0, cnt: (0, t))
    stream_vec = pl.BlockSpec((1, 1, D_MODEL), lambda t, s0, cnt: (jnp.minimum(t // N_ROUTE_LAT, 1), 0, 0))
    vec = pl.BlockSpec((1, D_MODEL), lambda t, s0, cnt: (0, 0))
    in_specs = [pl.BlockSpec(memory_space=pl.ANY), pl.BlockSpec(memory_space=pl.ANY), route,
                tile(D_MODEL), stream_vec, vec, vec]
    args = [s0, cnt, yh, yl, posm_all, x, gate, ln_g.reshape(1, -1), ln_b.reshape(1, -1)]
    out_shape = [jax.ShapeDtypeStruct((rows, D_MODEL), F32)]
    if not final:
        in_specs += [stream_vec, stream_vec]
        args += [sh, sc]
        out_shape.append(jax.ShapeDtypeStruct((rows, D_MODEL), BF16))
    return pl.pallas_call(
        functools.partial(_combine_kernel, final=final),
        out_shape=out_shape,
        grid_spec=pltpu.PrefetchScalarGridSpec(
            num_scalar_prefetch=2,
            grid=(n_tiles,),
            in_specs=in_specs,
            out_specs=[tile(D_MODEL)] * len(out_shape),
            scratch_shapes=[pltpu.VMEM((2, COMBINE_ROWS, D_MODEL), BF16),
                            pltpu.VMEM((2, COMBINE_ROWS, D_MODEL), BF16),
                            pltpu.VMEM((COMBINE_ROWS, ROUTE_TILE), BF16),
                            pltpu.VMEM((ROUTE_TILE, D_MODEL), F32),
                            pltpu.SemaphoreType.DMA((2, COMBINE_BLOCKS))]),
        compiler_params=_cparams(("arbitrary",)),
        name="moe_combine_post_norm",
    )(*args)


def expert_choice_moe_post_norm(h, aff, aff3, w_gate, w_up, w_down, layer, x, gate, ln_g, ln_b, sh=None, sc=None):
    aff_t = aff[:, :N_EXPERTS].T
    posm_lat = expert_choice_select(aff_t[:, :SEQ], CAP_LAT)
    posm_ctx = expert_choice_select(aff_t[:, SEQ:], CAP_CTX)
    posm_all = jnp.concatenate([posm_lat, jnp.where(posm_ctx >= 0, posm_ctx + CAP_LAT, -1)], axis=1)
    s0, cnt = route_tile_meta(posm_all)
    xin = moe_dispatch(h, posm_all, aff3, s0, cnt)
    act = expert_ffn_up(xin, w_gate, w_up, layer)
    yh, yl = expert_ffn_down(act, xin, w_down, layer)
    return moe_combine_post_norm(yh, yl, posm_all, s0, cnt, x, gate, ln_g, ln_b, sh, sc)


def even_mixer(h, tabs, j, w_in, w_dw, b_dw, cn_g, cn_b, qa_g, kva_g, w_uq, w_ukv, w_o):
    u_main = matmul([h], w_in, F32, tn=768, layer=j, n_cols=IN_A_MAIN)
    w_kr = lax.slice(w_in, (j, 0, IN_A_MAIN), (j + 1, D_MODEL, IN_A))[0]
    w_kr = jnp.pad(w_kr, ((0, 0), (0, LANES - MLA_ROPE)))
    kr = matmul([h], w_kr, F32, tn=LANES)
    q_cat, k_cat, vt = mla_projections(u_main, kr, qa_g, kva_g, w_uq, w_ukv, tabs)
    att = mla_attention(q_cat, k_cat, vt)
    att_c = context_attention(q_cat, k_cat, vt, MLA_HEADS, MLA_QK_PAD, MLA_V, 0, 0, 0, 1.0, base2=True,
                              v_transposed=True)
    conv = conformer_conv(u_main, w_dw, b_dw, cn_g, cn_b)
    return matmul([conv, (att, att_c)], w_o, F32, tn=1024, tm=ROW_TILE, layer=j)


def odd_mixer(h, j, w_qkv, rpb, w_o):
    tn = 1024
    q_tiles = NA_HEADS * NA_HEAD_DIM // tn
    qkv = matmul([h], w_qkv, BF16, tn=tn, layer=j, lead_scale=(q_tiles, NA_HEAD_DIM ** -0.5 * LOG2_E))
    o = neighbourhood_attention(qkv, *na_bias_strips(rpb))
    o_c = context_attention(qkv, qkv, qkv, NA_HEADS, NA_HEAD_DIM, NA_HEAD_DIM, 0, NA_HEADS, 2 * NA_HEADS, 1.0,
                            base2=True)
    return matmul([(o, o_c)], w_o, F32, tn=1024, tm=ROW_TILE, layer=j)


def kernel(x, c, ctx, c_ctx, w_ada, b_ada, ln1_g, ln1_b, ln2_g, ln2_b, a_w_in, a_w_dw, a_b_dw, a_cn_g, a_cn_b,
           b_qa_g, b_kva_g, b_w_uq, b_w_ukv, ab_w_o, c_w_qkv, c_rpb, c_w_o, moe_w_router, moe_w_gate, moe_w_up,
           moe_w_down):
    assert x.shape == (1, SEQ, D_MODEL) and ctx.shape == (1, CTX_LEN, D_MODEL)
    xs = jnp.concatenate([x[0], ctx[0]], axis=0)
    cc = jnp.concatenate([c, c_ctx[None], jnp.zeros((SUBLANES - 2, D_MODEL), F32)], axis=0)
    mod = ada_modulation(cc, w_ada, b_ada)

    def vec(layer, k):
        return mod[layer, :2, k * D_MODEL:(k + 1) * D_MODEL].reshape(2, 1, D_MODEL)

    tabs = rope_tables()
    h = modulate_rows(xs, vec(0, 0), vec(0, 1))
    for layer in range(DEPTH):
        j = layer // 2
        if layer % 2 == 0:
            o = even_mixer(h, tabs, j, a_w_in, a_w_dw[j], a_b_dw[j], a_cn_g[j], a_cn_b[j], b_qa_g[j], b_kva_g[j],
                           b_w_uq[j], b_w_ukv[j], ab_w_o)
        else:
            o = odd_mixer(h, j, c_w_qkv, c_rpb[j], c_w_o)
        xs, h2, aff, aff3 = post_norm_rows(xs, o, vec(layer, 2), ln1_g[layer], ln1_b[layer], vec(layer, 3), vec(layer, 4),
                                     moe_w_router[layer])
        moe = functools.partial(expert_choice_moe_post_norm, h2, aff, aff3, moe_w_gate, moe_w_up, moe_w_down, layer,
                                xs, vec(layer, 5), ln2_g[layer], ln2_b[layer])
        if layer + 1 < DEPTH:
            xs, h = moe(vec(layer + 1, 0), vec(layer + 1, 1))
        else:
            (xs,) = moe()
    return xs[None]
```

```python
import functools

import numpy as np
import jax
import jax.numpy as jnp
from jax import lax
from jax.experimental import pallas as pl
from jax.experimental.pallas import tpu as pltpu

F32 = jnp.float32
BF16 = jnp.bfloat16
I32 = jnp.int32

D_MODEL = 2048
SEQ = 8192
DEPTH = 4
GRID_W = 64
CTX_LEN = 256
M_ALL = SEQ + CTX_LEN

CONV_CH = 1024
CONV_WIDTH = 31
CONV_PAD = CONV_WIDTH // 2
MLA_HEADS = 8
MLA_Q_RANK = 512
MLA_KV_RANK = 512
MLA_NOPE = 128
MLA_ROPE = 64
MLA_V = 128
MLA_QK_PAD = 256
ROPE_BASE = 10000.0
NA_HEADS = 16
NA_HEAD_DIM = 128
NA_WIN_ROWS = 8
NA_WIN_COLS = 16
N_EXPERTS = 16
EXPERT_FF = 1408
EC_CAPACITY_FACTOR = 2
CAP_LAT = EC_CAPACITY_FACTOR * SEQ // N_EXPERTS
CAP_CTX = EC_CAPACITY_FACTOR * CTX_LEN // N_EXPERTS
CAP_ALL = CAP_LAT + CAP_CTX
IN_A = 2 * CONV_CH + MLA_Q_RANK + MLA_KV_RANK + MLA_ROPE
IN_A_MAIN = IN_A - MLA_ROPE
LOG2_E = 1.4426950408889634
DN_ALPHA = (2 * DEPTH) ** 0.25
LN_EPS = 1e-5
RMS_EPS = 1e-6
NEG_INF = -1e30

LANES = 128
SUBLANES = 8
VMEM_LIMIT = 56 * 1024 * 1024

ROW_TILE = 256
N_LAT_TILES = SEQ // ROW_TILE
N_ROW_TILES = M_ALL // ROW_TILE
MM_ROW_TILE = 768
ROUTE_TILE = 128
N_ROUTE_TILES = M_ALL // ROUTE_TILE
N_ROUTE_LAT = SEQ // ROUTE_TILE
Y_ROWS = CAP_ALL + 32


def _cparams(sem, vmem=VMEM_LIMIT):
    return pltpu.CompilerParams(dimension_semantics=sem, vmem_limit_bytes=vmem)


def _nt_dot(a, b):
    return lax.dot_general(a, b, (((1,), (1,)), ((), ())), preferred_element_type=F32)


def _tn_dot(a, b):
    return lax.dot_general(a, b, (((0,), (0,)), ((), ())), preferred_element_type=F32)


def _ada_kernel(c_ref, w_ref, b_ref, o_ref):
    c = c_ref[...]
    a = (c * jax.nn.sigmoid(c)).astype(BF16)
    o_ref[0] = jnp.dot(a, w_ref[0].astype(BF16), preferred_element_type=F32) + b_ref[0]


def ada_modulation(cc, w_ada, b_ada):
    depth, d, n = w_ada.shape
    tn = 1024
    return pl.pallas_call(
        _ada_kernel,
        out_shape=jax.ShapeDtypeStruct((depth, SUBLANES, n), F32),
        grid=(depth, n // tn),
        in_specs=[
            pl.BlockSpec((SUBLANES, d), lambda l, j: (0, 0)),
            pl.BlockSpec((1, d, tn), lambda l, j: (l, 0, j)),
            pl.BlockSpec((1, 1, tn), lambda l, j: (l, 0, j)),
        ],
        out_specs=pl.BlockSpec((1, SUBLANES, tn), lambda l, j: (l, 0, j)),
        compiler_params=_cparams(("arbitrary", "arbitrary")),
        name="ada_modulation",
    )(cc, w_ada, b_ada.reshape(depth, 1, n))


def _stream_vec_spec():
    return pl.BlockSpec((1, 1, D_MODEL), lambda i: (jnp.minimum(i // N_LAT_TILES, 1), 0, 0))


def _row_spec(width=D_MODEL):
    return pl.BlockSpec((ROW_TILE, width), lambda i: (i, 0))


def _stream_pair_specs(width=D_MODEL):
    return [pl.BlockSpec((ROW_TILE, width), lambda i: (jnp.minimum(i, N_LAT_TILES - 1), 0)),
            pl.BlockSpec((ROW_TILE, width), lambda i: (jnp.maximum(i - N_LAT_TILES, 0), 0))]


def _stream_pair_tile(lat_ref, ctx_ref):
    return jnp.where(pl.program_id(0) < N_LAT_TILES, lat_ref[...], ctx_ref[...])


def _modulate_kernel(xl_ref, xc_ref, sh_ref, sc_ref, h_ref):
    h_ref[...] = (_stream_pair_tile(xl_ref, xc_ref) * (1.0 + sc_ref[0]) + sh_ref[0]).astype(h_ref.dtype)


def modulate_rows(x_lat, x_ctx, sh, sc):
    return pl.pallas_call(
        _modulate_kernel,
        out_shape=jax.ShapeDtypeStruct((M_ALL, D_MODEL), BF16),
        grid=(N_ROW_TILES,),
        in_specs=_stream_pair_specs() + [_stream_vec_spec(), _stream_vec_spec()],
        out_specs=_row_spec(),
        compiler_params=_cparams(("arbitrary",)),
        name="modulate_rows",
    )(x_lat, x_ctx, sh, sc)


def _post_norm_kernel(*refs, with_router, x_paired):
    if x_paired:
        x = _stream_pair_tile(refs[0], refs[1])
        refs = refs[2:]
    else:
        x = refs[0][...]
        refs = refs[1:]
    if with_router:
        y_ref, gate_ref, lg_ref, lb_ref, sh_ref, sc_ref, wr_ref, xo_ref, h_ref, aff_ref, aff3_ref = refs
    else:
        y_ref, gate_ref, lg_ref, lb_ref, sh_ref, sc_ref, xo_ref, h_ref = refs
    z = DN_ALPHA * x + gate_ref[0] * y_ref[...]
    mu = jnp.mean(z, axis=-1, keepdims=True)
    zc = z - mu
    var = jnp.mean(zc * zc, axis=-1, keepdims=True)
    xn = zc * lax.rsqrt(var + LN_EPS) * lg_ref[...] + lb_ref[...]
    xo_ref[...] = xn
    h = (xn * (1.0 + sc_ref[0]) + sh_ref[0]).astype(BF16)
    h_ref[...] = h
    if with_router:
        logits = jnp.dot(h, wr_ref[...].astype(BF16), preferred_element_type=F32)
        lane = lax.broadcasted_iota(I32, logits.shape, 1)
        logits = jnp.where(lane < N_EXPERTS, logits, NEG_INF)
        e = jnp.exp(logits - jnp.max(logits, axis=-1, keepdims=True))
        aff = e / jnp.sum(e, axis=-1, keepdims=True)
        aff_ref[...] = aff
        hi = aff.astype(BF16).astype(F32)
        mid = (aff - hi).astype(BF16).astype(F32)
        lo = (aff - hi - mid).astype(BF16).astype(F32)
        aff3_ref[...] = (hi + pltpu.roll(mid, N_EXPERTS, 1) + pltpu.roll(lo, 2 * N_EXPERTS, 1)).astype(BF16)


def post_norm_rows(x, y, gate, ln_g, ln_b, sh, sc, w_router=None):
    with_router = w_router is not None
    x_paired = isinstance(x, tuple)
    vec = pl.BlockSpec((1, D_MODEL), lambda i: (0, 0))
    in_specs = (_stream_pair_specs() if x_paired else [_row_spec()]) + [
        _row_spec(), _stream_vec_spec(), vec, vec, _stream_vec_spec(), _stream_vec_spec()]
    out_shape = [jax.ShapeDtypeStruct(y.shape, F32), jax.ShapeDtypeStruct(y.shape, BF16)]
    out_specs = [_row_spec(), _row_spec()]
    args = (list(x) if x_paired else [x]) + [y, gate, ln_g.reshape(1, -1), ln_b.reshape(1, -1), sh, sc]
    if with_router:
        in_specs.append(pl.BlockSpec((D_MODEL, LANES), lambda i: (0, 0)))
        out_shape += [jax.ShapeDtypeStruct((y.shape[0], LANES), F32), jax.ShapeDtypeStruct((y.shape[0], LANES), BF16)]
        out_specs += [_row_spec(LANES), _row_spec(LANES)]
        args.append(jnp.pad(w_router, ((0, 0), (0, LANES - N_EXPERTS))))
    return pl.pallas_call(
        functools.partial(_post_norm_kernel, with_router=with_router, x_paired=x_paired),
        out_shape=out_shape,
        grid=(N_ROW_TILES,),
        in_specs=in_specs,
        out_specs=out_specs,
        compiler_params=_cparams(("arbitrary",)),
        name="post_norm_router" if with_router else "post_norm",
    )(*args)


def _matmul_kernel(*refs, splits, paired, lead_scale):
    n_refs = len(splits) + sum(paired)
    a_refs, w_ref, o_ref, wb_ref = list(refs[:n_refs]), refs[n_refs], refs[n_refs + 1], refs[n_refs + 2]

    @pl.when(pl.program_id(1) == 0)
    def _():
        wb_ref[...] = w_ref[...].astype(BF16)

    acc = None
    off = 0
    for k, pair in zip(splits, paired):
        a = a_refs.pop(0)[...]
        if pair:
            a = jnp.where(pl.program_id(1) < N_LAT_TILES, a, a_refs.pop(0)[...])
        part = jnp.dot(a.astype(BF16), wb_ref[off:off + k, :], preferred_element_type=F32)
        acc = part if acc is None else acc + part
        off += k
    if lead_scale is not None:
        n_tiles, value = lead_scale
        acc = acc * jnp.where(pl.program_id(0) < n_tiles, value, 1.0)
    o_ref[...] = acc.astype(o_ref.dtype)


def matmul(a_list, w, out_dtype, tn, tm=MM_ROW_TILE, layer=None, n_cols=None, lead_scale=None):
    paired = tuple(isinstance(a, tuple) for a in a_list)
    splits = tuple(a[0].shape[1] if p else a.shape[1] for a, p in zip(a_list, paired))
    m = M_ALL if any(paired) else a_list[0].shape[0]
    k, n = w.shape[-2:]
    n = n if n_cols is None else n_cols
    assert sum(splits) == k and n % tn == 0 and m % tm == 0 and (tm == ROW_TILE or not any(paired))
    in_specs, args = [], []
    for a, ki, p in zip(a_list, splits, paired):
        if p:
            in_specs += [pl.BlockSpec((tm, ki), lambda j, i: (jnp.minimum(i, N_LAT_TILES - 1), 0)),
                         pl.BlockSpec((tm, ki), lambda j, i: (0, 0))]
            args += list(a)
        else:
            in_specs.append(pl.BlockSpec((tm, ki), lambda j, i: (i, 0)))
            args.append(a)
    if layer is None:
        in_specs.append(pl.BlockSpec((k, tn), lambda j, i: (0, j)))
    else:
        in_specs.append(pl.BlockSpec((None, k, tn), lambda j, i: (layer, 0, j)))
    return pl.pallas_call(
        functools.partial(_matmul_kernel, splits=splits, paired=paired, lead_scale=lead_scale),
        out_shape=jax.ShapeDtypeStruct((m, n), out_dtype),
        grid=(n // tn, m // tm),
        in_specs=in_specs,
        out_specs=pl.BlockSpec((tm, tn), lambda j, i: (i, j)),
        scratch_shapes=[pltpu.VMEM((k, tn), BF16)],
        compiler_params=_cparams(("arbitrary", "arbitrary")),
        name="matmul",
    )(*args, w)


def rope_tables():
    t = jnp.arange(SEQ, dtype=I32)
    row = (t // GRID_W).astype(F32)
    col = (t % GRID_W).astype(F32)
    n_freq = MLA_ROPE // 4
    inv_freq = ROPE_BASE ** (-jnp.arange(n_freq, dtype=F32) / n_freq)
    ang = jnp.concatenate([row[:, None] * inv_freq, col[:, None] * inv_freq], axis=-1)
    cos, sin = jnp.cos(ang), jnp.sin(ang)
    half = MLA_ROPE // 2
    cos = jnp.concatenate([cos, jnp.ones((CTX_LEN, half), F32)], axis=0)
    sin = jnp.concatenate([sin, jnp.zeros((CTX_LEN, half), F32)], axis=0)
    z = jnp.zeros_like(cos)
    c_tab = jnp.concatenate([cos, cos, z, z], axis=1)
    sa_tab = jnp.concatenate([z, sin, z, z], axis=1)
    sb_tab = jnp.concatenate([-sin, z, z, z], axis=1)
    return c_tab, sa_tab, sb_tab


def _mla_proj_kernel(cq_ref, ckv_ref, kr_ref, qg_ref, kvg_ref, wq_ref, wkv_ref, c_ref, sa_ref, sb_ref,
                     q_out, k_out, vt_out, wqb, wkvb):
    @pl.when(pl.program_id(0) == 0)
    def _():
        wqb[...] = wq_ref[...].astype(BF16)
        wkvb[...] = wkv_ref[...].astype(BF16)

    def rms(x, g):
        return x * lax.rsqrt(jnp.mean(x * x, axis=-1, keepdims=True) + RMS_EPS) * g

    c_tab, sa_tab, sb_tab = c_ref[...], sa_ref[...], sb_ref[...]
    half = MLA_ROPE // 2

    def rope(g):
        return g * c_tab + pltpu.roll(g, half, 1) * sa_tab + pltpu.roll(g, LANES - half, 1) * sb_tab

    qk_scale = (MLA_NOPE + MLA_ROPE) ** -0.5 * LOG2_E
    q = jnp.dot(rms(cq_ref[...], qg_ref[...]).astype(BF16), wqb[...], preferred_element_type=F32) * qk_scale
    kv = jnp.dot(rms(ckv_ref[...], kvg_ref[...]).astype(BF16), wkvb[...], preferred_element_type=F32)
    k_rope = rope(kr_ref[...]).astype(BF16)
    for h in range(MLA_HEADS):
        lo = h * MLA_QK_PAD
        q_out[:, lo:lo + LANES] = q[:, lo:lo + LANES].astype(BF16)
        q_out[:, lo + LANES:lo + 2 * LANES] = rope(q[:, lo + LANES:lo + 2 * LANES]).astype(BF16)
        k_out[:, lo:lo + LANES] = kv[:, h * MLA_NOPE:(h + 1) * MLA_NOPE].astype(BF16)
        k_out[:, lo + LANES:lo + 2 * LANES] = k_rope
    vt_out[...] = kv[:, MLA_HEADS * MLA_NOPE:].T.astype(BF16)


def mla_projections(u_main, kr, qa_g, kva_g, w_uq, w_ukv, tabs):
    m = u_main.shape[0]
    tm = 384
    hq = MLA_HEADS * MLA_QK_PAD
    wq = w_uq.reshape(MLA_Q_RANK, MLA_HEADS, MLA_NOPE + MLA_ROPE)
    wq = jnp.pad(wq, ((0, 0), (0, 0), (0, MLA_QK_PAD - MLA_NOPE - MLA_ROPE))).reshape(MLA_Q_RANK, hq)
    wkv = w_ukv.reshape(MLA_KV_RANK, MLA_HEADS, MLA_NOPE + MLA_V)
    wkv = jnp.concatenate([wkv[..., :MLA_NOPE].reshape(MLA_KV_RANK, -1), wkv[..., MLA_NOPE:].reshape(MLA_KV_RANK, -1)], axis=1)
    nkv = wkv.shape[1]
    cq_blk = 2 * CONV_CH // MLA_Q_RANK
    tab_spec = pl.BlockSpec((tm, LANES), lambda i: (i, 0))
    return pl.pallas_call(
        _mla_proj_kernel,
        out_shape=[jax.ShapeDtypeStruct((m, hq), BF16), jax.ShapeDtypeStruct((m, hq), BF16),
                   jax.ShapeDtypeStruct((MLA_HEADS * MLA_V, m), BF16)],
        grid=(m // tm,),
        in_specs=[
            pl.BlockSpec((tm, MLA_Q_RANK), lambda i: (i, cq_blk)),
            pl.BlockSpec((tm, MLA_KV_RANK), lambda i: (i, cq_blk + 1)),
            pl.BlockSpec((tm, LANES), lambda i: (i, 0)),
            pl.BlockSpec((1, MLA_Q_RANK), lambda i: (0, 0)),
            pl.BlockSpec((1, MLA_KV_RANK), lambda i: (0, 0)),
            pl.BlockSpec((MLA_Q_RANK, hq), lambda i: (0, 0)),
            pl.BlockSpec((MLA_KV_RANK, nkv), lambda i: (0, 0)),
            tab_spec, tab_spec, tab_spec,
        ],
        out_specs=[pl.BlockSpec((tm, hq), lambda i: (i, 0)), pl.BlockSpec((tm, hq), lambda i: (i, 0)),
                   pl.BlockSpec((MLA_HEADS * MLA_V, tm), lambda i: (0, i))],
        scratch_shapes=[pltpu.VMEM((MLA_Q_RANK, hq), BF16), pltpu.VMEM((MLA_KV_RANK, nkv), BF16)],
        compiler_params=_cparams(("arbitrary",)),
        name="mla_projections",
    )(u_main, u_main, kr, qa_g.reshape(1, -1), kva_g.reshape(1, -1), wq, wkv, *tabs)


MLA_KEY_CHUNK = 1408
MLA_Q_TILE = 1024


def _flash_kernel(q_ref, k_ref, vt_ref, o_ref, *, tk, n_chunks):
    q = q_ref[...]
    tq = q.shape[0]
    m = jnp.full((1, tq), NEG_INF, F32)
    l = jnp.zeros((1, tq), F32)
    acc = jnp.zeros((vt_ref.shape[0], tq), F32)
    s_next = _nt_dot(k_ref[0:tk, :], q)
    for c in range(n_chunks):
        s = s_next
        if c + 1 < n_chunks:
            s_next = _nt_dot(k_ref[(c + 1) * tk:(c + 2) * tk, :], q)
        m_new = jnp.maximum(m, jnp.max(s, axis=0, keepdims=True))
        alpha = jnp.exp2(m - m_new)
        p = jnp.exp2(s - m_new)
        l = alpha * l + jnp.sum(p, axis=0, keepdims=True)
        acc = alpha * acc + jnp.dot(vt_ref[:, c * tk:(c + 1) * tk], p.astype(BF16), preferred_element_type=F32)
        m = m_new
    o_ref[...] = (acc / l).T.astype(o_ref.dtype)


def mla_attention(q_cat, k_cat, vt):
    m = k_cat.shape[0]
    tq, tk = MLA_Q_TILE, MLA_KEY_CHUNK
    return pl.pallas_call(
        functools.partial(_flash_kernel, tk=tk, n_chunks=m // tk),
        out_shape=jax.ShapeDtypeStruct((SEQ, MLA_HEADS * MLA_V), BF16),
        grid=(MLA_HEADS, SEQ // tq),
        in_specs=[
            pl.BlockSpec((tq, MLA_QK_PAD), lambda h, i: (i, h)),
            pl.BlockSpec((m, MLA_QK_PAD), lambda h, i: (0, h)),
            pl.BlockSpec((MLA_V, m), lambda h, i: (h, 0)),
        ],
        out_specs=pl.BlockSpec((tq, MLA_V), lambda h, i: (i, h)),
        compiler_params=_cparams(("arbitrary", "arbitrary")),
        name="mla_attention",
    )(q_cat, k_cat, vt)


def _ctx_attn_kernel(q_ref, k_ref, v_ref, o_ref, *, scale, base2, v_transposed):
    s = _nt_dot(q_ref[...], k_ref[...]) * scale
    z = s - jnp.max(s, axis=-1, keepdims=True)
    p = jnp.exp2(z) if base2 else jnp.exp(z)
    l = jnp.sum(p, axis=-1, keepdims=True)
    pv = _nt_dot(p.astype(BF16), v_ref[...]) if v_transposed else jnp.dot(p.astype(BF16), v_ref[...],
                                                                         preferred_element_type=F32)
    o_ref[...] = (pv / l).astype(o_ref.dtype)


def context_attention(q_arr, k_arr, v_arr, heads, dq, dv, q_col, k_col, v_col, scale, base2=False,
                      v_transposed=False):
    rb = SEQ // CTX_LEN
    if v_transposed:
        v_spec = pl.BlockSpec((dv, CTX_LEN), lambda h: (v_col + h, rb))
    else:
        v_spec = pl.BlockSpec((CTX_LEN, dv), lambda h: (rb, v_col + h))
    return pl.pallas_call(
        functools.partial(_ctx_attn_kernel, scale=scale, base2=base2, v_transposed=v_transposed),
        out_shape=jax.ShapeDtypeStruct((CTX_LEN, heads * dv), BF16),
        grid=(heads,),
        in_specs=[
            pl.BlockSpec((CTX_LEN, dq), lambda h: (rb, q_col + h)),
            pl.BlockSpec((CTX_LEN, dq), lambda h: (rb, k_col + h)),
            v_spec,
        ],
        out_specs=pl.BlockSpec((CTX_LEN, dv), lambda h: (0, h)),
        compiler_params=_cparams(("arbitrary",)),
        name="context_attention",
    )(q_arr, k_arr, v_arr)


CONV_HALO = 16
CONV_ROW_BLOCK = 64


def _conv_kernel(a_ref, g_ref, ap_ref, gp_ref, an_ref, gn_ref, w_ref, b_ref, cg_ref, cb_ref, o_ref, hbuf, hshift,
                 cbuf):
    i = pl.program_id(0)
    tl = a_ref.shape[0]

    def glu(a, g):
        return a[...] * jax.nn.sigmoid(g[...])

    has_prev = jnp.logical_and(i != 0, i != N_LAT_TILES)
    has_next = jnp.logical_and(i != N_LAT_TILES - 1, i != N_ROW_TILES - 1)
    hbuf[0:CONV_HALO, :] = jnp.where(has_prev, glu(ap_ref, gp_ref), 0.0)
    hbuf[CONV_HALO:CONV_HALO + tl, :] = glu(a_ref, g_ref)
    hbuf[CONV_HALO + tl:, :] = jnp.where(has_next, glu(an_ref, gn_ref), 0.0)

    span = hshift.shape[1]
    for s in range(SUBLANES):
        hshift[s] = hbuf[s:s + span, :]

    base = CONV_HALO - CONV_PAD
    for rb in range(tl // CONV_ROW_BLOCK):
        r0 = rb * CONV_ROW_BLOCK
        for c in range(CONV_CH // LANES):
            cs = slice(c * LANES, (c + 1) * LANES)
            acc = jnp.broadcast_to(b_ref[:, cs], (CONV_ROW_BLOCK, LANES))
            for j in range(CONV_WIDTH):
                s = (base + j) % SUBLANES
                a = r0 + base + j - s
                acc = acc + hshift[s, a:a + CONV_ROW_BLOCK, cs] * w_ref[j:j + 1, cs]
            cbuf[r0:r0 + CONV_ROW_BLOCK, cs] = acc

    y = cbuf[...]
    mu = jnp.mean(y, axis=-1, keepdims=True)
    yc = y - mu
    var = jnp.mean(yc * yc, axis=-1, keepdims=True)
    yn = yc * lax.rsqrt(var + LN_EPS) * cg_ref[...] + cb_ref[...]
    o_ref[...] = (yn * jax.nn.sigmoid(yn)).astype(o_ref.dtype)


def conformer_conv(u_main, w_dw, b_dw, cn_g, cn_b):
    m = u_main.shape[0]
    tl = ROW_TILE
    hpt = tl // CONV_HALO
    n_halo = m // CONV_HALO
    main = lambda col: pl.BlockSpec((tl, CONV_CH), lambda i: (i, col))
    prev = lambda col: pl.BlockSpec((CONV_HALO, CONV_CH), lambda i: (jnp.maximum(i * hpt - 1, 0), col))
    nxt = lambda col: pl.BlockSpec((CONV_HALO, CONV_CH), lambda i: (jnp.minimum((i + 1) * hpt, n_halo - 1), col))
    vec = pl.BlockSpec((1, CONV_CH), lambda i: (0, 0))
    return pl.pallas_call(
        _conv_kernel,
        out_shape=jax.ShapeDtypeStruct((m, CONV_CH), BF16),
        grid=(m // tl,),
        in_specs=[main(0), main(1), prev(0), prev(1), nxt(0), nxt(1),
                  pl.BlockSpec((CONV_WIDTH, CONV_CH), lambda i: (0, 0)), vec, vec, vec],
        out_specs=pl.BlockSpec((tl, CONV_CH), lambda i: (i, 0)),
        scratch_shapes=[pltpu.VMEM((tl + 2 * CONV_HALO, CONV_CH), F32),
                        pltpu.VMEM((SUBLANES, tl + 2 * CONV_HALO - SUBLANES, CONV_CH), F32),
                        pltpu.VMEM((tl, CONV_CH), F32)],
        compiler_params=_cparams(("arbitrary",)),
        name="conformer_conv",
    )(u_main, u_main, u_main, u_main, u_main, u_main, w_dw, b_dw.reshape(1, -1), cn_g.reshape(1, -1), cn_b.reshape(1, -1))


NA_Q_ROWS = 8
NA_K_ROWS = 16
NA_KBLK = 4 * GRID_W
NA_QTOK = NA_Q_ROWS * GRID_W
NA_KTOK = NA_K_ROWS * GRID_W
NA_GROUPS = SEQ // NA_QTOK


NA_HEADS_PER_STEP = 2
NA_CLASSES = (0, 1, NA_GROUPS - 1)


def _na_window_rows():
    rows = SEQ // GRID_W
    out = []
    for g in NA_CLASSES:
        qr = (NA_Q_ROWS * g + np.arange(NA_Q_ROWS))[:, None]
        kr = (NA_Q_ROWS * g - (NA_K_ROWS - NA_Q_ROWS) // 2 + np.arange(NA_K_ROWS))[None, :]
        start = np.clip(qr - NA_WIN_ROWS // 2, 0, rows - NA_WIN_ROWS)
        out.append((kr >= start) & (kr < start + NA_WIN_ROWS))
    return np.stack(out)


def na_bias_strips(rpb):
    qc = np.arange(GRID_W)[:, None]
    kc = np.arange(GRID_W)[None, :]
    ws = np.clip(qc - NA_WIN_COLS // 2, 0, GRID_W - NA_WIN_COLS)
    col_valid = (kc >= ws) & (kc < ws + NA_WIN_COLS)
    col_idx = np.clip(kc - qc + NA_WIN_COLS - 1, 0, 2 * NA_WIN_COLS - 2)
    onehot = jnp.asarray((col_idx[None] == np.arange(2 * NA_WIN_COLS - 1)[:, None, None]).astype(np.float32))
    cm = jnp.einsum('hrd,dqk->hrqk', rpb, onehot, precision=lax.Precision.HIGHEST) * LOG2_E
    cm = jnp.where(col_valid[None, None], cm, NEG_INF)
    neg = jnp.full_like(cm, NEG_INF)
    return jnp.concatenate([cm, neg], axis=-1), jnp.concatenate([neg, cm], axis=-1)


def _na_kernel(q_ref, k0, k1, k2, k3, v0, v1, v2, v3, kc_ref, vc_ref, bl_ref, br_ref, o_ref, tbl_ref):
    g = pl.program_id(1)
    row_valid = _na_window_rows()

    @pl.when(g == 0)
    def _():
        neg_tile = jnp.full((GRID_W, LANES), NEG_INF, F32)
        for hh in range(NA_HEADS_PER_STEP):
            for c in range(len(NA_CLASSES)):
                for ql in range(NA_Q_ROWS):
                    for kp in range(NA_K_ROWS // 2):
                        ka, kb = 2 * kp, 2 * kp + 1
                        da = ka - ql + NA_WIN_ROWS - 1 - (NA_K_ROWS - NA_Q_ROWS) // 2
                        ta = bl_ref[hh, da] if row_valid[c, ql, ka] else neg_tile
                        tb = br_ref[hh, da + 1] if row_valid[c, ql, kb] else neg_tile
                        tbl_ref[hh, c, ql * GRID_W:(ql + 1) * GRID_W, kp * LANES:(kp + 1) * LANES] = jnp.maximum(ta, tb)

    cls = jnp.where(g == 0, 0, jnp.where(g == NA_GROUPS - 1, 2, 1))
    for hh in range(NA_HEADS_PER_STEP):
        hs = slice(hh * NA_HEAD_DIM, (hh + 1) * NA_HEAD_DIM)
        q = q_ref[:, hs]
        s = jnp.concatenate([_nt_dot(q, k[:, hs]) for k in (k0, k1, k2, k3)], axis=1) + tbl_ref[hh, cls]
        sc = _nt_dot(q, kc_ref[:, hs])
        m = jnp.maximum(jnp.max(s, axis=-1, keepdims=True), jnp.max(sc, axis=-1, keepdims=True))
        p = jnp.exp2(s - m)
        pc = jnp.exp2(sc - m)
        l = jnp.sum(p, axis=-1, keepdims=True) + jnp.sum(pc, axis=-1, keepdims=True)
        o = jnp.dot(pc.astype(BF16), vc_ref[:, hs], preferred_element_type=F32)
        for j, v in enumerate((v0, v1, v2, v3)):
            o = o + jnp.dot(p[:, j * NA_KBLK:(j + 1) * NA_KBLK].astype(BF16), v[:, hs], preferred_element_type=F32)
        o_ref[:, hs] = (o / l).astype(o_ref.dtype)


def neighbourhood_attention(qkv, bias_left, bias_right):
    n_kblk = SEQ // NA_KBLK
    ctx_blk = SEQ // CTX_LEN
    hp = NA_HEADS_PER_STEP
    width = hp * NA_HEAD_DIM
    n_pairs = NA_HEADS // hp

    def kv_spec(j, col0):
        return pl.BlockSpec((NA_KBLK, width), lambda h, g: (jnp.clip(2 * g - 1 + j, 0, n_kblk - 1), col0 + h))

    strip_spec = pl.BlockSpec((hp, 2 * NA_WIN_ROWS - 1, GRID_W, LANES), lambda h, g: (h, 0, 0, 0))
    return pl.pallas_call(
        _na_kernel,
        out_shape=jax.ShapeDtypeStruct((SEQ, NA_HEADS * NA_HEAD_DIM), BF16),
        grid=(n_pairs, NA_GROUPS),
        in_specs=[pl.BlockSpec((NA_QTOK, width), lambda h, g: (g, h))]
        + [kv_spec(j, n_pairs) for j in range(4)]
        + [kv_spec(j, 2 * n_pairs) for j in range(4)]
        + [pl.BlockSpec((CTX_LEN, width), lambda h, g: (ctx_blk, n_pairs + h)),
           pl.BlockSpec((CTX_LEN, width), lambda h, g: (ctx_blk, 2 * n_pairs + h)),
           strip_spec, strip_spec],
        out_specs=pl.BlockSpec((NA_QTOK, width), lambda h, g: (g, h)),
        scratch_shapes=[pltpu.VMEM((hp, len(NA_CLASSES), NA_QTOK, NA_KTOK), F32)],
        compiler_params=_cparams(("arbitrary", "arbitrary")),
        name="neighbourhood_attention",
    )(*([qkv] * 11), bias_left, bias_right)


def _select_kernel(aff_ref, posm_ref, *, cap):
    bits = pltpu.bitcast(aff_ref[...], I32)
    n = bits.shape[1]

    def search(i, thr):
        cand = thr | (jnp.int32(1) << (30 - i))
        cnt = jnp.sum((bits >= cand).astype(I32), axis=1, keepdims=True)
        return jnp.where(cnt >= cap, cand, thr)

    thr = lax.fori_loop(0, 31, search, jnp.zeros((N_EXPERTS, 1), I32))
    gt = bits > thr
    eq = bits == thr
    need = (cap - jnp.sum(gt.astype(I32), axis=1, keepdims=True)).astype(F32)
    tri = jnp.where(lax.broadcasted_iota(I32, (LANES, LANES), 0) <= lax.broadcasted_iota(I32, (LANES, LANES), 1),
                    1.0, 0.0).astype(BF16)
    off_eq = jnp.zeros((N_EXPERTS, 1), F32)
    off_sel = jnp.zeros((N_EXPERTS, 1), F32)
    for b in range(n // LANES):
        cs = slice(b * LANES, (b + 1) * LANES)
        eq_b = jnp.where(eq[:, cs], 1.0, 0.0)
        inc_eq = jnp.dot(eq_b.astype(BF16), tri, preferred_element_type=F32)
        rank = off_eq + inc_eq - eq_b
        sel_b = jnp.logical_or(gt[:, cs], jnp.logical_and(eq[:, cs], rank < need))
        sel_f = jnp.where(sel_b, 1.0, 0.0)
        inc_sel = jnp.dot(sel_f.astype(BF16), tri, preferred_element_type=F32)
        pos = off_sel + inc_sel - sel_f
        posm_ref[:, cs] = jnp.where(sel_b, pos.astype(I32), -1)
        off_eq = off_eq + inc_eq[:, LANES - 1:LANES]
        off_sel = off_sel + inc_sel[:, LANES - 1:LANES]


def expert_choice_select(aff_t, cap):
    return pl.pallas_call(
        functools.partial(_select_kernel, cap=cap),
        out_shape=jax.ShapeDtypeStruct(aff_t.shape, I32),
        compiler_params=_cparams(None),
        name="expert_choice_select",
    )(aff_t)


def route_tile_meta(posm_all):
    cnt = (posm_all >= 0).astype(I32).reshape(N_EXPERTS, N_ROUTE_TILES, ROUTE_TILE).sum(-1).T
    lat, ctx = cnt[:N_ROUTE_LAT], cnt[N_ROUTE_LAT:]
    s0 = jnp.concatenate([jnp.cumsum(lat, 0) - lat, CAP_LAT + jnp.cumsum(ctx, 0) - ctx], axis=0)
    return s0.reshape(-1), cnt.reshape(-1)


DISPATCH_PIECE = 16
DISPATCH_BLOCK = 256
DISPATCH_ROWS = N_EXPERTS * ((ROUTE_TILE + SUBLANES - 1 + DISPATCH_PIECE - 1) // DISPATCH_PIECE) * DISPATCH_PIECE


XIN_WIDTH = D_MODEL + LANES


def _dispatch_kernel(s0_ref, cnt_ref, h_ref, posm_ref, aff3_ref, xin_ref, pt_ref, x_ref, carry_ref, sem):
    t = pl.program_id(0)

    @pl.when(t == 0)
    def _():
        carry_ref[...] = jnp.zeros(carry_ref.shape, F32)

    pt_ref[...] = jnp.zeros(pt_ref.shape, pt_ref.dtype)
    iota = lax.broadcasted_iota(I32, (DISPATCH_PIECE, ROUTE_TILE), 0)
    offs = []
    o = jnp.int32(0)
    for e in range(N_EXPERTS):
        n = cnt_ref[t * N_EXPERTS + e]
        s0 = s0_ref[t * N_EXPERTS + e]
        a0 = (s0 // SUBLANES) * SUBLANES
        npc = jnp.where(n > 0, (s0 + n - a0 + DISPATCH_PIECE - 1) // DISPATCH_PIECE, 0)
        row = posm_ref[e:e + 1, :]
        offs.append(o)

        def piece(p, carry, row=row, a0=a0, o=o):
            r0 = pl.multiple_of(o + p * DISPATCH_PIECE, DISPATCH_PIECE)
            hit = row == (a0 + p * DISPATCH_PIECE + iota)
            pt_ref[pl.ds(r0, DISPATCH_PIECE), :] = jnp.where(hit, 1.0, 0.0).astype(BF16)
            return carry

        lax.fori_loop(0, npc, piece, 0)
        o = o + npc * DISPATCH_PIECE

    def block(b, carry):
        r0 = pl.multiple_of(b * DISPATCH_BLOCK, DISPATCH_BLOCK)
        onehot = pt_ref[pl.ds(r0, DISPATCH_BLOCK), :]
        x_ref[pl.ds(r0, DISPATCH_BLOCK), 0:D_MODEL] = jnp.dot(onehot, h_ref[...], preferred_element_type=F32)
        x_ref[pl.ds(r0, DISPATCH_BLOCK), D_MODEL:] = jnp.dot(onehot, aff3_ref[...], preferred_element_type=F32)
        return carry

    lax.fori_loop(0, (o + DISPATCH_BLOCK - 1) // DISPATCH_BLOCK, block, 0)

    def tile_copy(src_row, e, dst_row):
        return pltpu.make_async_copy(x_ref.at[pl.ds(src_row, SUBLANES)], xin_ref.at[e, pl.ds(dst_row, SUBLANES)], sem)

    total = jnp.int32(0)
    for e in range(N_EXPERTS):
        n = cnt_ref[t * N_EXPERTS + e]
        s0 = s0_ref[t * N_EXPERTS + e]
        a0 = (s0 // SUBLANES) * SUBLANES
        span = s0 + n - a0
        full = jnp.where(n > 0, span // SUBLANES, 0)
        o_e = offs[e]

        @pl.when(n > 0)
        def _(e=e, a0=a0, span=span, full=full, o_e=o_e):
            head = pl.multiple_of(o_e, SUBLANES)
            x_ref[pl.ds(head, SUBLANES), :] += carry_ref[e]

            def issue(g, carry):
                tile_copy(pl.multiple_of(o_e + g * SUBLANES, SUBLANES), e,
                          pl.multiple_of(a0 + g * SUBLANES, SUBLANES)).start()
                return carry

            lax.fori_loop(0, full, issue, 0)
            tail = x_ref[pl.ds(pl.multiple_of(o_e + full * SUBLANES, SUBLANES), SUBLANES), :]
            carry_ref[e] = jnp.where(span - full * SUBLANES > 0, tail, 0.0)

        total = total + full

    def drain(g, carry):
        tile_copy(0, 0, 0).wait()
        return carry

    lax.fori_loop(0, total, drain, 0)


def moe_dispatch(h, posm_all, aff3, s0, cnt):
    return pl.pallas_call(
        _dispatch_kernel,
        out_shape=jax.ShapeDtypeStruct((N_EXPERTS, CAP_ALL, XIN_WIDTH), F32),
        grid_spec=pltpu.PrefetchScalarGridSpec(
            num_scalar_prefetch=2,
            grid=(N_ROUTE_TILES,),
            in_specs=[pl.BlockSpec((ROUTE_TILE, D_MODEL), lambda t, s0, cnt: (t, 0)),
                      pl.BlockSpec((N_EXPERTS, ROUTE_TILE), lambda t, s0, cnt: (0, t)),
                      pl.BlockSpec((ROUTE_TILE, LANES), lambda t, s0, cnt: (t, 0))],
            out_specs=pl.BlockSpec(memory_space=pl.ANY),
            scratch_shapes=[pltpu.VMEM((DISPATCH_ROWS, ROUTE_TILE), BF16),
                            pltpu.VMEM((DISPATCH_ROWS + SUBLANES, XIN_WIDTH), F32),
                            pltpu.VMEM((N_EXPERTS, SUBLANES, XIN_WIDTH), F32),
                            pltpu.SemaphoreType.DMA(())]),
        compiler_params=_cparams(("arbitrary",)),
        name="moe_dispatch",
    )(s0, cnt, h, posm_all, aff3)


def _ffn_up_kernel(x_ref, wg_ref, wu_ref, o_ref, ag_ref, au_ref):
    k = pl.program_id(1)
    x = x_ref[0].astype(BF16)
    pg = jnp.dot(x, wg_ref[0].astype(BF16), preferred_element_type=F32)
    pu = jnp.dot(x, wu_ref[0].astype(BF16), preferred_element_type=F32)

    @pl.when(k == 0)
    def _():
        ag_ref[...] = pg
        au_ref[...] = pu

    @pl.when(k > 0)
    def _():
        ag_ref[...] += pg
        au_ref[...] += pu

    @pl.when(k == pl.num_programs(1) - 1)
    def _():
        a = ag_ref[...]
        o_ref[0] = (a * jax.nn.sigmoid(a) * au_ref[...]).astype(o_ref.dtype)


def expert_ffn_up(xin, w_gate, w_up, layer):
    e, cap, _ = xin.shape
    d, ff = w_gate.shape[2:]
    tk = 1024
    return pl.pallas_call(
        _ffn_up_kernel,
        out_shape=jax.ShapeDtypeStruct((e, cap, ff), BF16),
        grid=(e, d // tk),
        in_specs=[pl.BlockSpec((1, cap, tk), lambda i, k: (i, 0, k)),
                  pl.BlockSpec((None, 1, tk, ff), lambda i, k: (layer, i, k, 0)),
                  pl.BlockSpec((None, 1, tk, ff), lambda i, k: (layer, i, k, 0))],
        out_specs=pl.BlockSpec((1, cap, ff), lambda i, k: (i, 0, 0)),
        scratch_shapes=[pltpu.VMEM((cap, ff), F32), pltpu.VMEM((cap, ff), F32)],
        compiler_params=_cparams(("arbitrary", "arbitrary"), vmem=62 * 1024 * 1024),
        name="expert_ffn_up",
    )(xin, w_gate, w_up)


def _ffn_down_kernel(a_ref, w_ref, g_ref, hi_ref, lo_ref):
    cap = a_ref.shape[1]
    g = g_ref[0]
    lane = lax.broadcasted_iota(I32, g.shape, 1)
    mine = jnp.logical_and(lane % N_EXPERTS == pl.program_id(0), lane < 3 * N_EXPERTS)
    gate = jnp.sum(jnp.where(mine, g, 0.0), axis=1, keepdims=True)
    y = jnp.dot(a_ref[0], w_ref[0].astype(BF16), preferred_element_type=F32) * gate
    hi = y.astype(BF16)
    hi_ref[0, 0:cap, :] = hi
    lo_ref[0, 0:cap, :] = (y - hi.astype(F32)).astype(BF16)
    pad = jnp.zeros((hi_ref.shape[1] - cap, hi_ref.shape[2]), BF16)
    hi_ref[0, cap:, :] = pad
    lo_ref[0, cap:, :] = pad


def expert_ffn_down(act, xin, w_down, layer):
    e, cap, ff = act.shape
    d = w_down.shape[3]
    tn = 1024
    out = jax.ShapeDtypeStruct((e, Y_ROWS, d), BF16)
    out_spec = pl.BlockSpec((1, Y_ROWS, tn), lambda i, j: (i, 0, j))
    return pl.pallas_call(
        _ffn_down_kernel,
        out_shape=[out, out],
        grid=(e, d // tn),
        in_specs=[pl.BlockSpec((1, cap, ff), lambda i, j: (i, 0, 0)),
                  pl.BlockSpec((None, 1, ff, tn), lambda i, j: (layer, i, 0, j)),
                  pl.BlockSpec((1, cap, LANES), lambda i, j: (i, 0, D_MODEL // LANES))],
        out_specs=[out_spec, out_spec],
        compiler_params=_cparams(("arbitrary", "arbitrary")),
        name="expert_ffn_down",
    )(act, w_down, xin)


COMBINE_PIECE = 32
COMBINE_ALIGN = 16
COMBINE_BLOCK = 256
COMBINE_PIECES_PER_BLOCK = COMBINE_BLOCK // COMBINE_PIECE
COMBINE_ROWS = N_EXPERTS * ((ROUTE_TILE + COMBINE_ALIGN - 1 + COMBINE_PIECE - 1) // COMBINE_PIECE) * COMBINE_PIECE
COMBINE_BLOCKS = COMBINE_ROWS // COMBINE_BLOCK


def _combine_kernel(s0_ref, cnt_ref, yh_ref, yl_ref, posm_ref, x_ref, gate_ref, lg_ref, lb_ref, *refs, final):
    if final:
        xo_ref, sth_ref, stl_ref, pt_ref, acc_ref, sems = refs
    else:
        sh_ref, sc_ref, xo_ref, h_ref, sth_ref, stl_ref, pt_ref, acc_ref, sems = refs
    t = pl.program_id(0)
    slot = t % 2
    iota = lax.broadcasted_iota(I32, (COMBINE_PIECE, ROUTE_TILE), 0)

    def window_copies(slot, e, w, k):
        r0 = pl.multiple_of(k * COMBINE_PIECE, COMBINE_PIECE)
        sem = sems.at[slot, k // COMBINE_PIECES_PER_BLOCK]
        return (pltpu.make_async_copy(yh_ref.at[e, pl.ds(w, COMBINE_PIECE)],
                                      sth_ref.at[slot, pl.ds(r0, COMBINE_PIECE)], sem),
                pltpu.make_async_copy(yl_ref.at[e, pl.ds(w, COMBINE_PIECE)],
                                      stl_ref.at[slot, pl.ds(r0, COMBINE_PIECE)], sem))

    def walk(tile, slot, fetch):
        kk = jnp.int32(0)
        for e in range(N_EXPERTS):
            n = cnt_ref[tile * N_EXPERTS + e]
            s0 = s0_ref[tile * N_EXPERTS + e]
            a0 = (s0 // COMBINE_ALIGN) * COMBINE_ALIGN
            npc = jnp.where(n > 0, (s0 + n - a0 + COMBINE_PIECE - 1) // COMBINE_PIECE, 0)
            prow = None if fetch else posm_ref[e:e + 1, :]

            def piece(p, carry, e=e, a0=a0, kk=kk, prow=prow):
                w = pl.multiple_of(a0 + p * COMBINE_PIECE, COMBINE_ALIGN)
                k = kk + p
                if fetch:
                    for cp in window_copies(slot, e, w, k):
                        cp.start()
                else:
                    r0 = pl.multiple_of(k * COMBINE_PIECE, COMBINE_PIECE)
                    pt_ref[pl.ds(r0, COMBINE_PIECE), :] = jnp.where(prow == (w + iota), 1.0, 0.0).astype(BF16)
                return carry

            lax.fori_loop(0, npc, piece, 0)
            kk = kk + npc
        return kk

    @pl.when(t == 0)
    def _():
        walk(t, slot, fetch=True)

    @pl.when(t + 1 < pl.num_programs(0))
    def _():
        walk(t + 1, 1 - slot, fetch=True)

    kk = walk(t, slot, fetch=False)
    nblk = (kk + COMBINE_PIECES_PER_BLOCK - 1) // COMBINE_PIECES_PER_BLOCK

    def zero_tail(p, carry):
        r0 = pl.multiple_of((kk + p) * COMBINE_PIECE, COMBINE_PIECE)
        sth_ref[slot, pl.ds(r0, COMBINE_PIECE), :] = jnp.zeros((COMBINE_PIECE, D_MODEL), BF16)
        stl_ref[slot, pl.ds(r0, COMBINE_PIECE), :] = jnp.zeros((COMBINE_PIECE, D_MODEL), BF16)
        pt_ref[pl.ds(r0, COMBINE_PIECE), :] = jnp.zeros((COMBINE_PIECE, ROUTE_TILE), BF16)
        return carry

    lax.fori_loop(0, nblk * COMBINE_PIECES_PER_BLOCK - kk, zero_tail, 0)
    acc_ref[...] = jnp.zeros(acc_ref.shape, F32)

    def block(b, carry):
        def drain(p, c):
            for cp in window_copies(slot, 0, 0, b * COMBINE_PIECES_PER_BLOCK):
                cp.wait()
            return c

        lax.fori_loop(0, jnp.minimum(kk - b * COMBINE_PIECES_PER_BLOCK, COMBINE_PIECES_PER_BLOCK), drain, 0)
        r0 = pl.multiple_of(b * COMBINE_BLOCK, COMBINE_BLOCK)
        onehot = pt_ref[pl.ds(r0, COMBINE_BLOCK), :]
        acc_ref[...] += (_tn_dot(onehot, sth_ref[slot, pl.ds(r0, COMBINE_BLOCK), :])
                         + _tn_dot(onehot, stl_ref[slot, pl.ds(r0, COMBINE_BLOCK), :]))
        return carry

    lax.fori_loop(0, nblk, block, 0)

    z = DN_ALPHA * x_ref[...] + gate_ref[0] * acc_ref[...]
    mu = jnp.mean(z, axis=-1, keepdims=True)
    zc = z - mu
    var = jnp.mean(zc * zc, axis=-1, keepdims=True)
    xn = zc * lax.rsqrt(var + LN_EPS) * lg_ref[...] + lb_ref[...]
    xo_ref[...] = xn
    if not final:
        h_ref[...] = (xn * (1.0 + sc_ref[0]) + sh_ref[0]).astype(h_ref.dtype)


def moe_combine_post_norm(yh, yl, posm_all, s0, cnt, x, gate, ln_g, ln_b, sh=None, sc=None):
    final = sh is None
    rows, n_tiles = (SEQ, N_ROUTE_LAT) if final else (M_ALL, N_ROUTE_TILES)
    tile = lambda width: pl.BlockSpec((ROUTE_TILE, width), lambda t, s0, cnt: (t, 0))
    route = pl.BlockSpec((N_EXPERTS, ROUTE_TILE), lambda t, s0, cnt: (0, t))
    stream_vec = pl.BlockSpec((1, 1, D_MODEL), lambda t, s0, cnt: (jnp.minimum(t // N_ROUTE_LAT, 1), 0, 0))
    vec = pl.BlockSpec((1, D_MODEL), lambda t, s0, cnt: (0, 0))
    in_specs = [pl.BlockSpec(memory_space=pl.ANY), pl.BlockSpec(memory_space=pl.ANY), route,
                tile(D_MODEL), stream_vec, vec, vec]
    args = [s0, cnt, yh, yl, posm_all, x, gate, ln_g.reshape(1, -1), ln_b.reshape(1, -1)]
    out_shape = [jax.ShapeDtypeStruct((rows, D_MODEL), F32)]
    if not final:
        in_specs += [stream_vec, stream_vec]
        args += [sh, sc]
        out_shape.append(jax.ShapeDtypeStruct((rows, D_MODEL), BF16))
    return pl.pallas_call(
        functools.partial(_combine_kernel, final=final),
        out_shape=out_shape,
        grid_spec=pltpu.PrefetchScalarGridSpec(
            num_scalar_prefetch=2,
            grid=(n_tiles,),
            in_specs=in_specs,
            out_specs=[tile(D_MODEL)] * len(out_shape),
            scratch_shapes=[pltpu.VMEM((2, COMBINE_ROWS, D_MODEL), BF16),
                            pltpu.VMEM((2, COMBINE_ROWS, D_MODEL), BF16),
                            pltpu.VMEM((COMBINE_ROWS, ROUTE_TILE), BF16),
                            pltpu.VMEM((ROUTE_TILE, D_MODEL), F32),
                            pltpu.SemaphoreType.DMA((2, COMBINE_BLOCKS))]),
        compiler_params=_cparams(("arbitrary",)),
        name="moe_combine_post_norm",
    )(*args)


def expert_choice_moe_post_norm(h, aff, aff3, w_gate, w_up, w_down, layer, x, gate, ln_g, ln_b, sh=None, sc=None):
    aff_t = aff[:, :N_EXPERTS].T
    posm_lat = expert_choice_select(aff_t[:, :SEQ], CAP_LAT)
    posm_ctx = expert_choice_select(aff_t[:, SEQ:], CAP_CTX)
    posm_all = jnp.concatenate([posm_lat, jnp.where(posm_ctx >= 0, posm_ctx + CAP_LAT, -1)], axis=1)
    s0, cnt = route_tile_meta(posm_all)
    xin = moe_dispatch(h, posm_all, aff3, s0, cnt)
    act = expert_ffn_up(xin, w_gate, w_up, layer)
    yh, yl = expert_ffn_down(act, xin, w_down, layer)
    return moe_combine_post_norm(yh, yl, posm_all, s0, cnt, x, gate, ln_g, ln_b, sh, sc)


def even_mixer(h, tabs, j, w_in, w_dw, b_dw, cn_g, cn_b, qa_g, kva_g, w_uq, w_ukv, w_o):
    u_main = matmul([h], w_in, F32, tn=768, layer=j, n_cols=IN_A_MAIN)
    w_kr = lax.slice(w_in, (j, 0, IN_A_MAIN), (j + 1, D_MODEL, IN_A))[0]
    w_kr = jnp.pad(w_kr, ((0, 0), (0, LANES - MLA_ROPE)))
    kr = matmul([h], w_kr, F32, tn=LANES)
    q_cat, k_cat, vt = mla_projections(u_main, kr, qa_g, kva_g, w_uq, w_ukv, tabs)
    att = mla_attention(q_cat, k_cat, vt)
    att_c = context_attention(q_cat, k_cat, vt, MLA_HEADS, MLA_QK_PAD, MLA_V, 0, 0, 0, 1.0, base2=True,
                              v_transposed=True)
    conv = conformer_conv(u_main, w_dw, b_dw, cn_g, cn_b)
    return matmul([conv, (att, att_c)], w_o, F32, tn=1024, tm=ROW_TILE, layer=j)


def odd_mixer(h, j, w_qkv, rpb, w_o):
    tn = 1024
    q_tiles = NA_HEADS * NA_HEAD_DIM // tn
    qkv = matmul([h], w_qkv, BF16, tn=tn, layer=j, lead_scale=(q_tiles, NA_HEAD_DIM ** -0.5 * LOG2_E))
    o = neighbourhood_attention(qkv, *na_bias_strips(rpb))
    o_c = context_attention(qkv, qkv, qkv, NA_HEADS, NA_HEAD_DIM, NA_HEAD_DIM, 0, NA_HEADS, 2 * NA_HEADS, 1.0,
                            base2=True)
    return matmul([(o, o_c)], w_o, F32, tn=1024, tm=ROW_TILE, layer=j)


def kernel(x, c, ctx, c_ctx, w_ada, b_ada, ln1_g, ln1_b, ln2_g, ln2_b, a_w_in, a_w_dw, a_b_dw, a_cn_g, a_cn_b,
           b_qa_g, b_kva_g, b_w_uq, b_w_ukv, ab_w_o, c_w_qkv, c_rpb, c_w_o, moe_w_router, moe_w_gate, moe_w_up,
           moe_w_down):
    assert x.shape == (1, SEQ, D_MODEL) and ctx.shape == (1, CTX_LEN, D_MODEL)
    xs = (x[0], ctx[0])
    cc = jnp.concatenate([c, c_ctx[None], jnp.zeros((SUBLANES - 2, D_MODEL), F32)], axis=0)
    mod = ada_modulation(cc, w_ada, b_ada)

    def vec(layer, k):
        return mod[layer, :2, k * D_MODEL:(k + 1) * D_MODEL].reshape(2, 1, D_MODEL)

    tabs = rope_tables()
    h = modulate_rows(*xs, vec(0, 0), vec(0, 1))
    for layer in range(DEPTH):
        j = layer // 2
        if layer % 2 == 0:
            o = even_mixer(h, tabs, j, a_w_in, a_w_dw[j], a_b_dw[j], a_cn_g[j], a_cn_b[j], b_qa_g[j], b_kva_g[j],
                           b_w_uq[j], b_w_ukv[j], ab_w_o)
        else:
            o = odd_mixer(h, j, c_w_qkv, c_rpb[j], c_w_o)
        xs, h2, aff, aff3 = post_norm_rows(xs, o, vec(layer, 2), ln1_g[layer], ln1_b[layer], vec(layer, 3), vec(layer, 4),
                                     moe_w_router[layer])
        moe = functools.partial(expert_choice_moe_post_norm, h2, aff, aff3, moe_w_gate, moe_w_up, moe_w_down, layer,
                                xs, vec(layer, 5), ln2_g[layer], ln2_b[layer])
        if layer + 1 < DEPTH:
            xs, h = moe(vec(layer + 1, 0), vec(layer + 1, 1))
        else:
            (xs,) = moe()
    return xs[None]
```

```python
import functools

import numpy as np
import jax
import jax.numpy as jnp
from jax import lax
from jax.experimental import pallas as pl
from jax.experimental.pallas import tpu as pltpu

F32 = jnp.float32
BF16 = jnp.bfloat16
I32 = jnp.int32

D_MODEL = 2048
SEQ = 8192
DEPTH = 4
GRID_W = 64
CTX_LEN = 256
M_ALL = SEQ + CTX_LEN

CONV_CH = 1024
CONV_WIDTH = 31
CONV_PAD = CONV_WIDTH // 2
MLA_HEADS = 8
MLA_Q_RANK = 512
MLA_KV_RANK = 512
MLA_NOPE = 128
MLA_ROPE = 64
MLA_V = 128
MLA_QK_PAD = 256
ROPE_BASE = 10000.0
NA_HEADS = 16
NA_HEAD_DIM = 128
NA_WIN_ROWS = 8
NA_WIN_COLS = 16
N_EXPERTS = 16
EXPERT_FF = 1408
EC_CAPACITY_FACTOR = 2
CAP_LAT = EC_CAPACITY_FACTOR * SEQ // N_EXPERTS
CAP_CTX = EC_CAPACITY_FACTOR * CTX_LEN // N_EXPERTS
CAP_ALL = CAP_LAT + CAP_CTX
IN_A = 2 * CONV_CH + MLA_Q_RANK + MLA_KV_RANK + MLA_ROPE
IN_A_MAIN = IN_A - MLA_ROPE
LOG2_E = 1.4426950408889634
DN_ALPHA = (2 * DEPTH) ** 0.25
LN_EPS = 1e-5
RMS_EPS = 1e-6
NEG_INF = -1e30

LANES = 128
SUBLANES = 8
VMEM_LIMIT = 56 * 1024 * 1024

ROW_TILE = 256
N_LAT_TILES = SEQ // ROW_TILE
N_ROW_TILES = M_ALL // ROW_TILE
MM_ROW_TILE = 768
ROUTE_TILE = 128
N_ROUTE_TILES = M_ALL // ROUTE_TILE
N_ROUTE_LAT = SEQ // ROUTE_TILE
Y_ROWS = CAP_ALL + 32


def _cparams(sem, vmem=VMEM_LIMIT):
    return pltpu.CompilerParams(dimension_semantics=sem, vmem_limit_bytes=vmem)


def _nt_dot(a, b):
    return lax.dot_general(a, b, (((1,), (1,)), ((), ())), preferred_element_type=F32)


def _tn_dot(a, b):
    return lax.dot_general(a, b, (((0,), (0,)), ((), ())), preferred_element_type=F32)


def _ada_kernel(c_ref, w_ref, b_ref, o_ref):
    c = c_ref[...]
    a = (c * jax.nn.sigmoid(c)).astype(BF16)
    o_ref[0] = jnp.dot(a, w_ref[0].astype(BF16), preferred_element_type=F32) + b_ref[0]


def ada_modulation(cc, w_ada, b_ada):
    depth, d, n = w_ada.shape
    tn = 1024
    return pl.pallas_call(
        _ada_kernel,
        out_shape=jax.ShapeDtypeStruct((depth, SUBLANES, n), F32),
        grid=(depth, n // tn),
        in_specs=[
            pl.BlockSpec((SUBLANES, d), lambda l, j: (0, 0)),
            pl.BlockSpec((1, d, tn), lambda l, j: (l, 0, j)),
            pl.BlockSpec((1, 1, tn), lambda l, j: (l, 0, j)),
        ],
        out_specs=pl.BlockSpec((1, SUBLANES, tn), lambda l, j: (l, 0, j)),
        compiler_params=_cparams(("arbitrary", "arbitrary")),
        name="ada_modulation",
    )(cc, w_ada, b_ada.reshape(depth, 1, n))


def _stream_vec_spec():
    return pl.BlockSpec((1, 1, D_MODEL), lambda i: (jnp.minimum(i // N_LAT_TILES, 1), 0, 0))


def _row_spec(width=D_MODEL):
    return pl.BlockSpec((ROW_TILE, width), lambda i: (i, 0))


def _stream_pair_specs(width=D_MODEL):
    return [pl.BlockSpec((ROW_TILE, width), lambda i: (jnp.minimum(i, N_LAT_TILES - 1), 0)),
            pl.BlockSpec((ROW_TILE, width), lambda i: (jnp.maximum(i - N_LAT_TILES, 0), 0))]


def _stream_pair_tile(lat_ref, ctx_ref):
    return jnp.where(pl.program_id(0) < N_LAT_TILES, lat_ref[...], ctx_ref[...])


def _modulate_kernel(xl_ref, xc_ref, sh_ref, sc_ref, h_ref):
    h_ref[...] = (_stream_pair_tile(xl_ref, xc_ref) * (1.0 + sc_ref[0]) + sh_ref[0]).astype(h_ref.dtype)


def modulate_rows(x_lat, x_ctx, sh, sc):
    return pl.pallas_call(
        _modulate_kernel,
        out_shape=jax.ShapeDtypeStruct((M_ALL, D_MODEL), BF16),
        grid=(N_ROW_TILES,),
        in_specs=_stream_pair_specs() + [_stream_vec_spec(), _stream_vec_spec()],
        out_specs=_row_spec(),
        compiler_params=_cparams(("arbitrary",)),
        name="modulate_rows",
    )(x_lat, x_ctx, sh, sc)


def _post_norm_kernel(*refs, with_router, x_paired):
    if x_paired:
        x = _stream_pair_tile(refs[0], refs[1])
        refs = refs[2:]
    else:
        x = refs[0][...]
        refs = refs[1:]
    if with_router:
        y_ref, gate_ref, lg_ref, lb_ref, sh_ref, sc_ref, wr_ref, xo_ref, h_ref, aff_ref, aff3_ref = refs
    else:
        y_ref, gate_ref, lg_ref, lb_ref, sh_ref, sc_ref, xo_ref, h_ref = refs
    z = DN_ALPHA * x + gate_ref[0] * y_ref[...]
    mu = jnp.mean(z, axis=-1, keepdims=True)
    zc = z - mu
    var = jnp.mean(zc * zc, axis=-1, keepdims=True)
    xn = zc * lax.rsqrt(var + LN_EPS) * lg_ref[...] + lb_ref[...]
    xo_ref[...] = xn
    h = (xn * (1.0 + sc_ref[0]) + sh_ref[0]).astype(BF16)
    h_ref[...] = h
    if with_router:
        logits = jnp.dot(h, wr_ref[...].astype(BF16), preferred_element_type=F32)
        lane = lax.broadcasted_iota(I32, logits.shape, 1)
        logits = jnp.where(lane < N_EXPERTS, logits, NEG_INF)
        e = jnp.exp(logits - jnp.max(logits, axis=-1, keepdims=True))
        aff = e / jnp.sum(e, axis=-1, keepdims=True)
        aff_ref[...] = aff
        hi = aff.astype(BF16).astype(F32)
        mid = (aff - hi).astype(BF16).astype(F32)
        lo = (aff - hi - mid).astype(BF16).astype(F32)
        aff3_ref[...] = (hi + pltpu.roll(mid, N_EXPERTS, 1) + pltpu.roll(lo, 2 * N_EXPERTS, 1)).astype(BF16)


def post_norm_rows(x, y, gate, ln_g, ln_b, sh, sc, w_router=None):
    with_router = w_router is not None
    x_paired = isinstance(x, tuple)
    vec = pl.BlockSpec((1, D_MODEL), lambda i: (0, 0))
    in_specs = (_stream_pair_specs() if x_paired else [_row_spec()]) + [
        _row_spec(), _stream_vec_spec(), vec, vec, _stream_vec_spec(), _stream_vec_spec()]
    out_shape = [jax.ShapeDtypeStruct(y.shape, F32), jax.ShapeDtypeStruct(y.shape, BF16)]
    out_specs = [_row_spec(), _row_spec()]
    args = (list(x) if x_paired else [x]) + [y, gate, ln_g.reshape(1, -1), ln_b.reshape(1, -1), sh, sc]
    if with_router:
        in_specs.append(pl.BlockSpec((D_MODEL, LANES), lambda i: (0, 0)))
        out_shape += [jax.ShapeDtypeStruct((y.shape[0], LANES), F32), jax.ShapeDtypeStruct((y.shape[0], LANES), BF16)]
        out_specs += [_row_spec(LANES), _row_spec(LANES)]
        args.append(jnp.pad(w_router, ((0, 0), (0, LANES - N_EXPERTS))))
    return pl.pallas_call(
        functools.partial(_post_norm_kernel, with_router=with_router, x_paired=x_paired),
        out_shape=out_shape,
        grid=(N_ROW_TILES,),
        in_specs=in_specs,
        out_specs=out_specs,
        compiler_params=_cparams(("arbitrary",)),
        name="post_norm_router" if with_router else "post_norm",
    )(*args)


def _matmul_kernel(*refs, splits, paired, lead_scale):
    n_refs = len(splits) + sum(paired)
    a_refs, w_ref, o_ref, wb_ref = list(refs[:n_refs]), refs[n_refs], refs[n_refs + 1], refs[n_refs + 2]

    @pl.when(pl.program_id(1) == 0)
    def _():
        wb_ref[...] = w_ref[...].astype(BF16)

    acc = None
    off = 0
    for k, pair in zip(splits, paired):
        a = a_refs.pop(0)[...]
        if pair:
            a = jnp.where(pl.program_id(1) < N_LAT_TILES, a, a_refs.pop(0)[...])
        part = jnp.dot(a.astype(BF16), wb_ref[off:off + k, :], preferred_element_type=F32)
        acc = part if acc is None else acc + part
        off += k
    if lead_scale is not None:
        n_tiles, value = lead_scale
        acc = acc * jnp.where(pl.program_id(0) < n_tiles, value, 1.0)
    o_ref[...] = acc.astype(o_ref.dtype)


def matmul(a_list, w, out_dtype, tn, tm=MM_ROW_TILE, layer=None, n_cols=None, lead_scale=None):
    paired = tuple(isinstance(a, tuple) for a in a_list)
    splits = tuple(a[0].shape[1] if p else a.shape[1] for a, p in zip(a_list, paired))
    m = M_ALL if any(paired) else a_list[0].shape[0]
    k, n = w.shape[-2:]
    n = n if n_cols is None else n_cols
    assert sum(splits) == k and n % tn == 0 and m % tm == 0 and (tm == ROW_TILE or not any(paired))
    in_specs, args = [], []
    for a, ki, p in zip(a_list, splits, paired):
        if p:
            in_specs += [pl.BlockSpec((tm, ki), lambda j, i: (jnp.minimum(i, N_LAT_TILES - 1), 0)),
                         pl.BlockSpec((tm, ki), lambda j, i: (0, 0))]
            args += list(a)
        else:
            in_specs.append(pl.BlockSpec((tm, ki), lambda j, i: (i, 0)))
            args.append(a)
    if layer is None:
        in_specs.append(pl.BlockSpec((k, tn), lambda j, i: (0, j)))
    else:
        in_specs.append(pl.BlockSpec((None, k, tn), lambda j, i: (layer, 0, j)))
    return pl.pallas_call(
        functools.partial(_matmul_kernel, splits=splits, paired=paired, lead_scale=lead_scale),
        out_shape=jax.ShapeDtypeStruct((m, n), out_dtype),
        grid=(n // tn, m // tm),
        in_specs=in_specs,
        out_specs=pl.BlockSpec((tm, tn), lambda j, i: (i, j)),
        scratch_shapes=[pltpu.VMEM((k, tn), BF16)],
        compiler_params=_cparams(("arbitrary", "arbitrary")),
        name="matmul",
    )(*args, w)


def rope_tables():
    t = jnp.arange(SEQ, dtype=I32)
    row = (t // GRID_W).astype(F32)
    col = (t % GRID_W).astype(F32)
    n_freq = MLA_ROPE // 4
    inv_freq = ROPE_BASE ** (-jnp.arange(n_freq, dtype=F32) / n_freq)
    ang = jnp.concatenate([row[:, None] * inv_freq, col[:, None] * inv_freq], axis=-1)
    cos, sin = jnp.cos(ang), jnp.sin(ang)
    half = MLA_ROPE // 2
    cos = jnp.concatenate([cos, jnp.ones((CTX_LEN, half), F32)], axis=0)
    sin = jnp.concatenate([sin, jnp.zeros((CTX_LEN, half), F32)], axis=0)
    z = jnp.zeros_like(cos)
    c_tab = jnp.concatenate([cos, cos, z, z], axis=1)
    sa_tab = jnp.concatenate([z, sin, z, z], axis=1)
    sb_tab = jnp.concatenate([-sin, z, z, z], axis=1)
    return c_tab, sa_tab, sb_tab


def _mla_proj_kernel(cq_ref, ckv_ref, kr_ref, qg_ref, kvg_ref, wq_ref, wkv_ref, c_ref, sa_ref, sb_ref,
                     q_out, k_out, vt_out, wqb, wkvb):
    @pl.when(pl.program_id(0) == 0)
    def _():
        wqb[...] = wq_ref[...].astype(BF16)
        wkvb[...] = wkv_ref[...].astype(BF16)

    def rms(x, g):
        return x * lax.rsqrt(jnp.mean(x * x, axis=-1, keepdims=True) + RMS_EPS) * g

    c_tab, sa_tab, sb_tab = c_ref[...], sa_ref[...], sb_ref[...]
    half = MLA_ROPE // 2

    def rope(g):
        return g * c_tab + pltpu.roll(g, half, 1) * sa_tab + pltpu.roll(g, LANES - half, 1) * sb_tab

    qk_scale = (MLA_NOPE + MLA_ROPE) ** -0.5 * LOG2_E
    q = jnp.dot(rms(cq_ref[...], qg_ref[...]).astype(BF16), wqb[...], preferred_element_type=F32) * qk_scale
    kv = jnp.dot(rms(ckv_ref[...], kvg_ref[...]).astype(BF16), wkvb[...], preferred_element_type=F32)
    k_rope = rope(kr_ref[...]).astype(BF16)
    for h in range(MLA_HEADS):
        lo = h * MLA_QK_PAD
        q_out[:, lo:lo + LANES] = q[:, lo:lo + LANES].astype(BF16)
        q_out[:, lo + LANES:lo + 2 * LANES] = rope(q[:, lo + LANES:lo + 2 * LANES]).astype(BF16)
        k_out[:, lo:lo + LANES] = kv[:, h * MLA_NOPE:(h + 1) * MLA_NOPE].astype(BF16)
        k_out[:, lo + LANES:lo + 2 * LANES] = k_rope
    vt_out[...] = kv[:, MLA_HEADS * MLA_NOPE:].T.astype(BF16)


def mla_projections(u_main, kr, qa_g, kva_g, w_uq, w_ukv, tabs):
    m = u_main.shape[0]
    tm = 384
    hq = MLA_HEADS * MLA_QK_PAD
    wq = w_uq.reshape(MLA_Q_RANK, MLA_HEADS, MLA_NOPE + MLA_ROPE)
    wq = jnp.pad(wq, ((0, 0), (0, 0), (0, MLA_QK_PAD - MLA_NOPE - MLA_ROPE))).reshape(MLA_Q_RANK, hq)
    wkv = w_ukv.reshape(MLA_KV_RANK, MLA_HEADS, MLA_NOPE + MLA_V)
    wkv = jnp.concatenate([wkv[..., :MLA_NOPE].reshape(MLA_KV_RANK, -1), wkv[..., MLA_NOPE:].reshape(MLA_KV_RANK, -1)], axis=1)
    nkv = wkv.shape[1]
    cq_blk = 2 * CONV_CH // MLA_Q_RANK
    tab_spec = pl.BlockSpec((tm, LANES), lambda i: (i, 0))
    return pl.pallas_call(
        _mla_proj_kernel,
        out_shape=[jax.ShapeDtypeStruct((m, hq), BF16), jax.ShapeDtypeStruct((m, hq), BF16),
                   jax.ShapeDtypeStruct((MLA_HEADS * MLA_V, m), BF16)],
        grid=(m // tm,),
        in_specs=[
            pl.BlockSpec((tm, MLA_Q_RANK), lambda i: (i, cq_blk)),
            pl.BlockSpec((tm, MLA_KV_RANK), lambda i: (i, cq_blk + 1)),
            pl.BlockSpec((tm, LANES), lambda i: (i, 0)),
            pl.BlockSpec((1, MLA_Q_RANK), lambda i: (0, 0)),
            pl.BlockSpec((1, MLA_KV_RANK), lambda i: (0, 0)),
            pl.BlockSpec((MLA_Q_RANK, hq), lambda i: (0, 0)),
            pl.BlockSpec((MLA_KV_RANK, nkv), lambda i: (0, 0)),
            tab_spec, tab_spec, tab_spec,
        ],
        out_specs=[pl.BlockSpec((tm, hq), lambda i: (i, 0)), pl.BlockSpec((tm, hq), lambda i: (i, 0)),
                   pl.BlockSpec((MLA_HEADS * MLA_V, tm), lambda i: (0, i))],
        scratch_shapes=[pltpu.VMEM((MLA_Q_RANK, hq), BF16), pltpu.VMEM((MLA_KV_RANK, nkv), BF16)],
        compiler_params=_cparams(("arbitrary",)),
        name="mla_projections",
    )(u_main, u_main, kr, qa_g.reshape(1, -1), kva_g.reshape(1, -1), wq, wkv, *tabs)


MLA_KEY_CHUNK = 1408
MLA_Q_TILE = 1024


def _flash_kernel(q_ref, k_ref, vt_ref, o_ref, *, tk, n_chunks):
    q = q_ref[...]
    tq = q.shape[0]
    m = jnp.full((1, tq), NEG_INF, F32)
    l = jnp.zeros((1, tq), F32)
    acc = jnp.zeros((vt_ref.shape[0], tq), F32)
    s_next = _nt_dot(k_ref[0:tk, :], q)
    for c in range(n_chunks):
        s = s_next
        if c + 1 < n_chunks:
            s_next = _nt_dot(k_ref[(c + 1) * tk:(c + 2) * tk, :], q)
        m_new = jnp.maximum(m, jnp.max(s, axis=0, keepdims=True))
        alpha = jnp.exp2(m - m_new)
        p = jnp.exp2(s - m_new)
        l = alpha * l + jnp.sum(p, axis=0, keepdims=True)
        acc = alpha * acc + jnp.dot(vt_ref[:, c * tk:(c + 1) * tk], p.astype(BF16), preferred_element_type=F32)
        m = m_new
    o_ref[...] = (acc / l).T.astype(o_ref.dtype)


def mla_attention(q_cat, k_cat, vt):
    m = k_cat.shape[0]
    tq, tk = MLA_Q_TILE, MLA_KEY_CHUNK
    return pl.pallas_call(
        functools.partial(_flash_kernel, tk=tk, n_chunks=m // tk),
        out_shape=jax.ShapeDtypeStruct((SEQ, MLA_HEADS * MLA_V), BF16),
        grid=(MLA_HEADS, SEQ // tq),
        in_specs=[
            pl.BlockSpec((tq, MLA_QK_PAD), lambda h, i: (i, h)),
            pl.BlockSpec((m, MLA_QK_PAD), lambda h, i: (0, h)),
            pl.BlockSpec((MLA_V, m), lambda h, i: (h, 0)),
        ],
        out_specs=pl.BlockSpec((tq, MLA_V), lambda h, i: (i, h)),
        compiler_params=_cparams(("arbitrary", "arbitrary")),
        name="mla_attention",
    )(q_cat, k_cat, vt)


def _ctx_attn_kernel(q_ref, k_ref, v_ref, o_ref, *, scale, base2, v_transposed):
    s = _nt_dot(q_ref[...], k_ref[...]) * scale
    z = s - jnp.max(s, axis=-1, keepdims=True)
    p = jnp.exp2(z) if base2 else jnp.exp(z)
    l = jnp.sum(p, axis=-1, keepdims=True)
    pv = _nt_dot(p.astype(BF16), v_ref[...]) if v_transposed else jnp.dot(p.astype(BF16), v_ref[...],
                                                                         preferred_element_type=F32)
    o_ref[...] = (pv / l).astype(o_ref.dtype)


def context_attention(q_arr, k_arr, v_arr, heads, dq, dv, q_col, k_col, v_col, scale, base2=False,
                      v_transposed=False):
    rb = SEQ // CTX_LEN
    if v_transposed:
        v_spec = pl.BlockSpec((dv, CTX_LEN), lambda h: (v_col + h, rb))
    else:
        v_spec = pl.BlockSpec((CTX_LEN, dv), lambda h: (rb, v_col + h))
    return pl.pallas_call(
        functools.partial(_ctx_attn_kernel, scale=scale, base2=base2, v_transposed=v_transposed),
        out_shape=jax.ShapeDtypeStruct((CTX_LEN, heads * dv), BF16),
        grid=(heads,),
        in_specs=[
            pl.BlockSpec((CTX_LEN, dq), lambda h: (rb, q_col + h)),
            pl.BlockSpec((CTX_LEN, dq), lambda h: (rb, k_col + h)),
            v_spec,
        ],
        out_specs=pl.BlockSpec((CTX_LEN, dv), lambda h: (0, h)),
        compiler_params=_cparams(("arbitrary",)),
        name="context_attention",
    )(q_arr, k_arr, v_arr)


CONV_HALO = 16
CONV_ROW_BLOCK = 64


def _conv_kernel(a_ref, g_ref, ap_ref, gp_ref, an_ref, gn_ref, w_ref, b_ref, cg_ref, cb_ref, o_ref, hbuf, hshift,
                 cbuf):
    i = pl.program_id(0)
    tl = a_ref.shape[0]

    def glu(a, g):
        return a[...] * jax.nn.sigmoid(g[...])

    has_prev = jnp.logical_and(i != 0, i != N_LAT_TILES)
    has_next = jnp.logical_and(i != N_LAT_TILES - 1, i != N_ROW_TILES - 1)
    hbuf[0:CONV_HALO, :] = jnp.where(has_prev, glu(ap_ref, gp_ref), 0.0)
    hbuf[CONV_HALO:CONV_HALO + tl, :] = glu(a_ref, g_ref)
    hbuf[CONV_HALO + tl:, :] = jnp.where(has_next, glu(an_ref, gn_ref), 0.0)

    span = hshift.shape[1]
    for s in range(SUBLANES):
        hshift[s] = hbuf[s:s + span, :]

    base = CONV_HALO - CONV_PAD
    for rb in range(tl // CONV_ROW_BLOCK):
        r0 = rb * CONV_ROW_BLOCK
        for c in range(CONV_CH // LANES):
            cs = slice(c * LANES, (c + 1) * LANES)
            acc = jnp.broadcast_to(b_ref[:, cs], (CONV_ROW_BLOCK, LANES))
            for j in range(CONV_WIDTH):
                s = (base + j) % SUBLANES
                a = r0 + base + j - s
                acc = acc + hshift[s, a:a + CONV_ROW_BLOCK, cs] * w_ref[j:j + 1, cs]
            cbuf[r0:r0 + CONV_ROW_BLOCK, cs] = acc

    y = cbuf[...]
    mu = jnp.mean(y, axis=-1, keepdims=True)
    yc = y - mu
    var = jnp.mean(yc * yc, axis=-1, keepdims=True)
    yn = yc * lax.rsqrt(var + LN_EPS) * cg_ref[...] + cb_ref[...]
    o_ref[...] = (yn * jax.nn.sigmoid(yn)).astype(o_ref.dtype)


def conformer_conv(u_main, w_dw, b_dw, cn_g, cn_b):
    m = u_main.shape[0]
    tl = ROW_TILE
    hpt = tl // CONV_HALO
    n_halo = m // CONV_HALO
    main = lambda col: pl.BlockSpec((tl, CONV_CH), lambda i: (i, col))
    prev = lambda col: pl.BlockSpec((CONV_HALO, CONV_CH), lambda i: (jnp.maximum(i * hpt - 1, 0), col))
    nxt = lambda col: pl.BlockSpec((CONV_HALO, CONV_CH), lambda i: (jnp.minimum((i + 1) * hpt, n_halo - 1), col))
    vec = pl.BlockSpec((1, CONV_CH), lambda i: (0, 0))
    return pl.pallas_call(
        _conv_kernel,
        out_shape=jax.ShapeDtypeStruct((m, CONV_CH), BF16),
        grid=(m // tl,),
        in_specs=[main(0), main(1), prev(0), prev(1), nxt(0), nxt(1),
                  pl.BlockSpec((CONV_WIDTH, CONV_CH), lambda i: (0, 0)), vec, vec, vec],
        out_specs=pl.BlockSpec((tl, CONV_CH), lambda i: (i, 0)),
        scratch_shapes=[pltpu.VMEM((tl + 2 * CONV_HALO, CONV_CH), F32),
                        pltpu.VMEM((SUBLANES, tl + 2 * CONV_HALO - SUBLANES, CONV_CH), F32),
                        pltpu.VMEM((tl, CONV_CH), F32)],
        compiler_params=_cparams(("arbitrary",)),
        name="conformer_conv",
    )(u_main, u_main, u_main, u_main, u_main, u_main, w_dw, b_dw.reshape(1, -1), cn_g.reshape(1, -1), cn_b.reshape(1, -1))


NA_Q_ROWS = 8
NA_K_ROWS = 16
NA_KBLK = 4 * GRID_W
NA_QTOK = NA_Q_ROWS * GRID_W
NA_KTOK = NA_K_ROWS * GRID_W
NA_GROUPS = SEQ // NA_QTOK


NA_HEADS_PER_STEP = 2
NA_CLASSES = (0, 1, NA_GROUPS - 1)


def _na_window_rows():
    rows = SEQ // GRID_W
    out = []
    for g in NA_CLASSES:
        qr = (NA_Q_ROWS * g + np.arange(NA_Q_ROWS))[:, None]
        kr = (NA_Q_ROWS * g - (NA_K_ROWS - NA_Q_ROWS) // 2 + np.arange(NA_K_ROWS))[None, :]
        start = np.clip(qr - NA_WIN_ROWS // 2, 0, rows - NA_WIN_ROWS)
        out.append((kr >= start) & (kr < start + NA_WIN_ROWS))
    return np.stack(out)


def na_bias_strips(rpb):
    qc = np.arange(GRID_W)[:, None]
    kc = np.arange(GRID_W)[None, :]
    ws = np.clip(qc - NA_WIN_COLS // 2, 0, GRID_W - NA_WIN_COLS)
    col_valid = (kc >= ws) & (kc < ws + NA_WIN_COLS)
    col_idx = np.clip(kc - qc + NA_WIN_COLS - 1, 0, 2 * NA_WIN_COLS - 2)
    onehot = jnp.asarray((col_idx[None] == np.arange(2 * NA_WIN_COLS - 1)[:, None, None]).astype(np.float32))
    cm = jnp.einsum('hrd,dqk->hrqk', rpb, onehot, precision=lax.Precision.HIGHEST) * LOG2_E
    cm = jnp.where(col_valid[None, None], cm, NEG_INF)
    neg = jnp.full_like(cm, NEG_INF)
    return jnp.concatenate([cm, neg], axis=-1), jnp.concatenate([neg, cm], axis=-1)


def _na_kernel(q_ref, k0, k1, k2, k3, v0, v1, v2, v3, kc_ref, vc_ref, bl_ref, br_ref, o_ref, tbl_ref):
    g = pl.program_id(1)
    row_valid = _na_window_rows()

    @pl.when(g == 0)
    def _():
        neg_tile = jnp.full((GRID_W, LANES), NEG_INF, F32)
        for hh in range(NA_HEADS_PER_STEP):
            for c in range(len(NA_CLASSES)):
                for ql in range(NA_Q_ROWS):
                    for kp in range(NA_K_ROWS // 2):
                        ka, kb = 2 * kp, 2 * kp + 1
                        da = ka - ql + NA_WIN_ROWS - 1 - (NA_K_ROWS - NA_Q_ROWS) // 2
                        ta = bl_ref[hh, da] if row_valid[c, ql, ka] else neg_tile
                        tb = br_ref[hh, da + 1] if row_valid[c, ql, kb] else neg_tile
                        tbl_ref[hh, c, ql * GRID_W:(ql + 1) * GRID_W, kp * LANES:(kp + 1) * LANES] = jnp.maximum(ta, tb)

    cls = jnp.where(g == 0, 0, jnp.where(g == NA_GROUPS - 1, 2, 1))
    for hh in range(NA_HEADS_PER_STEP):
        hs = slice(hh * NA_HEAD_DIM, (hh + 1) * NA_HEAD_DIM)
        q = q_ref[:, hs]
        s = jnp.concatenate([_nt_dot(q, k[:, hs]) for k in (k0, k1, k2, k3)], axis=1) + tbl_ref[hh, cls]
        sc = _nt_dot(q, kc_ref[:, hs])
        m = jnp.maximum(jnp.max(s, axis=-1, keepdims=True), jnp.max(sc, axis=-1, keepdims=True))
        p = jnp.exp2(s - m)
        pc = jnp.exp2(sc - m)
        l = jnp.sum(p, axis=-1, keepdims=True) + jnp.sum(pc, axis=-1, keepdims=True)
        o = jnp.dot(pc.astype(BF16), vc_ref[:, hs], preferred_element_type=F32)
        for j, v in enumerate((v0, v1, v2, v3)):
            o = o + jnp.dot(p[:, j * NA_KBLK:(j + 1) * NA_KBLK].astype(BF16), v[:, hs], preferred_element_type=F32)
        o_ref[:, hs] = (o / l).astype(o_ref.dtype)


def neighbourhood_attention(qkv, bias_left, bias_right):
    n_kblk = SEQ // NA_KBLK
    ctx_blk = SEQ // CTX_LEN
    hp = NA_HEADS_PER_STEP
    width = hp * NA_HEAD_DIM
    n_pairs = NA_HEADS // hp

    def kv_spec(j, col0):
        return pl.BlockSpec((NA_KBLK, width), lambda h, g: (jnp.clip(2 * g - 1 + j, 0, n_kblk - 1), col0 + h))

    strip_spec = pl.BlockSpec((hp, 2 * NA_WIN_ROWS - 1, GRID_W, LANES), lambda h, g: (h, 0, 0, 0))
    return pl.pallas_call(
        _na_kernel,
        out_shape=jax.ShapeDtypeStruct((SEQ, NA_HEADS * NA_HEAD_DIM), BF16),
        grid=(n_pairs, NA_GROUPS),
        in_specs=[pl.BlockSpec((NA_QTOK, width), lambda h, g: (g, h))]
        + [kv_spec(j, n_pairs) for j in range(4)]
        + [kv_spec(j, 2 * n_pairs) for j in range(4)]
        + [pl.BlockSpec((CTX_LEN, width), lambda h, g: (ctx_blk, n_pairs + h)),
           pl.BlockSpec((CTX_LEN, width), lambda h, g: (ctx_blk, 2 * n_pairs + h)),
           strip_spec, strip_spec],
        out_specs=pl.BlockSpec((NA_QTOK, width), lambda h, g: (g, h)),
        scratch_shapes=[pltpu.VMEM((hp, len(NA_CLASSES), NA_QTOK, NA_KTOK), F32)],
        compiler_params=_cparams(("arbitrary", "arbitrary")),
        name="neighbourhood_attention",
    )(*([qkv] * 11), bias_left, bias_right)


def _select_kernel(aff_ref, posm_ref, *, cap):
    bits = pltpu.bitcast(aff_ref[...], I32)
    n = bits.shape[1]

    def search(i, thr):
        cand = thr | (jnp.int32(1) << (30 - i))
        cnt = jnp.sum((bits >= cand).astype(I32), axis=1, keepdims=True)
        return jnp.where(cnt >= cap, cand, thr)

    thr = lax.fori_loop(0, 31, search, jnp.zeros((N_EXPERTS, 1), I32))
    gt = bits > thr
    eq = bits == thr
    need = (cap - jnp.sum(gt.astype(I32), axis=1, keepdims=True)).astype(F32)
    tri = jnp.where(lax.broadcasted_iota(I32, (LANES, LANES), 0) <= lax.broadcasted_iota(I32, (LANES, LANES), 1),
                    1.0, 0.0).astype(BF16)
    off_eq = jnp.zeros((N_EXPERTS, 1), F32)
    off_sel = jnp.zeros((N_EXPERTS, 1), F32)
    for b in range(n // LANES):
        cs = slice(b * LANES, (b + 1) * LANES)
        eq_b = jnp.where(eq[:, cs], 1.0, 0.0)
        inc_eq = jnp.dot(eq_b.astype(BF16), tri, preferred_element_type=F32)
        rank = off_eq + inc_eq - eq_b
        sel_b = jnp.logical_or(gt[:, cs], jnp.logical_and(eq[:, cs], rank < need))
        sel_f = jnp.where(sel_b, 1.0, 0.0)
        inc_sel = jnp.dot(sel_f.astype(BF16), tri, preferred_element_type=F32)
        pos = off_sel + inc_sel - sel_f
        posm_ref[:, cs] = jnp.where(sel_b, pos.astype(I32), -1)
        off_eq = off_eq + inc_eq[:, LANES - 1:LANES]
        off_sel = off_sel + inc_sel[:, LANES - 1:LANES]


def expert_choice_select(aff_t, cap):
    return pl.pallas_call(
        functools.partial(_select_kernel, cap=cap),
        out_shape=jax.ShapeDtypeStruct(aff_t.shape, I32),
        compiler_params=_cparams(None),
        name="expert_choice_select",
    )(aff_t)


def route_tile_meta(posm_all):
    cnt = (posm_all >= 0).astype(I32).reshape(N_EXPERTS, N_ROUTE_TILES, ROUTE_TILE).sum(-1).T
    lat, ctx = cnt[:N_ROUTE_LAT], cnt[N_ROUTE_LAT:]
    s0 = jnp.concatenate([jnp.cumsum(lat, 0) - lat, CAP_LAT + jnp.cumsum(ctx, 0) - ctx], axis=0)
    return s0.reshape(-1), cnt.reshape(-1)


DISPATCH_PIECE = 16
DISPATCH_BLOCK = 256
DISPATCH_ROWS = N_EXPERTS * ((ROUTE_TILE + SUBLANES - 1 + DISPATCH_PIECE - 1) // DISPATCH_PIECE) * DISPATCH_PIECE


XIN_WIDTH = D_MODEL + LANES


def _dispatch_kernel(s0_ref, cnt_ref, h_ref, posm_ref, aff3_ref, xin_ref, pt_ref, x_ref, carry_ref, sem):
    t = pl.program_id(0)

    @pl.when(t == 0)
    def _():
        carry_ref[...] = jnp.zeros(carry_ref.shape, F32)

    pt_ref[...] = jnp.zeros(pt_ref.shape, pt_ref.dtype)
    iota = lax.broadcasted_iota(I32, (DISPATCH_PIECE, ROUTE_TILE), 0)
    offs = []
    o = jnp.int32(0)
    for e in range(N_EXPERTS):
        n = cnt_ref[t * N_EXPERTS + e]
        s0 = s0_ref[t * N_EXPERTS + e]
        a0 = (s0 // SUBLANES) * SUBLANES
        npc = jnp.where(n > 0, (s0 + n - a0 + DISPATCH_PIECE - 1) // DISPATCH_PIECE, 0)
        row = posm_ref[e:e + 1, :]
        offs.append(o)

        def piece(p, carry, row=row, a0=a0, o=o):
            r0 = pl.multiple_of(o + p * DISPATCH_PIECE, DISPATCH_PIECE)
            hit = row == (a0 + p * DISPATCH_PIECE + iota)
            pt_ref[pl.ds(r0, DISPATCH_PIECE), :] = jnp.where(hit, 1.0, 0.0).astype(BF16)
            return carry

        lax.fori_loop(0, npc, piece, 0)
        o = o + npc * DISPATCH_PIECE

    def block(b, carry):
        r0 = pl.multiple_of(b * DISPATCH_BLOCK, DISPATCH_BLOCK)
        onehot = pt_ref[pl.ds(r0, DISPATCH_BLOCK), :]
        x_ref[pl.ds(r0, DISPATCH_BLOCK), 0:D_MODEL] = jnp.dot(onehot, h_ref[...], preferred_element_type=F32)
        x_ref[pl.ds(r0, DISPATCH_BLOCK), D_MODEL:] = jnp.dot(onehot, aff3_ref[...], preferred_element_type=F32)
        return carry

    lax.fori_loop(0, (o + DISPATCH_BLOCK - 1) // DISPATCH_BLOCK, block, 0)

    def tile_copy(src_row, e, dst_row):
        return pltpu.make_async_copy(x_ref.at[pl.ds(src_row, SUBLANES)], xin_ref.at[e, pl.ds(dst_row, SUBLANES)], sem)

    total = jnp.int32(0)
    for e in range(N_EXPERTS):
        n = cnt_ref[t * N_EXPERTS + e]
        s0 = s0_ref[t * N_EXPERTS + e]
        a0 = (s0 // SUBLANES) * SUBLANES
        span = s0 + n - a0
        full = jnp.where(n > 0, span // SUBLANES, 0)
        o_e = offs[e]

        @pl.when(n > 0)
        def _(e=e, a0=a0, span=span, full=full, o_e=o_e):
            head = pl.multiple_of(o_e, SUBLANES)
            x_ref[pl.ds(head, SUBLANES), :] += carry_ref[e]

            def issue(g, carry):
                tile_copy(pl.multiple_of(o_e + g * SUBLANES, SUBLANES), e,
                          pl.multiple_of(a0 + g * SUBLANES, SUBLANES)).start()
                return carry

            lax.fori_loop(0, full, issue, 0)
            tail = x_ref[pl.ds(pl.multiple_of(o_e + full * SUBLANES, SUBLANES), SUBLANES), :]
            carry_ref[e] = jnp.where(span - full * SUBLANES > 0, tail, 0.0)

        total = total + full

    def drain(g, carry):
        tile_copy(0, 0, 0).wait()
        return carry

    lax.fori_loop(0, total, drain, 0)


def moe_dispatch(h, posm_all, aff3, s0, cnt):
    return pl.pallas_call(
        _dispatch_kernel,
        out_shape=jax.ShapeDtypeStruct((N_EXPERTS, CAP_ALL, XIN_WIDTH), F32),
        grid_spec=pltpu.PrefetchScalarGridSpec(
            num_scalar_prefetch=2,
            grid=(N_ROUTE_TILES,),
            in_specs=[pl.BlockSpec((ROUTE_TILE, D_MODEL), lambda t, s0, cnt: (t, 0)),
                      pl.BlockSpec((N_EXPERTS, ROUTE_TILE), lambda t, s0, cnt: (0, t)),
                      pl.BlockSpec((ROUTE_TILE, LANES), lambda t, s0, cnt: (t, 0))],
            out_specs=pl.BlockSpec(memory_space=pl.ANY),
            scratch_shapes=[pltpu.VMEM((DISPATCH_ROWS, ROUTE_TILE), BF16),
                            pltpu.VMEM((DISPATCH_ROWS + SUBLANES, XIN_WIDTH), F32),
                            pltpu.VMEM((N_EXPERTS, SUBLANES, XIN_WIDTH), F32),
                            pltpu.SemaphoreType.DMA(())]),
        compiler_params=_cparams(("arbitrary",)),
        name="moe_dispatch",
    )(s0, cnt, h, posm_all, aff3)


def _ffn_up_kernel(x_ref, wg_ref, wu_ref, o_ref, ag_ref, au_ref):
    k = pl.program_id(1)
    x = x_ref[0].astype(BF16)
    pg = jnp.dot(x, wg_ref[0].astype(BF16), preferred_element_type=F32)
    pu = jnp.dot(x, wu_ref[0].astype(BF16), preferred_element_type=F32)

    @pl.when(k == 0)
    def _():
        ag_ref[...] = pg
        au_ref[...] = pu

    @pl.when(k > 0)
    def _():
        ag_ref[...] += pg
        au_ref[...] += pu

    @pl.when(k == pl.num_programs(1) - 1)
    def _():
        a = ag_ref[...]
        o_ref[0] = (a * jax.nn.sigmoid(a) * au_ref[...]).astype(o_ref.dtype)


def expert_ffn_up(xin, w_gate, w_up, layer):
    e, cap, _ = xin.shape
    d, ff = w_gate.shape[2:]
    tk = 1024
    return pl.pallas_call(
        _ffn_up_kernel,
        out_shape=jax.ShapeDtypeStruct((e, cap, ff), BF16),
        grid=(e, d // tk),
        in_specs=[pl.BlockSpec((1, cap, tk), lambda i, k: (i, 0, k)),
                  pl.BlockSpec((None, 1, tk, ff), lambda i, k: (layer, i, k, 0)),
                  pl.BlockSpec((None, 1, tk, ff), lambda i, k: (layer, i, k, 0))],
        out_specs=pl.BlockSpec((1, cap, ff), lambda i, k: (i, 0, 0)),
        scratch_shapes=[pltpu.VMEM((cap, ff), F32), pltpu.VMEM((cap, ff), F32)],
        compiler_params=_cparams(("arbitrary", "arbitrary"), vmem=62 * 1024 * 1024),
        name="expert_ffn_up",
    )(xin, w_gate, w_up)


def _ffn_down_kernel(a_ref, w_ref, g_ref, hi_ref, lo_ref):
    cap = a_ref.shape[1]
    g = g_ref[0]
    lane = lax.broadcasted_iota(I32, g.shape, 1)
    mine = jnp.logical_and(lane % N_EXPERTS == pl.program_id(0), lane < 3 * N_EXPERTS)
    gate = jnp.sum(jnp.where(mine, g, 0.0), axis=1, keepdims=True)
    y = jnp.dot(a_ref[0], w_ref[0].astype(BF16), preferred_element_type=F32) * gate
    hi = y.astype(BF16)
    hi_ref[0, 0:cap, :] = hi
    lo_ref[0, 0:cap, :] = (y - hi.astype(F32)).astype(BF16)
    pad = jnp.zeros((hi_ref.shape[1] - cap, hi_ref.shape[2]), BF16)
    hi_ref[0, cap:, :] = pad
    lo_ref[0, cap:, :] = pad


def expert_ffn_down(act, xin, w_down, layer):
    e, cap, ff = act.shape
    d = w_down.shape[3]
    tn = 2048
    out = jax.ShapeDtypeStruct((e, Y_ROWS, d), BF16)
    out_spec = pl.BlockSpec((1, Y_ROWS, tn), lambda i, j: (i, 0, j))
    return pl.pallas_call(
        _ffn_down_kernel,
        out_shape=[out, out],
        grid=(e, d // tn),
        in_specs=[pl.BlockSpec((1, cap, ff), lambda i, j: (i, 0, 0)),
                  pl.BlockSpec((None, 1, ff, tn), lambda i, j: (layer, i, 0, j)),
                  pl.BlockSpec((1, cap, LANES), lambda i, j: (i, 0, D_MODEL // LANES))],
        out_specs=[out_spec, out_spec],
        compiler_params=_cparams(("arbitrary", "arbitrary")),
        name="expert_ffn_down",
    )(act, w_down, xin)


COMBINE_PIECE = 32
COMBINE_ALIGN = 16
COMBINE_BLOCK = 256
COMBINE_PIECES_PER_BLOCK = COMBINE_BLOCK // COMBINE_PIECE
COMBINE_ROWS = N_EXPERTS * ((ROUTE_TILE + COMBINE_ALIGN - 1 + COMBINE_PIECE - 1) // COMBINE_PIECE) * COMBINE_PIECE
COMBINE_BLOCKS = COMBINE_ROWS // COMBINE_BLOCK


def _combine_kernel(s0_ref, cnt_ref, yh_ref, yl_ref, posm_ref, x_ref, gate_ref, lg_ref, lb_ref, *refs, final):
    if final:
        xo_ref, sth_ref, stl_ref, pt_ref, acc_ref, sems = refs
    else:
        sh_ref, sc_ref, xo_ref, h_ref, sth_ref, stl_ref, pt_ref, acc_ref, sems = refs
    t = pl.program_id(0)
    slot = t % 2
    iota = lax.broadcasted_iota(I32, (COMBINE_PIECE, ROUTE_TILE), 0)

    def window_copies(slot, e, w, k):
        r0 = pl.multiple_of(k * COMBINE_PIECE, COMBINE_PIECE)
        sem = sems.at[slot, k // COMBINE_PIECES_PER_BLOCK]
        return (pltpu.make_async_copy(yh_ref.at[e, pl.ds(w, COMBINE_PIECE)],
                                      sth_ref.at[slot, pl.ds(r0, COMBINE_PIECE)], sem),
                pltpu.make_async_copy(yl_ref.at[e, pl.ds(w, COMBINE_PIECE)],
                                      stl_ref.at[slot, pl.ds(r0, COMBINE_PIECE)], sem))

    def walk(tile, slot, fetch):
        kk = jnp.int32(0)
        for e in range(N_EXPERTS):
            n = cnt_ref[tile * N_EXPERTS + e]
            s0 = s0_ref[tile * N_EXPERTS + e]
            a0 = (s0 // COMBINE_ALIGN) * COMBINE_ALIGN
            npc = jnp.where(n > 0, (s0 + n - a0 + COMBINE_PIECE - 1) // COMBINE_PIECE, 0)
            prow = None if fetch else posm_ref[e:e + 1, :]

            def piece(p, carry, e=e, a0=a0, kk=kk, prow=prow):
                w = pl.multiple_of(a0 + p * COMBINE_PIECE, COMBINE_ALIGN)
                k = kk + p
                if fetch:
                    for cp in window_copies(slot, e, w, k):
                        cp.start()
                else:
                    r0 = pl.multiple_of(k * COMBINE_PIECE, COMBINE_PIECE)
                    pt_ref[pl.ds(r0, COMBINE_PIECE), :] = jnp.where(prow == (w + iota), 1.0, 0.0).astype(BF16)
                return carry

            lax.fori_loop(0, npc, piece, 0)
            kk = kk + npc
        return kk

    @pl.when(t == 0)
    def _():
        walk(t, slot, fetch=True)

    @pl.when(t + 1 < pl.num_programs(0))
    def _():
        walk(t + 1, 1 - slot, fetch=True)

    kk = walk(t, slot, fetch=False)
    nblk = (kk + COMBINE_PIECES_PER_BLOCK - 1) // COMBINE_PIECES_PER_BLOCK

    def zero_tail(p, carry):
        r0 = pl.multiple_of((kk + p) * COMBINE_PIECE, COMBINE_PIECE)
        sth_ref[slot, pl.ds(r0, COMBINE_PIECE), :] = jnp.zeros((COMBINE_PIECE, D_MODEL), BF16)
        stl_ref[slot, pl.ds(r0, COMBINE_PIECE), :] = jnp.zeros((COMBINE_PIECE, D_MODEL), BF16)
        pt_ref[pl.ds(r0, COMBINE_PIECE), :] = jnp.zeros((COMBINE_PIECE, ROUTE_TILE), BF16)
        return carry

    lax.fori_loop(0, nblk * COMBINE_PIECES_PER_BLOCK - kk, zero_tail, 0)
    acc_ref[...] = jnp.zeros(acc_ref.shape, F32)

    def block(b, carry):
        def drain(p, c):
            for cp in window_copies(slot, 0, 0, b * COMBINE_PIECES_PER_BLOCK):
                cp.wait()
            return c

        lax.fori_loop(0, jnp.minimum(kk - b * COMBINE_PIECES_PER_BLOCK, COMBINE_PIECES_PER_BLOCK), drain, 0)
        r0 = pl.multiple_of(b * COMBINE_BLOCK, COMBINE_BLOCK)
        onehot = pt_ref[pl.ds(r0, COMBINE_BLOCK), :]
        acc_ref[...] += (_tn_dot(onehot, sth_ref[slot, pl.ds(r0, COMBINE_BLOCK), :])
                         + _tn_dot(onehot, stl_ref[slot, pl.ds(r0, COMBINE_BLOCK), :]))
        return carry

    lax.fori_loop(0, nblk, block, 0)

    z = DN_ALPHA * x_ref[...] + gate_ref[0] * acc_ref[...]
    mu = jnp.mean(z, axis=-1, keepdims=True)
    zc = z - mu
    var = jnp.mean(zc * zc, axis=-1, keepdims=True)
    xn = zc * lax.rsqrt(var + LN_EPS) * lg_ref[...] + lb_ref[...]
    xo_ref[...] = xn
    if not final:
        h_ref[...] = (xn * (1.0 + sc_ref[0]) + sh_ref[0]).astype(h_ref.dtype)


def moe_combine_post_norm(yh, yl, posm_all, s0, cnt, x, gate, ln_g, ln_b, sh=None, sc=None):
    final = sh is None
    rows, n_tiles = (SEQ, N_ROUTE_LAT) if final else (M_ALL, N_ROUTE_TILES)
    tile = lambda width: pl.BlockSpec((ROUTE_TILE, width), lambda t, s0, cnt: (t, 0))
    route = pl.BlockSpec((N_EXPERTS, ROUTE_TILE), lambda t, s0, cnt: (0, t))
    stream_vec = pl.BlockSpec((1, 1, D_MODEL), lambda t, s0, cnt: (jnp.minimum(t // N_ROUTE_LAT, 1), 0, 0))
    vec = pl.BlockSpec((1, D_MODEL), lambda t, s0, cnt: (0, 0))
    in_specs = [pl.BlockSpec(memory_space=pl.ANY), pl.BlockSpec(memory_space=pl.ANY), route,
                tile(D_MODEL), stream_vec, vec, vec]
    args = [s0, cnt, yh, yl, posm_all, x, gate, ln_g.reshape(1, -1), ln_b.reshape(1, -1)]
    out_shape = [jax.ShapeDtypeStruct((rows, D_MODEL), F32)]
    if not final:
        in_specs += [stream_vec, stream_vec]
        args += [sh, sc]
        out_shape.append(jax.ShapeDtypeStruct((rows, D_MODEL), BF16))
    return pl.pallas_call(
        functools.partial(_combine_kernel, final=final),
        out_shape=out_shape,
        grid_spec=pltpu.PrefetchScalarGridSpec(
            num_scalar_prefetch=2,
            grid=(n_tiles,),
            in_specs=in_specs,
            out_specs=[tile(D_MODEL)] * len(out_shape),
            scratch_shapes=[pltpu.VMEM((2, COMBINE_ROWS, D_MODEL), BF16),
                            pltpu.VMEM((2, COMBINE_ROWS, D_MODEL), BF16),
                            pltpu.VMEM((COMBINE_ROWS, ROUTE_TILE), BF16),
                            pltpu.VMEM((ROUTE_TILE, D_MODEL), F32),
                            pltpu.SemaphoreType.DMA((2, COMBINE_BLOCKS))]),
        compiler_params=_cparams(("arbitrary",)),
        name="moe_combine_post_norm",
    )(*args)


def expert_choice_moe_post_norm(h, aff, aff3, w_gate, w_up, w_down, layer, x, gate, ln_g, ln_b, sh=None, sc=None):
    aff_t = aff[:, :N_EXPERTS].T
    posm_lat = expert_choice_select(aff_t[:, :SEQ], CAP_LAT)
    posm_ctx = expert_choice_select(aff_t[:, SEQ:], CAP_CTX)
    posm_all = jnp.concatenate([posm_lat, jnp.where(posm_ctx >= 0, posm_ctx + CAP_LAT, -1)], axis=1)
    s0, cnt = route_tile_meta(posm_all)
    xin = moe_dispatch(h, posm_all, aff3, s0, cnt)
    act = expert_ffn_up(xin, w_gate, w_up, layer)
    yh, yl = expert_ffn_down(act, xin, w_down, layer)
    return moe_combine_post_norm(yh, yl, posm_all, s0, cnt, x, gate, ln_g, ln_b, sh, sc)


def even_mixer(h, tabs, j, w_in, w_dw, b_dw, cn_g, cn_b, qa_g, kva_g, w_uq, w_ukv, w_o):
    u_main = matmul([h], w_in, F32, tn=768, layer=j, n_cols=IN_A_MAIN)
    w_kr = lax.slice(w_in, (j, 0, IN_A_MAIN), (j + 1, D_MODEL, IN_A))[0]
    w_kr = jnp.pad(w_kr, ((0, 0), (0, LANES - MLA_ROPE)))
    kr = matmul([h], w_kr, F32, tn=LANES)
    q_cat, k_cat, vt = mla_projections(u_main, kr, qa_g, kva_g, w_uq, w_ukv, tabs)
    att = mla_attention(q_cat, k_cat, vt)
    att_c = context_attention(q_cat, k_cat, vt, MLA_HEADS, MLA_QK_PAD, MLA_V, 0, 0, 0, 1.0, base2=True,
                              v_transposed=True)
    conv = conformer_conv(u_main, w_dw, b_dw, cn_g, cn_b)
    return matmul([conv, (att, att_c)], w_o, F32, tn=D_MODEL, tm=ROW_TILE, layer=j)


def odd_mixer(h, j, w_qkv, rpb, w_o):
    tn = 1024
    q_tiles = NA_HEADS * NA_HEAD_DIM // tn
    qkv = matmul([h], w_qkv, BF16, tn=tn, layer=j, lead_scale=(q_tiles, NA_HEAD_DIM ** -0.5 * LOG2_E))
    o = neighbourhood_attention(qkv, *na_bias_strips(rpb))
    o_c = context_attention(qkv, qkv, qkv, NA_HEADS, NA_HEAD_DIM, NA_HEAD_DIM, 0, NA_HEADS, 2 * NA_HEADS, 1.0,
                            base2=True)
    return matmul([(o, o_c)], w_o, F32, tn=D_MODEL, tm=ROW_TILE, layer=j)


def kernel(x, c, ctx, c_ctx, w_ada, b_ada, ln1_g, ln1_b, ln2_g, ln2_b, a_w_in, a_w_dw, a_b_dw, a_cn_g, a_cn_b,
           b_qa_g, b_kva_g, b_w_uq, b_w_ukv, ab_w_o, c_w_qkv, c_rpb, c_w_o, moe_w_router, moe_w_gate, moe_w_up,
           moe_w_down):
    assert x.shape == (1, SEQ, D_MODEL) and ctx.shape == (1, CTX_LEN, D_MODEL)
    xs = (x[0], ctx[0])
    cc = jnp.concatenate([c, c_ctx[None], jnp.zeros((SUBLANES - 2, D_MODEL), F32)], axis=0)
    mod = ada_modulation(cc, w_ada, b_ada)

    def vec(layer, k):
        return mod[layer, :2, k * D_MODEL:(k + 1) * D_MODEL].reshape(2, 1, D_MODEL)

    tabs = rope_tables()
    h = modulate_rows(*xs, vec(0, 0), vec(0, 1))
    for layer in range(DEPTH):
        j = layer // 2
        if layer % 2 == 0:
            o = even_mixer(h, tabs, j, a_w_in, a_w_dw[j], a_b_dw[j], a_cn_g[j], a_cn_b[j], b_qa_g[j], b_kva_g[j],
                           b_w_uq[j], b_w_ukv[j], ab_w_o)
        else:
            o = odd_mixer(h, j, c_w_qkv, c_rpb[j], c_w_o)
        xs, h2, aff, aff3 = post_norm_rows(xs, o, vec(layer, 2), ln1_g[layer], ln1_b[layer], vec(layer, 3), vec(layer, 4),
                                     moe_w_router[layer])
        moe = functools.partial(expert_choice_moe_post_norm, h2, aff, aff3, moe_w_gate, moe_w_up, moe_w_down, layer,
                                xs, vec(layer, 5), ln2_g[layer], ln2_b[layer])
        if layer + 1 < DEPTH:
            xs, h = moe(vec(layer + 1, 0), vec(layer + 1, 1))
        else:
            (xs,) = moe()
    return xs[None]
```
